```python
import jax, jax.numpy as jnp
from jax import lax
import numpy as np

D_MODEL = 1024
BATCH = 16
SEQ = 256
DEPTH = 4
DEC_BATCH = 4
DEC_SEQ = 2048
PAST_LEN = 256

GRID_W = 64
EXPAND = 2
MIX_WIDTH = EXPAND * D_MODEL
BRANCH_W = MIX_WIDTH // 2
HEAD_DK = 128
HEAD_DV = 128
N_HEADS_A = BRANCH_W // HEAD_DV
CHUNK_A = 32
POOL_WINDOWS = (2, 4, 8, 16)
N_POOL_GROUPS = len(POOL_WINDOWS)
POOL_GROUP_W = BRANCH_W // N_POOL_GROUPS
CHUNK_C = 128
N_GROUPS_C = 8
GROUP_W_C = MIX_WIDTH // N_GROUPS_C
N_AB_LAYERS = (DEPTH + 1) // 2
N_C_LAYERS = DEPTH // 2
AB_IN = 7 * BRANCH_W
C_IN = 3 * MIX_WIDTH
EPS = 1e-6

kernel_name = "hybrid_hgrn2_pool_gmlp_diffusion_step"


def _rmsnorm(x, g):
    xf = x.astype(jnp.float32)
    y = xf * lax.rsqrt(jnp.mean(xf * xf, axis=-1, keepdims=True) + EPS)
    return (y * g.astype(jnp.float32)).astype(x.dtype)


def _modulation(cond, w_ada, b_ada):
    m = jax.nn.silu(cond) @ w_ada + b_ada
    shift, scale, gate = jnp.split(m, 3, axis=-1)
    return shift[:, None, :], scale[:, None, :], gate[:, None, :]


def _hgrn_chunk_scan(q, k, v, log_f, s0):
    b, h, t, dk = q.shape
    n = t // CHUNK_A

    def to_chunks(a):
        return a.reshape(b, h, n, CHUNK_A, a.shape[-1]).transpose(2, 0, 1, 3, 4)

    mask = jnp.tril(jnp.ones((CHUNK_A, CHUNK_A), dtype=bool))[:, :, None]

    def step(s, inp):
        qi, ki, vi, gi = inp
        bcum = jnp.cumsum(gi, axis=2)
        diff = bcum[:, :, :, None, :] - bcum[:, :, None, :, :]
        decay = jnp.exp(jnp.where(mask, diff, -jnp.inf))
        scores = jnp.einsum('bhtd,bhsd,bhtsd->bhts', qi, ki, decay)
        o = jnp.einsum('bhts,bhse->bhte', scores, vi) + jnp.einsum('bhtd,bhde->bhte', qi * jnp.exp(bcum), s)
        b_last = bcum[:, :, -1:, :]
        s_new = jnp.exp(b_last[:, :, 0, :])[..., None] * s + jnp.einsum('bhsd,bhse->bhde', ki * jnp.exp(b_last - bcum), vi)
        return s_new, o

    s_fin, oc = lax.scan(step, s0, (to_chunks(q), to_chunks(k), to_chunks(v), to_chunks(log_f)))
    o = oc.transpose(1, 2, 0, 3, 4).reshape(b, h, t, v.shape[-1])
    return o, s_fin


def _hgrn_bidir(q, i_in, f_pre_fwd, f_pre_bwd, lb, s0):
    bsz, t, _ = q.shape

    def heads(a, d):
        return a.reshape(bsz, t, N_HEADS_A, d).transpose(0, 2, 1, 3).astype(jnp.float32)

    qh = heads(jax.nn.silu(q), HEAD_DK)
    vh = heads(i_in, HEAD_DV)
    o_sum = None
    finals = []
    for d, f_pre in enumerate((f_pre_fwd, f_pre_bwd)):
        lbd = lb[d].astype(jnp.float32).reshape(1, N_HEADS_A, 1, HEAD_DK)
        f = lbd + (1.0 - lbd) * jax.nn.sigmoid(heads(f_pre, HEAD_DK))
        log_f = jnp.log(f)
        k = 1.0 - f
        qd, kd, vd, gd = qh, k, vh, log_f
        if d == 1:
            qd, kd, vd, gd = qd[:, :, ::-1], kd[:, :, ::-1], vd[:, :, ::-1], gd[:, :, ::-1]
        o, s_fin = _hgrn_chunk_scan(qd, kd, vd, gd, s0[:, d].astype(jnp.float32))
        if d == 1:
            o = o[:, :, ::-1]
        o_sum = o if o_sum is None else o_sum + o
        finals.append(s_fin)
    return o_sum, jnp.stack(finals, axis=1)


def _centred_mean(x, w, axis):
    n = x.shape[axis]
    xf = x.astype(jnp.float32)
    cs = jnp.cumsum(xf, axis=axis)
    zshape = list(xf.shape)
    zshape[axis] = 1
    cs = jnp.concatenate([jnp.zeros(zshape, jnp.float32), cs], axis=axis)
    pos = jnp.arange(n)
    lo = jnp.clip(pos - w // 2, 0, n)
    hi = jnp.clip(pos - w // 2 + w, 0, n)
    s = jnp.take(cs, hi, axis=axis) - jnp.take(cs, lo, axis=axis)
    cshape = [1] * xf.ndim
    cshape[axis] = n
    cnt = (hi - lo).astype(jnp.float32).reshape(cshape)
    return s / cnt


def _pool_mixer(p, w_pool, pool_scale, grid):
    b, t, _ = p.shape
    groups = p.reshape(b, t, N_POOL_GROUPS, POOL_GROUP_W)
    outs = []
    for gi, w in enumerate(POOL_WINDOWS):
        xg = groups[:, :, gi]
        if grid:
            rows = t // GRID_W
            xs = xg.reshape(b, rows, GRID_W, POOL_GROUP_W)
            m = _centred_mean(_centred_mean(xs, w, 1), w, 2).reshape(b, t, POOL_GROUP_W)
        else:
            m = _centred_mean(xg, w, 1)
        outs.append(m - xg.astype(jnp.float32))
    dlt = jnp.stack(outs, axis=2).astype(p.dtype)
    y = jnp.einsum('btgc,gce->btge', dlt, w_pool).reshape(b, t, BRANCH_W)
    return y * pool_scale


def _chunk_mlp(u, v, ln_g, ln_b, w_s, b_s):
    b, t, _ = v.shape
    vf = v.astype(jnp.float32)
    mu = jnp.mean(vf, axis=-1, keepdims=True)
    var = jnp.mean(jnp.square(vf - mu), axis=-1, keepdims=True)
    vn = ((vf - mu) * lax.rsqrt(var + EPS) * ln_g.astype(jnp.float32) + ln_b.astype(jnp.float32)).astype(v.dtype)
    vc = vn.reshape(b, t // CHUNK_C, CHUNK_C, N_GROUPS_C, GROUP_W_C)
    sp = jnp.einsum('gts,bnsgc->bntgc', w_s, vc) + b_s.T[None, None, :, :, None]
    return u * sp.reshape(b, t, MIX_WIDTH)


def _layer_ab(x, shift, scale, gate, g_pre, g_post, w_in, w_out, lb, g_onorm, w_pool, pool_scale, s0, grid):
    h = _rmsnorm(x, g_pre) * (1.0 + scale) + shift
    proj = h @ w_in
    q, f_f, f_b, i_in, gate_a, p, gate_b = jnp.split(proj, 7, axis=-1)
    o, s_fin = _hgrn_bidir(q, i_in, f_f, f_b, lb, s0)
    b, _, t, _ = o.shape
    o = _rmsnorm(o, g_onorm).transpose(0, 2, 1, 3).reshape(b, t, BRANCH_W).astype(x.dtype)
    y_a = o * jax.nn.silu(gate_a)
    y_b = _pool_mixer(p, w_pool, pool_scale, grid) * jax.nn.silu(gate_b)
    y = jnp.concatenate([y_a, y_b], axis=-1) @ w_out
    return x + gate * _rmsnorm(y, g_post), s_fin


def _layer_c(x, shift, scale, gate, g_pre, g_post, w_in, w_out, ln_g, ln_b, w_s, b_s):
    h = _rmsnorm(x, g_pre) * (1.0 + scale) + shift
    u, v, g = jnp.split(h @ w_in, 3, axis=-1)
    y = (_chunk_mlp(u, v, ln_g, ln_b, w_s, b_s) * jax.nn.silu(g)) @ w_out
    return x + gate * _rmsnorm(y, g_post)


def setup_inputs(seed: int = 0) -> dict:
    key = jax.random.key(seed)
    ks = jax.random.split(key, 24)
    f32 = jnp.float32
    nrm = lambda k, s: jax.random.normal(k, s, f32)
    return {
        "x_prompt": nrm(ks[0], (BATCH, SEQ, D_MODEL)),
        "x_sample": nrm(ks[1], (DEC_BATCH, DEC_SEQ, D_MODEL)),
        "c": nrm(ks[2], (DEC_BATCH, D_MODEL)),
        "state_hgrn": 0.5 * nrm(ks[3], (DEC_BATCH, N_AB_LAYERS, 2, N_HEADS_A, HEAD_DK, HEAD_DV)),
        "c_ctx": nrm(ks[4], (D_MODEL,)),
        "w_ada": 0.5 * D_MODEL ** -0.5 * nrm(ks[5], (DEPTH, D_MODEL, 3 * D_MODEL)),
        "b_ada": 0.01 * nrm(ks[6], (DEPTH, 3 * D_MODEL)),
        "g_pre": 1.0 + 0.02 * nrm(ks[7], (DEPTH, D_MODEL)),
        "g_post": 1.0 + 0.02 * nrm(ks[8], (DEPTH, D_MODEL)),
        "w_in_ab": D_MODEL ** -0.5 * nrm(ks[9], (N_AB_LAYERS, D_MODEL, AB_IN)),
        "w_out_ab": MIX_WIDTH ** -0.5 * nrm(ks[10], (N_AB_LAYERS, MIX_WIDTH, D_MODEL)),
        "lb_logits": 0.5 * nrm(ks[11], (N_AB_LAYERS, 2, N_HEADS_A * HEAD_DK)),
        "g_onorm_a": 1.0 + 0.02 * nrm(ks[12], (N_AB_LAYERS, HEAD_DV)),
        "w_pool": POOL_GROUP_W ** -0.5 * nrm(ks[13], (N_AB_LAYERS, N_POOL_GROUPS, POOL_GROUP_W, POOL_GROUP_W)),
        "pool_scale": 1.0 + 0.02 * nrm(ks[14], (N_AB_LAYERS, BRANCH_W)),
        "w_in_c": D_MODEL ** -0.5 * nrm(ks[15], (N_C_LAYERS, D_MODEL, C_IN)),
        "w_out_c": MIX_WIDTH ** -0.5 * nrm(ks[16], (N_C_LAYERS, MIX_WIDTH, D_MODEL)),
        "ln_v_g": 1.0 + 0.02 * nrm(ks[17], (N_C_LAYERS, MIX_WIDTH)),
        "ln_v_b": 0.02 * nrm(ks[18], (N_C_LAYERS, MIX_WIDTH)),
        "w_spatial": CHUNK_C ** -0.5 * nrm(ks[19], (N_C_LAYERS, N_GROUPS_C, CHUNK_C, CHUNK_C)),
        "b_spatial": 1.0 + 0.02 * nrm(ks[20], (N_C_LAYERS, N_GROUPS_C, CHUNK_C)),
    }


def reference(x_prompt, x_sample, c, state_hgrn, c_ctx, w_ada, b_ada, g_pre, g_post, w_in_ab, w_out_ab, lb_logits, g_onorm_a, w_pool, pool_scale, w_in_c, w_out_c, ln_v_g, ln_v_b, w_spatial, b_spatial):
    lbf = jax.nn.softmax(lb_logits.astype(jnp.float32), axis=0)
    lb_all = jnp.cumsum(lbf, axis=0) - lbf[0:1]
    zero_state = jnp.zeros((x_prompt.shape[0], 2, N_HEADS_A, HEAD_DK, HEAD_DV), jnp.float32)
    y_p, y_s = x_prompt, x_sample
    ctx_states = []
    for l in range(DEPTH):
        m_ctx = _modulation(c_ctx[None, :], w_ada[l], b_ada[l])
        m_lat = _modulation(c, w_ada[l], b_ada[l])
        j = l // 2
        if l % 2 == 0:
            y_p, s_ctx = _layer_ab(y_p, *m_ctx, g_pre[l], g_post[l], w_in_ab[j], w_out_ab[j], lb_all[j], g_onorm_a[j], w_pool[j], pool_scale[j], zero_state, False)
            y_s, _ = _layer_ab(y_s, *m_lat, g_pre[l], g_post[l], w_in_ab[j], w_out_ab[j], lb_all[j], g_onorm_a[j], w_pool[j], pool_scale[j], state_hgrn[:, j], True)
            ctx_states.append(s_ctx)
        else:
            y_p = _layer_c(y_p, *m_ctx, g_pre[l], g_post[l], w_in_c[j], w_out_c[j], ln_v_g[j], ln_v_b[j], w_spatial[j], b_spatial[j])
            y_s = _layer_c(y_s, *m_lat, g_pre[l], g_post[l], w_in_c[j], w_out_c[j], ln_v_g[j], ln_v_b[j], w_spatial[j], b_spatial[j])
    new_state_hgrn = jnp.stack(ctx_states, axis=1)
    return (y_p, y_s, new_state_hgrn)
```

```python
import functools

import numpy as np
import jax
import jax.numpy as jnp
from jax import lax
from jax.experimental import pallas as pl
from jax.experimental.pallas import tpu as pltpu

F32 = jnp.float32
BF16 = jnp.bfloat16

D_MODEL = 1024
DEPTH = 4
MIX_WIDTH = 2 * D_MODEL
BRANCH_W = MIX_WIDTH // 2
HEAD_D = 128
N_HEADS = BRANCH_W // HEAD_D
GRID_W = 64
POOL_WINDOWS = (2, 4, 8, 16)
POOL_GROUP_W = BRANCH_W // len(POOL_WINDOWS)
CHUNK_C = 128
N_GROUPS_C = 8
GROUP_W_C = MIX_WIDTH // N_GROUPS_C
AB_IN = 7 * BRANCH_W
C_IN = 3 * MIX_WIDTH
EPS = 1e-6
N_COND_ROWS = 8

HGRN_CHUNK = 128
HGRN_LEVELS = (64, 32, 16, 8, 4, 2)
MIB = 1024 * 1024


def _cparams(n_axes, vmem_mib):
    return pltpu.CompilerParams(
        dimension_semantics=("arbitrary",) * n_axes, vmem_limit_bytes=int(vmem_mib * MIB))


def _silu(x):
    return x * jax.nn.sigmoid(x)


def _rms(x, g):
    return x * lax.rsqrt(jnp.mean(x * x, axis=-1, keepdims=True) + EPS) * g


def _mod_kernel(cond_ref, w_ref, b_ref, o_ref):
    a = _silu(cond_ref[...])
    o_ref[0] = jnp.dot(a, w_ref[0], preferred_element_type=F32,
                       precision=lax.Precision.HIGHEST) + b_ref[0]


def _modulations(cond, w_ada, b_ada):
    tn = 1024
    n3 = 3 * D_MODEL
    return pl.pallas_call(
        _mod_kernel,
        grid=(DEPTH, n3 // tn),
        in_specs=[
            pl.BlockSpec((N_COND_ROWS, D_MODEL), lambda l, j: (0, 0)),
            pl.BlockSpec((1, D_MODEL, tn), lambda l, j: (l, 0, j)),
            pl.BlockSpec((1, 1, tn), lambda l, j: (l, 0, j)),
        ],
        out_specs=pl.BlockSpec((1, N_COND_ROWS, tn), lambda l, j: (l, 0, j)),
        out_shape=jax.ShapeDtypeStruct((DEPTH, N_COND_ROWS, n3), F32),
        compiler_params=_cparams(2, 24),
        name="adaln_modulation",
    )(cond, w_ada, b_ada.reshape(DEPTH, 1, n3))


def _lb_kernel(x_ref, o_ref):
    x = x_ref[...]
    e = jnp.exp(x - jnp.max(x, axis=0, keepdims=True))
    p = e / jnp.sum(e, axis=0, keepdims=True)
    run = p[0]
    o_ref[0] = run - p[0]
    for l in range(1, x.shape[0]):
        run = run + p[l]
        o_ref[l] = run - p[0]


def _lower_bounds(lb_logits):
    return pl.pallas_call(
        _lb_kernel, out_shape=jax.ShapeDtypeStruct(lb_logits.shape, F32), name="hgrn_lower_bounds",
    )(lb_logits)


def _in_kernel(x_ref, mod_ref, g_ref, w_ref, o_ref, h_ref):
    @pl.when(pl.program_id(1) == 0)
    def _():
        m = mod_ref[0]
        h = _rms(x_ref[...], g_ref[...]) * (1.0 + m[1:2]) + m[0:1]
        h_ref[...] = h.astype(BF16)

    o_ref[...] = jnp.dot(h_ref[...], w_ref[...], preferred_element_type=F32).astype(o_ref.dtype)


def _in_proj(x, mod, g, w, row_of_tile, tm, tn):
    n_tok, n_out = x.shape[0], w.shape[1]
    return pl.pallas_call(
        _in_kernel,
        grid=(n_tok // tm, n_out // tn),
        in_specs=[
            pl.BlockSpec((tm, D_MODEL), lambda i, j: (i, 0)),
            pl.BlockSpec((1, 3, D_MODEL), lambda i, j: (row_of_tile(i), 0, 0)),
            pl.BlockSpec((1, D_MODEL), lambda i, j: (0, 0)),
            pl.BlockSpec((D_MODEL, tn), lambda i, j: (0, j)),
        ],
        out_specs=pl.BlockSpec((tm, tn), lambda i, j: (i, j)),
        out_shape=jax.ShapeDtypeStruct((n_tok, n_out), F32),
        scratch_shapes=[pltpu.VMEM((tm, D_MODEL), BF16)],
        compiler_params=_cparams(2, 32),
        name="in_projection",
    )(x, mod, g.reshape(1, D_MODEL), w)


def _out_kernel(*refs, n_parts):
    y_refs, w_refs = refs[:n_parts], refs[n_parts:2 * n_parts]
    x_ref, mod_ref, g_ref, o_ref = refs[2 * n_parts:]
    acc = jnp.dot(y_refs[0][...], w_refs[0][...], preferred_element_type=F32)
    for y_ref, w_ref in zip(y_refs[1:], w_refs[1:]):
        acc = acc + jnp.dot(y_ref[...], w_ref[...], preferred_element_type=F32)
    o_ref[...] = x_ref[...] + mod_ref[0][2:3] * _rms(acc, g_ref[...])


def _out_proj(ys, ws, x, mod, g, row_of_tile, tm):
    n_tok = x.shape[0]
    n_parts = len(ys)
    in_specs = [pl.BlockSpec((tm, y.shape[1]), lambda i: (i, 0)) for y in ys]
    in_specs += [pl.BlockSpec(w.shape, lambda i: (0, 0)) for w in ws]
    in_specs += [
        pl.BlockSpec((tm, D_MODEL), lambda i: (i, 0)),
        pl.BlockSpec((1, 3, D_MODEL), lambda i: (row_of_tile(i), 0, 0)),
        pl.BlockSpec((1, D_MODEL), lambda i: (0, 0)),
    ]
    return pl.pallas_call(
        functools.partial(_out_kernel, n_parts=n_parts),
        grid=(n_tok // tm,),
        in_specs=in_specs,
        out_specs=pl.BlockSpec((tm, D_MODEL), lambda i: (i, 0)),
        out_shape=jax.ShapeDtypeStruct((n_tok, D_MODEL), F32),
        compiler_params=_cparams(1, 40),
        name="out_projection",
    )(*ys, *ws, x, mod, g.reshape(1, D_MODEL))


def _mixc_kernel(u_ref, v_ref, g_ref, lng_ref, lnb_ref, ws_ref, bs_ref, o_ref, *, tm):
    v = v_ref[...]
    mu = jnp.mean(v, axis=-1, keepdims=True)
    vc = v - mu
    var = jnp.mean(vc * vc, axis=-1, keepdims=True)
    vn = (vc * lax.rsqrt(var + EPS) * lng_ref[...] + lnb_ref[...]).astype(BF16)
    for n in range(tm // CHUNK_C):
        rows = slice(n * CHUNK_C, (n + 1) * CHUNK_C)
        for gi in range(N_GROUPS_C):
            cols = slice(gi * GROUP_W_C, (gi + 1) * GROUP_W_C)
            sp = jnp.dot(ws_ref[gi], vn[rows, cols], preferred_element_type=F32) + bs_ref[:, gi:gi + 1]
            o_ref[rows, cols] = (u_ref[rows, cols] * sp * _silu(g_ref[rows, cols])).astype(BF16)


def _mix_c(proj, ln_g, ln_b, w_s, b_s, tm):
    n_tok = proj.shape[0]
    return pl.pallas_call(
        functools.partial(_mixc_kernel, tm=tm),
        grid=(n_tok // tm,),
        in_specs=[
            pl.BlockSpec((tm, MIX_WIDTH), lambda i: (i, 0)),
            pl.BlockSpec((tm, MIX_WIDTH), lambda i: (i, 1)),
            pl.BlockSpec((tm, MIX_WIDTH), lambda i: (i, 2)),
            pl.BlockSpec((1, MIX_WIDTH), lambda i: (0, 0)),
            pl.BlockSpec((1, MIX_WIDTH), lambda i: (0, 0)),
            pl.BlockSpec((N_GROUPS_C, CHUNK_C, CHUNK_C), lambda i: (0, 0, 0)),
            pl.BlockSpec((CHUNK_C, N_GROUPS_C), lambda i: (0, 0)),
        ],
        out_specs=pl.BlockSpec((tm, MIX_WIDTH), lambda i: (i, 0)),
        out_shape=jax.ShapeDtypeStruct((n_tok, MIX_WIDTH), BF16),
        compiler_params=_cparams(1, 40),
        name="gmlp_mixer",
    )(proj, proj, proj, ln_g.reshape(1, MIX_WIDTH), ln_b.reshape(1, MIX_WIDTH),
      w_s.astype(BF16), b_s.T)


def _centred_mean(x, w, stride, n):
    rows = x.shape[0]
    t = lax.broadcasted_iota(jnp.int32, x.shape, 0)
    idx = (t >> (stride.bit_length() - 1)) & (n - 1)
    half = w // 2
    trail, lead = x, x
    size = 1
    while size < half:
        sh = size * stride
        trail = trail + jnp.where(idx >= size, pltpu.roll(trail, sh, 0), 0.0)
        lead = lead + jnp.where(idx + size < n, pltpu.roll(lead, rows - sh, 0), 0.0)
        size *= 2
    s = jnp.where(idx >= 1, pltpu.roll(trail, stride, 0), 0.0) + lead
    cnt = jnp.minimum(idx - half + w, n) - jnp.maximum(idx - half, 0)
    return s / cnt.astype(F32)


def _pool_kernel(p_ref, gate_ref, wp_ref, ps_ref, y_ref, *, seq_len, grid_mode):
    grp = pl.program_id(1)
    for k, w in enumerate(POOL_WINDOWS):
        @pl.when(grp == k)
        def _(w=w):
            x = p_ref[...]
            if grid_mode:
                m = _centred_mean(x, w, GRID_W, seq_len // GRID_W)
                m = _centred_mean(m, w, 1, GRID_W)
            else:
                m = _centred_mean(x, w, 1, seq_len)
            dlt = (m - x).astype(BF16)
            y = jnp.dot(dlt, wp_ref[0], preferred_element_type=F32) * ps_ref[...]
            y_ref[...] = (y * _silu(gate_ref[...])).astype(BF16)


def _pool_mixer(proj, w_pool, pool_scale, seq_len, grid_mode):
    n_tok = proj.shape[0]
    n_grp = len(POOL_WINDOWS)
    p_blk0 = 5 * BRANCH_W // POOL_GROUP_W
    g_blk0 = 6 * BRANCH_W // POOL_GROUP_W
    return pl.pallas_call(
        functools.partial(_pool_kernel, seq_len=seq_len, grid_mode=grid_mode),
        grid=(n_tok // seq_len, n_grp),
        in_specs=[
            pl.BlockSpec((seq_len, POOL_GROUP_W), lambda b, g: (b, p_blk0 + g)),
            pl.BlockSpec((seq_len, POOL_GROUP_W), lambda b, g: (b, g_blk0 + g)),
            pl.BlockSpec((1, POOL_GROUP_W, POOL_GROUP_W), lambda b, g: (g, 0, 0)),
            pl.BlockSpec((1, POOL_GROUP_W), lambda b, g: (0, g)),
        ],
        out_specs=pl.BlockSpec((seq_len, POOL_GROUP_W), lambda b, g: (b, g)),
        out_shape=jax.ShapeDtypeStruct((n_tok, BRANCH_W), BF16),
        compiler_params=_cparams(2, 56),
        name="pool_mixer",
    )(proj, proj, w_pool.astype(BF16), pool_scale.reshape(1, BRANCH_W))


def _hgrn_partial_sum_matrix():
    c = HGRN_CHUNK
    blocks = []
    for m in HGRN_LEVELS:
        mq = np.zeros((c, 2 * c), np.float32)
        mk = np.zeros((c, 2 * c), np.float32)
        for i in range(c):
            start = (i // (2 * m)) * (2 * m)
            r1, r2 = start + m - 1, start + m
            if i >= r2:
                mq[i, r2:i + 1] = 1
                mk[i, c + r2:c + i] = 1
            else:
                mq[i, c + i:c + r1 + 1] = 1
                mk[i, i + 1:r1 + 1] = 1
        blocks += [mq, mk]
    incl_prefix = np.zeros((c, 2 * c), np.float32)
    incl_suffix = np.zeros((c, 2 * c), np.float32)
    excl_suffix = np.zeros((c, 2 * c), np.float32)
    excl_prefix = np.zeros((c, 2 * c), np.float32)
    for i in range(c):
        incl_prefix[i, :i + 1] = 1
        incl_suffix[i, c + i:] = 1
        excl_suffix[i, i + 1:c] = 1
        excl_prefix[i, c:c + i] = 1
    blocks += [incl_prefix, incl_suffix, excl_suffix, excl_prefix]
    return np.concatenate(blocks, axis=0)


def _hgrn_level_index():
    i = np.arange(HGRN_CHUNK)
    x = i[:, None] ^ i[None, :]
    lv = np.zeros_like(x)
    nz = x > 0
    lv[nz] = np.floor(np.log2(x[nz])).astype(x.dtype) + 1
    return lv.astype(np.int32)


_HGRN_PSUM = _hgrn_partial_sum_matrix()
_HGRN_LEVEL = _hgrn_level_index()

_NT = (((1,), (1,)), ((), ()))
_TN = (((0,), (0,)), ((), ()))


def _hgrn_kernel(*refs, n_chunks, has_s0, want_final):
    q_ref, ff_ref, fb_ref, v_ref, ga_ref, lb_ref, gon_ref, psum_ref, lvl_ref = refs[:9]
    pos = 9
    s0_ref = sfin_ref = None
    if has_s0:
        s0_ref = refs[pos]
        pos += 1
    y_ref = refs[pos]
    pos += 1
    if want_final:
        sfin_ref = refs[pos]
        pos += 1
    o_scr, qfb_scr, dst_scr, dec_scr, st_scr = refs[pos:]

    c = HGRN_CHUNK
    lbv = lb_ref[...]
    lb_f, lb_b = lbv[0:1], lbv[1:2]
    level = lvl_ref[...]
    row = lax.broadcasted_iota(jnp.int32, (c, HEAD_D), 0)

    def chunk_rows(ci):
        return pl.ds(pl.multiple_of(ci * c, c), c)

    def local_pass(ci, carry):
        rows = chunk_rows(ci)
        f_f = lb_f + (1.0 - lb_f) * jax.nn.sigmoid(ff_ref[rows, :])
        f_b = lb_b + (1.0 - lb_b) * jax.nn.sigmoid(fb_ref[rows, :])
        k_f, k_b = 1.0 - f_f, 1.0 - f_b
        logs = jnp.concatenate([jnp.log(f_f), jnp.log(f_b)], axis=0)
        hi = logs.astype(BF16)
        mid = (logs - hi.astype(F32)).astype(BF16)
        dd = jnp.dot(psum_ref[...], jnp.concatenate([hi, mid], axis=1), preferred_element_type=F32)
        dsum = dd[:, :HEAD_D] + dd[:, HEAD_D:]

        q = _silu(q_ref[rows, :])
        v = v_ref[rows, :].astype(BF16)
        qb = q.astype(BF16)

        s = lax.dot_general(qb, (k_f + k_b).astype(BF16), _NT, preferred_element_type=F32)
        s = jnp.where(level == 0, s, 0.0)
        odd = (row & 1) == 1
        q1 = (q * jnp.where(odd, f_f, f_b)).astype(BF16)
        k1 = jnp.where(odd, k_b, k_f).astype(BF16)
        s = jnp.where(level == 1, lax.dot_general(q1, k1, _NT, preferred_element_type=F32), s)
        for li, m in enumerate(HGRN_LEVELS):
            d_q = dsum[(2 * li) * c:(2 * li + 1) * c]
            d_k = dsum[(2 * li + 1) * c:(2 * li + 2) * c]
            second = ((row >> (m.bit_length() - 1)) & 1) == 1
            qt = (q * jnp.exp(d_q)).astype(BF16)
            kt = (jnp.where(second, k_b, k_f) * jnp.exp(d_k)).astype(BF16)
            s_l = lax.dot_general(qt, kt, _NT, preferred_element_type=F32)
            s = jnp.where(level == m.bit_length(), s_l, s)

        base = 2 * len(HGRN_LEVELS) * c
        b_f = dsum[base:base + c]
        b_b = dsum[base + c:base + 2 * c]
        e_f = dsum[base + 2 * c:base + 3 * c]
        e_b = dsum[base + 3 * c:base + 4 * c]
        q_fb = jnp.concatenate([q * jnp.exp(b_f), q * jnp.exp(b_b)], axis=1).astype(BF16)
        k_fb = jnp.concatenate([k_f * jnp.exp(e_f), k_b * jnp.exp(e_b)], axis=1).astype(BF16)
        dec = jnp.concatenate([jnp.exp(b_f[c - 1:c]), jnp.exp(b_b[0:1])], axis=1)

        o_scr[rows, :] = jnp.dot(s.astype(BF16), v, preferred_element_type=F32)
        qfb_scr[rows, :] = q_fb
        dst_scr[ci] = lax.dot_general(v, k_fb, _TN, preferred_element_type=F32)
        dec_scr[ci] = jnp.broadcast_to(dec, (8, 2 * HEAD_D))
        return carry

    lax.fori_loop(0, n_chunks, local_pass, 0)

    if has_s0:
        st_f0 = s0_ref[0, 0, 0, 0].T
        st_b0 = s0_ref[0, 0, 1, 0].T
    else:
        st_f0 = jnp.zeros((HEAD_D, HEAD_D), F32)
        st_b0 = jnp.zeros((HEAD_D, HEAD_D), F32)

    def fwd_scan(ci, st):
        st_scr[ci, :, 0:HEAD_D] = st.astype(BF16)
        return st * dec_scr[ci, 0:1, 0:HEAD_D] + dst_scr[ci, :, 0:HEAD_D]

    def bwd_scan(k, st):
        ci = n_chunks - 1 - k
        st_scr[ci, :, HEAD_D:2 * HEAD_D] = st.astype(BF16)
        return st * dec_scr[ci, 0:1, HEAD_D:2 * HEAD_D] + dst_scr[ci, :, HEAD_D:2 * HEAD_D]

    st_f = lax.fori_loop(0, n_chunks, fwd_scan, st_f0)
    st_b = lax.fori_loop(0, n_chunks, bwd_scan, st_b0)
    if want_final:
        sfin_ref[0, 0, 0] = st_f.T
        sfin_ref[0, 1, 0] = st_b.T

    def output_pass(ci, carry):
        rows = chunk_rows(ci)
        o = o_scr[rows, :] + lax.dot_general(qfb_scr[rows, :], st_scr[ci], _NT,
                                             preferred_element_type=F32)
        y_ref[rows, :] = (_rms(o, gon_ref[...]) * _silu(ga_ref[rows, :])).astype(BF16)
        return carry

    lax.fori_loop(0, n_chunks, output_pass, 0)


def _hgrn_mixer(proj, lb, g_onorm, state, layer_j, seq_len, want_final):
    n_tok = proj.shape[0]
    n_seq = n_tok // seq_len
    n_chunks = seq_len // HGRN_CHUNK
    has_s0 = state is not None
    blk = lambda part: pl.BlockSpec(
        (seq_len, HEAD_D), lambda b, h: (b, part * N_HEADS + h))
    in_specs = [blk(0), blk(1), blk(2), blk(3), blk(4),
                pl.BlockSpec((2, HEAD_D), lambda b, h: (0, h)),
                pl.BlockSpec((1, HEAD_D), lambda b, h: (0, 0)),
                pl.BlockSpec(_HGRN_PSUM.shape, lambda b, h: (0, 0)),
                pl.BlockSpec(_HGRN_LEVEL.shape, lambda b, h: (0, 0))]
    args = [proj, proj, proj, proj, proj, lb, g_onorm.reshape(1, HEAD_D),
            jnp.asarray(_HGRN_PSUM, BF16), jnp.asarray(_HGRN_LEVEL)]
    if has_s0:
        in_specs.append(pl.BlockSpec((1, 1, 2, 1, HEAD_D, HEAD_D),
                                     lambda b, h: (b, layer_j, 0, h, 0, 0)))
        args.append(state)
    out_specs = [pl.BlockSpec((seq_len, HEAD_D), lambda b, h: (b, h))]
    out_shape = [jax.ShapeDtypeStruct((n_tok, BRANCH_W), BF16)]
    if want_final:
        out_specs.append(pl.BlockSpec((1, 2, 1, HEAD_D, HEAD_D), lambda b, h: (b, 0, h, 0, 0)))
        out_shape.append(jax.ShapeDtypeStruct((n_seq, 2, N_HEADS, HEAD_D, HEAD_D), F32))
    outs = pl.pallas_call(
        functools.partial(_hgrn_kernel, n_chunks=n_chunks, has_s0=has_s0, want_final=want_final),
        grid=(n_seq, N_HEADS),
        in_specs=in_specs,
        out_specs=out_specs,
        out_shape=out_shape,
        scratch_shapes=[
            pltpu.VMEM((seq_len, HEAD_D), F32),
            pltpu.VMEM((seq_len, 2 * HEAD_D), BF16),
            pltpu.VMEM((n_chunks, HEAD_D, 2 * HEAD_D), F32),
            pltpu.VMEM((n_chunks, 8, 2 * HEAD_D), F32),
            pltpu.VMEM((n_chunks, HEAD_D, 2 * HEAD_D), BF16),
        ],
        compiler_params=_cparams(2, 48),
        name="hgrn_mixer",
    )(*args)
    return outs if want_final else (outs[0], None)


def kernel(x_prompt, x_sample, c, state_hgrn, c_ctx, w_ada, b_ada, g_pre, g_post, w_in_ab, w_out_ab,
           lb_logits, g_onorm_a, w_pool, pool_scale, w_in_c, w_out_c, ln_v_g, ln_v_b, w_spatial,
           b_spatial):
    n_batch, seq_p, _ = x_prompt.shape
    dec_batch, seq_s, _ = x_sample.shape
    tm = 512

    cond = jnp.concatenate(
        [c_ctx[None, :], c, jnp.zeros((N_COND_ROWS - 1 - dec_batch, D_MODEL), F32)], axis=0)
    mods = _modulations(cond, w_ada, b_ada).reshape(DEPTH, N_COND_ROWS, 3, D_MODEL)
    lb_all = _lower_bounds(lb_logits)

    tiles_per_sample = seq_s // tm
    flows = [
        dict(x=x_prompt.reshape(n_batch * seq_p, D_MODEL), seq=seq_p, grid=False, state=None,
             row=lambda i: 0),
        dict(x=x_sample.reshape(dec_batch * seq_s, D_MODEL), seq=seq_s, grid=True, state=state_hgrn,
             row=lambda i: 1 + i // tiles_per_sample),
    ]
    ctx_states = []
    for l in range(DEPTH):
        j = l // 2
        for fi, fl in enumerate(flows):
            if l % 2 == 0:
                w_in, w_out = w_in_ab[j].astype(BF16), w_out_ab[j].astype(BF16)
                proj = _in_proj(fl["x"], mods[l], g_pre[l], w_in, fl["row"], tm, 1024)
                y_a, s_fin = _hgrn_mixer(proj, lb_all[j], g_onorm_a[j], fl["state"], j, fl["seq"],
                                         want_final=(fi == 0))
                y_b = _pool_mixer(proj, w_pool[j], pool_scale[j], fl["seq"], fl["grid"])
                fl["x"] = _out_proj([y_a, y_b], [w_out[:BRANCH_W], w_out[BRANCH_W:]], fl["x"],
                                    mods[l], g_post[l], fl["row"], tm)
                if fi == 0:
                    ctx_states.append(s_fin)
            else:
                w_in, w_out = w_in_c[j].astype(BF16), w_out_c[j].astype(BF16)
                proj = _in_proj(fl["x"], mods[l], g_pre[l], w_in, fl["row"], tm, 1024)
                y = _mix_c(proj, ln_v_g[j], ln_v_b[j], w_spatial[j], b_spatial[j], 256)
                fl["x"] = _out_proj([y], [w_out], fl["x"], mods[l], g_post[l], fl["row"], tm)
    y_p = flows[0]["x"].reshape(n_batch, seq_p, D_MODEL)
    y_s = flows[1]["x"].reshape(dec_batch, seq_s, D_MODEL)
    return (y_p, y_s, jnp.stack(ctx_states, axis=1))
```

```python
import functools

import numpy as np
import jax
import jax.numpy as jnp
from jax import lax
from jax.experimental import pallas as pl
from jax.experimental.pallas import tpu as pltpu

F32 = jnp.float32
BF16 = jnp.bfloat16

D_MODEL = 1024
DEPTH = 4
MIX_WIDTH = 2 * D_MODEL
BRANCH_W = MIX_WIDTH // 2
HEAD_D = 128
N_HEADS = BRANCH_W // HEAD_D
GRID_W = 64
POOL_WINDOWS = (2, 4, 8, 16)
POOL_GROUP_W = BRANCH_W // len(POOL_WINDOWS)
CHUNK_C = 128
N_GROUPS_C = 8
GROUP_W_C = MIX_WIDTH // N_GROUPS_C
AB_IN = 7 * BRANCH_W
C_IN = 3 * MIX_WIDTH
EPS = 1e-6
N_COND_ROWS = 8

HGRN_CHUNK = 128
HGRN_LEVELS = (64, 32, 16, 8, 4)
MIB = 1024 * 1024


def _cparams(n_axes, vmem_mib):
    return pltpu.CompilerParams(
        dimension_semantics=("arbitrary",) * n_axes, vmem_limit_bytes=int(vmem_mib * MIB))


def _silu(x):
    return x * jax.nn.sigmoid(x)


def _rms(x, g):
    return x * lax.rsqrt(jnp.mean(x * x, axis=-1, keepdims=True) + EPS) * g


def _mod_kernel(cond_ref, w_ref, b_ref, o_ref):
    a = _silu(cond_ref[...])
    o_ref[0] = jnp.dot(a, w_ref[0], preferred_element_type=F32,
                       precision=lax.Precision.HIGHEST) + b_ref[0]


def _modulations(cond, w_ada, b_ada):
    tn = 1024
    n3 = 3 * D_MODEL
    return pl.pallas_call(
        _mod_kernel,
        grid=(DEPTH, n3 // tn),
        in_specs=[
            pl.BlockSpec((N_COND_ROWS, D_MODEL), lambda l, j: (0, 0)),
            pl.BlockSpec((1, D_MODEL, tn), lambda l, j: (l, 0, j)),
            pl.BlockSpec((1, 1, tn), lambda l, j: (l, 0, j)),
        ],
        out_specs=pl.BlockSpec((1, N_COND_ROWS, tn), lambda l, j: (l, 0, j)),
        out_shape=jax.ShapeDtypeStruct((DEPTH, N_COND_ROWS, n3), F32),
        compiler_params=_cparams(2, 24),
        name="adaln_modulation",
    )(cond, w_ada, b_ada.reshape(DEPTH, 1, n3))


def _lb_kernel(x_ref, o_ref):
    x = x_ref[...]
    e = jnp.exp(x - jnp.max(x, axis=0, keepdims=True))
    p = e / jnp.sum(e, axis=0, keepdims=True)
    run = p[0]
    o_ref[0] = run - p[0]
    for l in range(1, x.shape[0]):
        run = run + p[l]
        o_ref[l] = run - p[0]


def _lower_bounds(lb_logits):
    return pl.pallas_call(
        _lb_kernel, out_shape=jax.ShapeDtypeStruct(lb_logits.shape, F32), name="hgrn_lower_bounds",
    )(lb_logits)


def _in_kernel(x_ref, mod_ref, g_ref, w_ref, o_ref, h_ref):
    @pl.when(pl.program_id(1) == 0)
    def _():
        m = mod_ref[0]
        h = _rms(x_ref[...], g_ref[...]) * (1.0 + m[1:2]) + m[0:1]
        h_ref[...] = h.astype(BF16)

    o_ref[...] = jnp.dot(h_ref[...], w_ref[...], preferred_element_type=F32).astype(o_ref.dtype)


def _in_proj(x, mod, g, w, row_of_tile, tm, tn):
    n_tok, n_out = x.shape[0], w.shape[1]
    return pl.pallas_call(
        _in_kernel,
        grid=(n_tok // tm, n_out // tn),
        in_specs=[
            pl.BlockSpec((tm, D_MODEL), lambda i, j: (i, 0)),
            pl.BlockSpec((1, 3, D_MODEL), lambda i, j: (row_of_tile(i), 0, 0)),
            pl.BlockSpec((1, D_MODEL), lambda i, j: (0, 0)),
            pl.BlockSpec((D_MODEL, tn), lambda i, j: (0, j)),
        ],
        out_specs=pl.BlockSpec((tm, tn), lambda i, j: (i, j)),
        out_shape=jax.ShapeDtypeStruct((n_tok, n_out), F32),
        scratch_shapes=[pltpu.VMEM((tm, D_MODEL), BF16)],
        compiler_params=_cparams(2, 32),
        name="in_projection",
    )(x, mod, g.reshape(1, D_MODEL), w)


def _out_kernel(*refs, n_parts):
    y_refs, w_refs = refs[:n_parts], refs[n_parts:2 * n_parts]
    x_ref, mod_ref, g_ref, o_ref = refs[2 * n_parts:]
    acc = jnp.dot(y_refs[0][...], w_refs[0][...], preferred_element_type=F32)
    for y_ref, w_ref in zip(y_refs[1:], w_refs[1:]):
        acc = acc + jnp.dot(y_ref[...], w_ref[...], preferred_element_type=F32)
    o_ref[...] = x_ref[...] + mod_ref[0][2:3] * _rms(acc, g_ref[...])


def _out_proj(ys, ws, x, mod, g, row_of_tile, tm):
    n_tok = x.shape[0]
    n_parts = len(ys)
    in_specs = [pl.BlockSpec((tm, y.shape[1]), lambda i: (i, 0)) for y in ys]
    in_specs += [pl.BlockSpec(w.shape, lambda i: (0, 0)) for w in ws]
    in_specs += [
        pl.BlockSpec((tm, D_MODEL), lambda i: (i, 0)),
        pl.BlockSpec((1, 3, D_MODEL), lambda i: (row_of_tile(i), 0, 0)),
        pl.BlockSpec((1, D_MODEL), lambda i: (0, 0)),
    ]
    return pl.pallas_call(
        functools.partial(_out_kernel, n_parts=n_parts),
        grid=(n_tok // tm,),
        in_specs=in_specs,
        out_specs=pl.BlockSpec((tm, D_MODEL), lambda i: (i, 0)),
        out_shape=jax.ShapeDtypeStruct((n_tok, D_MODEL), F32),
        compiler_params=_cparams(1, 40),
        name="out_projection",
    )(*ys, *ws, x, mod, g.reshape(1, D_MODEL))


def _mixc_kernel(u_ref, v_ref, g_ref, lng_ref, lnb_ref, ws_ref, bs_ref, o_ref, *, tm):
    v = v_ref[...]
    mu = jnp.mean(v, axis=-1, keepdims=True)
    vc = v - mu
    var = jnp.mean(vc * vc, axis=-1, keepdims=True)
    vn = (vc * lax.rsqrt(var + EPS) * lng_ref[...] + lnb_ref[...]).astype(BF16)
    for n in range(tm // CHUNK_C):
        rows = slice(n * CHUNK_C, (n + 1) * CHUNK_C)
        for gi in range(N_GROUPS_C):
            cols = slice(gi * GROUP_W_C, (gi + 1) * GROUP_W_C)
            sp = jnp.dot(ws_ref[gi], vn[rows, cols], preferred_element_type=F32) + bs_ref[:, gi:gi + 1]
            o_ref[rows, cols] = (u_ref[rows, cols] * sp * _silu(g_ref[rows, cols])).astype(BF16)


def _mix_c(proj, ln_g, ln_b, w_s, b_s, tm):
    n_tok = proj.shape[0]
    return pl.pallas_call(
        functools.partial(_mixc_kernel, tm=tm),
        grid=(n_tok // tm,),
        in_specs=[
            pl.BlockSpec((tm, MIX_WIDTH), lambda i: (i, 0)),
            pl.BlockSpec((tm, MIX_WIDTH), lambda i: (i, 1)),
            pl.BlockSpec((tm, MIX_WIDTH), lambda i: (i, 2)),
            pl.BlockSpec((1, MIX_WIDTH), lambda i: (0, 0)),
            pl.BlockSpec((1, MIX_WIDTH), lambda i: (0, 0)),
            pl.BlockSpec((N_GROUPS_C, CHUNK_C, CHUNK_C), lambda i: (0, 0, 0)),
            pl.BlockSpec((CHUNK_C, N_GROUPS_C), lambda i: (0, 0)),
        ],
        out_specs=pl.BlockSpec((tm, MIX_WIDTH), lambda i: (i, 0)),
        out_shape=jax.ShapeDtypeStruct((n_tok, MIX_WIDTH), BF16),
        compiler_params=_cparams(1, 40),
        name="gmlp_mixer",
    )(proj, proj, proj, ln_g.reshape(1, MIX_WIDTH), ln_b.reshape(1, MIX_WIDTH),
      w_s.astype(BF16), b_s.T)


def _centred_mean(x, w, stride, n):
    rows = x.shape[0]
    t = lax.broadcasted_iota(jnp.int32, x.shape, 0)
    idx = (t >> (stride.bit_length() - 1)) & (n - 1)
    half = w // 2
    trail, lead = x, x
    size = 1
    while size < half:
        sh = size * stride
        trail = trail + jnp.where(idx >= size, pltpu.roll(trail, sh, 0), 0.0)
        lead = lead + jnp.where(idx + size < n, pltpu.roll(lead, rows - sh, 0), 0.0)
        size *= 2
    s = jnp.where(idx >= 1, pltpu.roll(trail, stride, 0), 0.0) + lead
    cnt = jnp.minimum(idx - half + w, n) - jnp.maximum(idx - half, 0)
    return s / cnt.astype(F32)


def _pool_kernel(p_ref, gate_ref, wp_ref, ps_ref, y_ref, *, seq_len, grid_mode):
    grp = pl.program_id(1)
    for k, w in enumerate(POOL_WINDOWS):
        @pl.when(grp == k)
        def _(w=w):
            x = p_ref[...]
            if grid_mode:
                m = _centred_mean(x, w, GRID_W, seq_len // GRID_W)
                m = _centred_mean(m, w, 1, GRID_W)
            else:
                m = _centred_mean(x, w, 1, seq_len)
            dlt = (m - x).astype(BF16)
            y = jnp.dot(dlt, wp_ref[0], preferred_element_type=F32) * ps_ref[...]
            y_ref[...] = (y * _silu(gate_ref[...])).astype(BF16)


def _pool_mixer(proj, w_pool, pool_scale, seq_len, grid_mode):
    n_tok = proj.shape[0]
    n_grp = len(POOL_WINDOWS)
    p_blk0 = 5 * BRANCH_W // POOL_GROUP_W
    g_blk0 = 6 * BRANCH_W // POOL_GROUP_W
    return pl.pallas_call(
        functools.partial(_pool_kernel, seq_len=seq_len, grid_mode=grid_mode),
        grid=(n_tok // seq_len, n_grp),
        in_specs=[
            pl.BlockSpec((seq_len, POOL_GROUP_W), lambda b, g: (b, p_blk0 + g)),
            pl.BlockSpec((seq_len, POOL_GROUP_W), lambda b, g: (b, g_blk0 + g)),
            pl.BlockSpec((1, POOL_GROUP_W, POOL_GROUP_W), lambda b, g: (g, 0, 0)),
            pl.BlockSpec((1, POOL_GROUP_W), lambda b, g: (0, g)),
        ],
        out_specs=pl.BlockSpec((seq_len, POOL_GROUP_W), lambda b, g: (b, g)),
        out_shape=jax.ShapeDtypeStruct((n_tok, BRANCH_W), BF16),
        compiler_params=_cparams(2, 56),
        name="pool_mixer",
    )(proj, proj, w_pool.astype(BF16), pool_scale.reshape(1, BRANCH_W))


def _hgrn_triangles():
    ones = np.ones((HGRN_CHUNK, HGRN_CHUNK), np.float32)
    return np.concatenate([np.tril(ones), np.triu(ones)], axis=0)


def _hgrn_level_index():
    i = np.arange(HGRN_CHUNK)
    x = i[:, None] ^ i[None, :]
    lv = np.zeros_like(x)
    nz = x > 0
    lv[nz] = np.floor(np.log2(x[nz])).astype(x.dtype) + 1
    return lv.astype(np.int32)


_HGRN_TRI = _hgrn_triangles()
_HGRN_LEVEL = _hgrn_level_index()

_NT = (((1,), (1,)), ((), ()))
_TN = (((0,), (0,)), ((), ()))


def _hgrn_kernel(*refs, n_chunks, has_s0, want_final):
    q_ref, ff_ref, fb_ref, v_ref, ga_ref, lb_ref, gon_ref, tri_ref, lvl_ref = refs[:9]
    pos = 9
    s0_ref = sfin_ref = None
    if has_s0:
        s0_ref = refs[pos]
        pos += 1
    y_ref = refs[pos]
    pos += 1
    if want_final:
        sfin_ref = refs[pos]
        pos += 1
    o_scr, qfb_scr, dst_scr, dec_scr, st_scr = refs[pos:]

    c = HGRN_CHUNK
    lbv = lb_ref[...]
    lb_f, lb_b = lbv[0:1], lbv[1:2]
    level = lvl_ref[...]
    row = lax.broadcasted_iota(jnp.int32, (c, HEAD_D), 0)

    def chunk_rows(ci):
        return pl.ds(pl.multiple_of(ci * c, c), c)

    def cumulative(tri, x):
        hi = x.astype(BF16)
        mid = (x - hi.astype(F32)).astype(BF16)
        dd = jnp.dot(tri, jnp.concatenate([hi, mid], axis=1), preferred_element_type=F32)
        return dd[:, :HEAD_D] + dd[:, HEAD_D:]

    def boundary_rows(b, first, block):
        return jnp.concatenate(
            [jnp.broadcast_to(b[r:r + 1, :], (block, HEAD_D)) for r in range(first, c, block)], axis=0)

    def local_pass(ci, carry):
        rows = chunk_rows(ci)
        f_f = lb_f + (1.0 - lb_f) * jax.nn.sigmoid(ff_ref[rows, :])
        f_b = lb_b + (1.0 - lb_b) * jax.nn.sigmoid(fb_ref[rows, :])
        k_f, k_b = 1.0 - f_f, 1.0 - f_b
        l_f, l_b = jnp.log2(f_f), jnp.log2(f_b)
        b_f = cumulative(tri_ref[0:c, :], l_f)
        b_b = cumulative(tri_ref[c:2 * c, :], l_b)

        q = _silu(q_ref[rows, :])
        v = v_ref[rows, :].astype(BF16)

        def level_scores(d_q, d_k, k_sel):
            qt = (q * jnp.exp2(d_q)).astype(BF16)
            kt = (k_sel * jnp.exp2(d_k)).astype(BF16)
            return lax.dot_general(qt, kt, _NT, preferred_element_type=F32)

        s = lax.dot_general(q.astype(BF16), (k_f + k_b).astype(BF16), _NT,
                            preferred_element_type=F32)
        s = jnp.where(level == 0, s, 0.0)
        odd = (row & 1) == 1
        q1 = (q * jnp.where(odd, f_f, f_b)).astype(BF16)
        k1 = jnp.where(odd, k_b, k_f).astype(BF16)
        s = jnp.where(level == 1, lax.dot_general(q1, k1, _NT, preferred_element_type=F32), s)
        r4 = row & 3
        lf_prev, lf_next = pltpu.roll(l_f, 1, 0), pltpu.roll(l_f, c - 1, 0)
        lb_prev, lb_next = pltpu.roll(l_b, 1, 0), pltpu.roll(l_b, c - 1, 0)
        d_q = jnp.where(r4 == 0, l_b + lb_next,
                        jnp.where(r4 == 1, l_b, jnp.where(r4 == 2, l_f, lf_prev + l_f)))
        d_k = jnp.where(r4 == 0, lf_next, jnp.where(r4 == 3, lb_prev, 0.0))
        s = jnp.where(level == 2, level_scores(d_q, d_k, jnp.where(r4 >= 2, k_b, k_f)), s)
        for m in HGRN_LEVELS:
            if m % 8 == 0:
                dq_parts, dk_parts, k_parts = [], [], []
                for lo in range(0, c, 2 * m):
                    mid_row, hi_row = lo + m, lo + 2 * m
                    edge_f = b_f[mid_row - 1:mid_row]
                    edge_b = b_b[mid_row:mid_row + 1]
                    dq_parts += [b_b[lo:mid_row] - edge_b, b_f[mid_row:hi_row] - edge_f]
                    dk_parts += [edge_f - b_f[lo:mid_row], edge_b - b_b[mid_row:hi_row]]
                    k_parts += [k_f[lo:mid_row], k_b[mid_row:hi_row]]
                d_q = jnp.concatenate(dq_parts, axis=0)
                d_k = jnp.concatenate(dk_parts, axis=0)
                k_sel = jnp.concatenate(k_parts, axis=0)
            else:
                x_f = b_f - boundary_rows(b_f, m - 1, 2 * m)
                x_b = b_b - boundary_rows(b_b, m, 2 * m)
                second = ((row >> (m.bit_length() - 1)) & 1) == 1
                d_q = jnp.where(second, x_f, x_b)
                d_k = -jnp.where(second, x_b, x_f)
                k_sel = jnp.where(second, k_b, k_f)
            s = jnp.where(level == m.bit_length(), level_scores(d_q, d_k, k_sel), s)

        q_fb = jnp.concatenate([q * jnp.exp2(b_f), q * jnp.exp2(b_b)], axis=1).astype(BF16)
        k_fb = jnp.concatenate([k_f * jnp.exp2(b_f[c - 1:c] - b_f),
                                k_b * jnp.exp2(b_b[0:1] - b_b)], axis=1).astype(BF16)
        dec = jnp.concatenate([jnp.exp2(b_f[c - 1:c]), jnp.exp2(b_b[0:1])], axis=1)

        o_scr[rows, :] = jnp.dot(s.astype(BF16), v, preferred_element_type=F32)
        qfb_scr[rows, :] = q_fb
        dst_scr[ci] = lax.dot_general(v, k_fb, _TN, preferred_element_type=F32)
        dec_scr[ci] = jnp.broadcast_to(dec, (8, 2 * HEAD_D))
        return carry

    lax.fori_loop(0, n_chunks, local_pass, 0, unroll=2)

    if has_s0:
        st_f0 = s0_ref[0, 0, 0, 0].T
        st_b0 = s0_ref[0, 0, 1, 0].T
    else:
        st_f0 = jnp.zeros((HEAD_D, HEAD_D), F32)
        st_b0 = jnp.zeros((HEAD_D, HEAD_D), F32)

    def fwd_scan(ci, st):
        st_scr[ci, :, 0:HEAD_D] = st.astype(BF16)
        return st * dec_scr[ci, 0:1, 0:HEAD_D] + dst_scr[ci, :, 0:HEAD_D]

    def bwd_scan(k, st):
        ci = n_chunks - 1 - k
        st_scr[ci, :, HEAD_D:2 * HEAD_D] = st.astype(BF16)
        return st * dec_scr[ci, 0:1, HEAD_D:2 * HEAD_D] + dst_scr[ci, :, HEAD_D:2 * HEAD_D]

    st_f = lax.fori_loop(0, n_chunks, fwd_scan, st_f0)
    st_b = lax.fori_loop(0, n_chunks, bwd_scan, st_b0)
    if want_final:
        sfin_ref[0, 0, 0] = st_f.T
        sfin_ref[0, 1, 0] = st_b.T

    def output_pass(ci, carry):
        rows = chunk_rows(ci)
        o = o_scr[rows, :] + lax.dot_general(qfb_scr[rows, :], st_scr[ci], _NT,
                                             preferred_element_type=F32)
        y_ref[rows, :] = (_rms(o, gon_ref[...]) * _silu(ga_ref[rows, :])).astype(BF16)
        return carry

    lax.fori_loop(0, n_chunks, output_pass, 0, unroll=min(n_chunks, 4))


def _hgrn_mixer(proj, lb, g_onorm, state, layer_j, seq_len, want_final):
    n_tok = proj.shape[0]
    n_seq = n_tok // seq_len
    n_chunks = seq_len // HGRN_CHUNK
    has_s0 = state is not None
    blk = lambda part: pl.BlockSpec(
        (seq_len, HEAD_D), lambda b, h: (b, part * N_HEADS + h))
    in_specs = [blk(0), blk(1), blk(2), blk(3), blk(4),
                pl.BlockSpec((2, HEAD_D), lambda b, h: (0, h)),
                pl.BlockSpec((1, HEAD_D), lambda b, h: (0, 0)),
                pl.BlockSpec(_HGRN_TRI.shape, lambda b, h: (0, 0)),
                pl.BlockSpec(_HGRN_LEVEL.shape, lambda b, h: (0, 0))]
    args = [proj, proj, proj, proj, proj, lb, g_onorm.reshape(1, HEAD_D),
            jnp.asarray(_HGRN_TRI, BF16), jnp.asarray(_HGRN_LEVEL)]
    if has_s0:
        in_specs.append(pl.BlockSpec((1, 1, 2, 1, HEAD_D, HEAD_D),
                                     lambda b, h: (b, layer_j, 0, h, 0, 0)))
        args.append(state)
    out_specs = [pl.BlockSpec((seq_len, HEAD_D), lambda b, h: (b, h))]
    out_shape = [jax.ShapeDtypeStruct((n_tok, BRANCH_W), BF16)]
    if want_final:
        out_specs.append(pl.BlockSpec((1, 2, 1, HEAD_D, HEAD_D), lambda b, h: (b, 0, h, 0, 0)))
        out_shape.append(jax.ShapeDtypeStruct((n_seq, 2, N_HEADS, HEAD_D, HEAD_D), F32))
    outs = pl.pallas_call(
        functools.partial(_hgrn_kernel, n_chunks=n_chunks, has_s0=has_s0, want_final=want_final),
        grid=(n_seq, N_HEADS),
        in_specs=in_specs,
        out_specs=out_specs,
        out_shape=out_shape,
        scratch_shapes=[
            pltpu.VMEM((seq_len, HEAD_D), F32),
            pltpu.VMEM((seq_len, 2 * HEAD_D), BF16),
            pltpu.VMEM((n_chunks, HEAD_D, 2 * HEAD_D), F32),
            pltpu.VMEM((n_chunks, 8, 2 * HEAD_D), F32),
            pltpu.VMEM((n_chunks, HEAD_D, 2 * HEAD_D), BF16),
        ],
        compiler_params=_cparams(2, 48),
        name="hgrn_mixer",
    )(*args)
    return outs if want_final else (outs[0], None)


def kernel(x_prompt, x_sample, c, state_hgrn, c_ctx, w_ada, b_ada, g_pre, g_post, w_in_ab, w_out_ab,
           lb_logits, g_onorm_a, w_pool, pool_scale, w_in_c, w_out_c, ln_v_g, ln_v_b, w_spatial,
           b_spatial):
    n_batch, seq_p, _ = x_prompt.shape
    dec_batch, seq_s, _ = x_sample.shape
    tm = 512

    cond = jnp.concatenate(
        [c_ctx[None, :], c, jnp.zeros((N_COND_ROWS - 1 - dec_batch, D_MODEL), F32)], axis=0)
    mods = _modulations(cond, w_ada, b_ada).reshape(DEPTH, N_COND_ROWS, 3, D_MODEL)
    lb_all = _lower_bounds(lb_logits)

    tiles_per_sample = seq_s // tm
    flows = [
        dict(x=x_prompt.reshape(n_batch * seq_p, D_MODEL), seq=seq_p, grid=False, state=None,
             row=lambda i: 0),
        dict(x=x_sample.reshape(dec_batch * seq_s, D_MODEL), seq=seq_s, grid=True, state=state_hgrn,
             row=lambda i: 1 + i // tiles_per_sample),
    ]
    ctx_states = []
    for l in range(DEPTH):
        j = l // 2
        for fi, fl in enumerate(flows):
            if l % 2 == 0:
                w_in, w_out = w_in_ab[j].astype(BF16), w_out_ab[j].astype(BF16)
                proj = _in_proj(fl["x"], mods[l], g_pre[l], w_in, fl["row"], tm, 1024)
                y_a, s_fin = _hgrn_mixer(proj, lb_all[j], g_onorm_a[j], fl["state"], j, fl["seq"],
                                         want_final=(fi == 0))
                y_b = _pool_mixer(proj, w_pool[j], pool_scale[j], fl["seq"], fl["grid"])
                fl["x"] = _out_proj([y_a, y_b], [w_out[:BRANCH_W], w_out[BRANCH_W:]], fl["x"],
                                    mods[l], g_post[l], fl["row"], tm)
                if fi == 0:
                    ctx_states.append(s_fin)
            else:
                w_in, w_out = w_in_c[j].astype(BF16), w_out_c[j].astype(BF16)
                proj = _in_proj(fl["x"], mods[l], g_pre[l], w_in, fl["row"], tm, 1024)
                y = _mix_c(proj, ln_v_g[j], ln_v_b[j], w_spatial[j], b_spatial[j], 256)
                fl["x"] = _out_proj([y], [w_out], fl["x"], mods[l], g_post[l], fl["row"], tm)
    y_p = flows[0]["x"].reshape(n_batch, seq_p, D_MODEL)
    y_s = flows[1]["x"].reshape(dec_batch, seq_s, D_MODEL)
    return (y_p, y_s, jnp.stack(ctx_states, axis=1))
```

```python
import functools

import numpy as np
import jax
import jax.numpy as jnp
from jax import lax
from jax.experimental import pallas as pl
from jax.experimental.pallas import tpu as pltpu

F32 = jnp.float32
BF16 = jnp.bfloat16

D_MODEL = 1024
DEPTH = 4
MIX_WIDTH = 2 * D_MODEL
BRANCH_W = MIX_WIDTH // 2
HEAD_D = 128
N_HEADS = BRANCH_W // HEAD_D
GRID_W = 64
POOL_WINDOWS = (2, 4, 8, 16)
POOL_GROUP_W = BRANCH_W // len(POOL_WINDOWS)
CHUNK_C = 128
N_GROUPS_C = 8
GROUP_W_C = MIX_WIDTH // N_GROUPS_C
AB_IN = 7 * BRANCH_W
C_IN = 3 * MIX_WIDTH
AB_F32_PARTS = 2
AB_PART = {"q": 0, "i": 1, "gate_a": 2, "pool_in": 3, "gate_b": 4}
EPS = 1e-6
N_COND_ROWS = 8

HGRN_CHUNK = 128
HGRN_LEVELS = (64, 32, 16, 8, 4)
MIB = 1024 * 1024


def _cparams(n_axes, vmem_mib):
    return pltpu.CompilerParams(
        dimension_semantics=("arbitrary",) * n_axes, vmem_limit_bytes=int(vmem_mib * MIB))


def _silu(x):
    return x * jax.nn.sigmoid(x)


def _rms(x, g):
    return x * lax.rsqrt(jnp.mean(x * x, axis=-1, keepdims=True) + EPS) * g


def _mod_kernel(cond_ref, w_ref, b_ref, o_ref):
    a = _silu(cond_ref[...])
    o_ref[0] = jnp.dot(a, w_ref[0], preferred_element_type=F32,
                       precision=lax.Precision.HIGHEST) + b_ref[0]


def _modulations(cond, w_ada, b_ada):
    tn = 1024
    n3 = 3 * D_MODEL
    return pl.pallas_call(
        _mod_kernel,
        grid=(DEPTH, n3 // tn),
        in_specs=[
            pl.BlockSpec((N_COND_ROWS, D_MODEL), lambda l, j: (0, 0)),
            pl.BlockSpec((1, D_MODEL, tn), lambda l, j: (l, 0, j)),
            pl.BlockSpec((1, 1, tn), lambda l, j: (l, 0, j)),
        ],
        out_specs=pl.BlockSpec((1, N_COND_ROWS, tn), lambda l, j: (l, 0, j)),
        out_shape=jax.ShapeDtypeStruct((DEPTH, N_COND_ROWS, n3), F32),
        compiler_params=_cparams(2, 24),
        name="adaln_modulation",
    )(cond, w_ada, b_ada.reshape(DEPTH, 1, n3))


def _lb_kernel(x_ref, o_ref):
    x = x_ref[...]
    e = jnp.exp(x - jnp.max(x, axis=0, keepdims=True))
    p = e / jnp.sum(e, axis=0, keepdims=True)
    run = p[0]
    o_ref[0] = run - p[0]
    for l in range(1, x.shape[0]):
        run = run + p[l]
        o_ref[l] = run - p[0]


def _lower_bounds(lb_logits):
    return pl.pallas_call(
        _lb_kernel, out_shape=jax.ShapeDtypeStruct(lb_logits.shape, F32), name="hgrn_lower_bounds",
    )(lb_logits)


def _in_kernel(x_ref, mod_ref, g_ref, w_ref, *rest, n_f32_tiles):
    out_refs, h_ref = rest[:-1], rest[-1]
    j = pl.program_id(1)

    @pl.when(j == 0)
    def _():
        m = mod_ref[0]
        h = _rms(x_ref[...], g_ref[...]) * (1.0 + m[1:2]) + m[0:1]
        h_ref[...] = h.astype(BF16)

    def project(o_ref):
        o_ref[...] = jnp.dot(h_ref[...], w_ref[...], preferred_element_type=F32).astype(o_ref.dtype)

    if n_f32_tiles:
        pl.when(j < n_f32_tiles)(lambda: project(out_refs[0]))
        pl.when(j >= n_f32_tiles)(lambda: project(out_refs[1]))
    else:
        project(out_refs[0])


def _in_proj(x, mod, g, w, row_of_tile, tm, tn, n_f32_tiles=0):
    n_tok, n_out = x.shape[0], w.shape[1]
    n_tiles = n_out // tn
    out_specs, out_shape = [], []
    if n_f32_tiles:
        out_specs.append(pl.BlockSpec((tm, tn), lambda i, j: (i, jnp.minimum(j, n_f32_tiles - 1))))
        out_shape.append(jax.ShapeDtypeStruct((n_tok, n_f32_tiles * tn), F32))
    out_specs.append(pl.BlockSpec((tm, tn), lambda i, j: (i, jnp.maximum(j - n_f32_tiles, 0))))
    out_shape.append(jax.ShapeDtypeStruct((n_tok, (n_tiles - n_f32_tiles) * tn), BF16))
    return pl.pallas_call(
        functools.partial(_in_kernel, n_f32_tiles=n_f32_tiles),
        grid=(n_tok // tm, n_tiles),
        in_specs=[
            pl.BlockSpec((tm, D_MODEL), lambda i, j: (i, 0)),
            pl.BlockSpec((1, 3, D_MODEL), lambda i, j: (row_of_tile(i), 0, 0)),
            pl.BlockSpec((1, D_MODEL), lambda i, j: (0, 0)),
            pl.BlockSpec((D_MODEL, tn), lambda i, j: (0, j)),
        ],
        out_specs=out_specs,
        out_shape=out_shape,
        scratch_shapes=[pltpu.VMEM((tm, D_MODEL), BF16)],
        compiler_params=_cparams(2, 40),
        name="in_projection",
    )(x, mod, g.reshape(1, D_MODEL), w)


def _out_kernel(*refs, n_parts):
    y_refs, w_refs = refs[:n_parts], refs[n_parts:2 * n_parts]
    x_ref, mod_ref, g_ref, o_ref = refs[2 * n_parts:]
    acc = jnp.dot(y_refs[0][...], w_refs[0][...], preferred_element_type=F32)
    for y_ref, w_ref in zip(y_refs[1:], w_refs[1:]):
        acc = acc + jnp.dot(y_ref[...], w_ref[...], preferred_element_type=F32)
    o_ref[...] = x_ref[...] + mod_ref[0][2:3] * _rms(acc, g_ref[...])


def _out_proj(ys, ws, x, mod, g, row_of_tile, tm):
    n_tok = x.shape[0]
    n_parts = len(ys)
    in_specs = [pl.BlockSpec((tm, y.shape[1]), lambda i: (i, 0)) for y in ys]
    in_specs += [pl.BlockSpec(w.shape, lambda i: (0, 0)) for w in ws]
    in_specs += [
        pl.BlockSpec((tm, D_MODEL), lambda i: (i, 0)),
        pl.BlockSpec((1, 3, D_MODEL), lambda i: (row_of_tile(i), 0, 0)),
        pl.BlockSpec((1, D_MODEL), lambda i: (0, 0)),
    ]
    return pl.pallas_call(
        functools.partial(_out_kernel, n_parts=n_parts),
        grid=(n_tok // tm,),
        in_specs=in_specs,
        out_specs=pl.BlockSpec((tm, D_MODEL), lambda i: (i, 0)),
        out_shape=jax.ShapeDtypeStruct((n_tok, D_MODEL), F32),
        compiler_params=_cparams(1, 40),
        name="out_projection",
    )(*ys, *ws, x, mod, g.reshape(1, D_MODEL))


def _mixc_kernel(u_ref, v_ref, g_ref, lng_ref, lnb_ref, ws_ref, bs_ref, o_ref, *, tm):
    v = v_ref[...].astype(F32)
    mu = jnp.mean(v, axis=-1, keepdims=True)
    vc = v - mu
    var = jnp.mean(vc * vc, axis=-1, keepdims=True)
    vn = (vc * lax.rsqrt(var + EPS) * lng_ref[...] + lnb_ref[...]).astype(BF16)
    for n in range(tm // CHUNK_C):
        rows = slice(n * CHUNK_C, (n + 1) * CHUNK_C)
        for gi in range(N_GROUPS_C):
            cols = slice(gi * GROUP_W_C, (gi + 1) * GROUP_W_C)
            sp = jnp.dot(ws_ref[gi], vn[rows, cols], preferred_element_type=F32) + bs_ref[:, gi:gi + 1]
            gate = _silu(g_ref[rows, cols].astype(F32))
            o_ref[rows, cols] = (u_ref[rows, cols].astype(F32) * sp * gate).astype(BF16)


def _mix_c(proj, ln_g, ln_b, w_s, b_s, tm):
    n_tok = proj.shape[0]
    return pl.pallas_call(
        functools.partial(_mixc_kernel, tm=tm),
        grid=(n_tok // tm,),
        in_specs=[
            pl.BlockSpec((tm, MIX_WIDTH), lambda i: (i, 0)),
            pl.BlockSpec((tm, MIX_WIDTH), lambda i: (i, 1)),
            pl.BlockSpec((tm, MIX_WIDTH), lambda i: (i, 2)),
            pl.BlockSpec((1, MIX_WIDTH), lambda i: (0, 0)),
            pl.BlockSpec((1, MIX_WIDTH), lambda i: (0, 0)),
            pl.BlockSpec((N_GROUPS_C, CHUNK_C, CHUNK_C), lambda i: (0, 0, 0)),
            pl.BlockSpec((CHUNK_C, N_GROUPS_C), lambda i: (0, 0)),
        ],
        out_specs=pl.BlockSpec((tm, MIX_WIDTH), lambda i: (i, 0)),
        out_shape=jax.ShapeDtypeStruct((n_tok, MIX_WIDTH), BF16),
        compiler_params=_cparams(1, 40),
        name="gmlp_mixer",
    )(proj, proj, proj, ln_g.reshape(1, MIX_WIDTH), ln_b.reshape(1, MIX_WIDTH),
      w_s.astype(BF16), b_s.T)


def _centred_mean(x, w, stride, n):
    rows = x.shape[0]
    t = lax.broadcasted_iota(jnp.int32, x.shape, 0)
    idx = (t >> (stride.bit_length() - 1)) & (n - 1)
    half = w // 2
    trail, lead = x, x
    size = 1
    while size < half:
        sh = size * stride
        trail = trail + jnp.where(idx >= size, pltpu.roll(trail, sh, 0), 0.0)
        lead = lead + jnp.where(idx + size < n, pltpu.roll(lead, rows - sh, 0), 0.0)
        size *= 2
    s = jnp.where(idx >= 1, pltpu.roll(trail, stride, 0), 0.0) + lead
    cnt = jnp.minimum(idx - half + w, n) - jnp.maximum(idx - half, 0)
    return s / cnt.astype(F32)


def _pool_kernel(p_ref, gate_ref, wp_ref, ps_ref, y_ref, *, seq_len, grid_mode):
    grp = pl.program_id(1)
    for k, w in enumerate(POOL_WINDOWS):
        @pl.when(grp == k)
        def _(w=w):
            x = p_ref[...].astype(F32)
            if grid_mode:
                m = _centred_mean(x, w, GRID_W, seq_len // GRID_W)
                m = _centred_mean(m, w, 1, GRID_W)
            else:
                m = _centred_mean(x, w, 1, seq_len)
            dlt = (m - x).astype(BF16)
            y = jnp.dot(dlt, wp_ref[0], preferred_element_type=F32) * ps_ref[...]
            y_ref[...] = (y * _silu(gate_ref[...].astype(F32))).astype(BF16)


def _pool_mixer(proj, w_pool, pool_scale, seq_len, grid_mode):
    n_tok = proj.shape[0]
    n_grp = len(POOL_WINDOWS)
    p_blk0 = AB_PART["pool_in"] * BRANCH_W // POOL_GROUP_W
    g_blk0 = AB_PART["gate_b"] * BRANCH_W // POOL_GROUP_W
    return pl.pallas_call(
        functools.partial(_pool_kernel, seq_len=seq_len, grid_mode=grid_mode),
        grid=(n_tok // seq_len, n_grp),
        in_specs=[
            pl.BlockSpec((seq_len, POOL_GROUP_W), lambda b, g: (b, p_blk0 + g)),
            pl.BlockSpec((seq_len, POOL_GROUP_W), lambda b, g: (b, g_blk0 + g)),
            pl.BlockSpec((1, POOL_GROUP_W, POOL_GROUP_W), lambda b, g: (g, 0, 0)),
            pl.BlockSpec((1, POOL_GROUP_W), lambda b, g: (0, g)),
        ],
        out_specs=pl.BlockSpec((seq_len, POOL_GROUP_W), lambda b, g: (b, g)),
        out_shape=jax.ShapeDtypeStruct((n_tok, BRANCH_W), BF16),
        compiler_params=_cparams(2, 56),
        name="pool_mixer",
    )(proj, proj, w_pool.astype(BF16), pool_scale.reshape(1, BRANCH_W))


def _hgrn_triangles():
    ones = np.ones((HGRN_CHUNK, HGRN_CHUNK), np.float32)
    return np.concatenate([np.tril(ones), np.triu(ones)], axis=0)


def _hgrn_level_index():
    i = np.arange(HGRN_CHUNK)
    x = i[:, None] ^ i[None, :]
    lv = np.zeros_like(x)
    nz = x > 0
    lv[nz] = np.floor(np.log2(x[nz])).astype(x.dtype) + 1
    return lv.astype(np.int32)


_HGRN_TRI = _hgrn_triangles()
_HGRN_LEVEL = _hgrn_level_index()

_NT = (((1,), (1,)), ((), ()))
_TN = (((0,), (0,)), ((), ()))


def _hgrn_kernel(*refs, n_chunks, has_s0, want_final):
    q_ref, ff_ref, fb_ref, v_ref, ga_ref, lb_ref, gon_ref, tri_ref, lvl_ref = refs[:9]
    pos = 9
    s0_ref = sfin_ref = None
    if has_s0:
        s0_ref = refs[pos]
        pos += 1
    y_ref = refs[pos]
    pos += 1
    if want_final:
        sfin_ref = refs[pos]
        pos += 1
    o_scr, qfb_scr, dst_scr, dec_scr, st_scr = refs[pos:]

    c = HGRN_CHUNK
    lbv = lb_ref[...]
    lb_f, lb_b = lbv[0:1], lbv[1:2]
    level = lvl_ref[...]
    row = lax.broadcasted_iota(jnp.int32, (c, HEAD_D), 0)

    def chunk_rows(ci):
        return pl.ds(pl.multiple_of(ci * c, c), c)

    def cumulative(tri, x):
        hi = x.astype(BF16)
        mid = (x - hi.astype(F32)).astype(BF16)
        dd = jnp.dot(tri, jnp.concatenate([hi, mid], axis=1), preferred_element_type=F32)
        return dd[:, :HEAD_D] + dd[:, HEAD_D:]

    def boundary_rows(b, first, block):
        return jnp.concatenate(
            [jnp.broadcast_to(b[r:r + 1, :], (block, HEAD_D)) for r in range(first, c, block)], axis=0)

    def local_pass(ci, carry):
        rows = chunk_rows(ci)
        f_f = lb_f + (1.0 - lb_f) * jax.nn.sigmoid(ff_ref[rows, :])
        f_b = lb_b + (1.0 - lb_b) * jax.nn.sigmoid(fb_ref[rows, :])
        k_f, k_b = 1.0 - f_f, 1.0 - f_b
        l_f, l_b = jnp.log2(f_f), jnp.log2(f_b)
        b_f = cumulative(tri_ref[0:c, :], l_f)
        b_b = cumulative(tri_ref[c:2 * c, :], l_b)

        q = _silu(q_ref[rows, :].astype(F32))
        v = v_ref[rows, :]

        def level_scores(d_q, d_k, k_sel):
            qt = (q * jnp.exp2(d_q)).astype(BF16)
            kt = (k_sel * jnp.exp2(d_k)).astype(BF16)
            return lax.dot_general(qt, kt, _NT, preferred_element_type=F32)

        s = lax.dot_general(q.astype(BF16), (k_f + k_b).astype(BF16), _NT,
                            preferred_element_type=F32)
        s = jnp.where(level == 0, s, 0.0)
        odd = (row & 1) == 1
        q1 = (q * jnp.where(odd, f_f, f_b)).astype(BF16)
        k1 = jnp.where(odd, k_b, k_f).astype(BF16)
        s = jnp.where(level == 1, lax.dot_general(q1, k1, _NT, preferred_element_type=F32), s)
        r4 = row & 3
        lf_prev, lf_next = pltpu.roll(l_f, 1, 0), pltpu.roll(l_f, c - 1, 0)
        lb_prev, lb_next = pltpu.roll(l_b, 1, 0), pltpu.roll(l_b, c - 1, 0)
        d_q = jnp.where(r4 == 0, l_b + lb_next,
                        jnp.where(r4 == 1, l_b, jnp.where(r4 == 2, l_f, lf_prev + l_f)))
        d_k = jnp.where(r4 == 0, lf_next, jnp.where(r4 == 3, lb_prev, 0.0))
        s = jnp.where(level == 2, level_scores(d_q, d_k, jnp.where(r4 >= 2, k_b, k_f)), s)
        for m in HGRN_LEVELS:
            if m % 8 == 0:
                dq_parts, dk_parts, k_parts = [], [], []
                for lo in range(0, c, 2 * m):
                    mid_row, hi_row = lo + m, lo + 2 * m
                    edge_f = b_f[mid_row - 1:mid_row]
                    edge_b = b_b[mid_row:mid_row + 1]
                    dq_parts += [b_b[lo:mid_row] - edge_b, b_f[mid_row:hi_row] - edge_f]
                    dk_parts += [edge_f - b_f[lo:mid_row], edge_b - b_b[mid_row:hi_row]]
                    k_parts += [k_f[lo:mid_row], k_b[mid_row:hi_row]]
                d_q = jnp.concatenate(dq_parts, axis=0)
                d_k = jnp.concatenate(dk_parts, axis=0)
                k_sel = jnp.concatenate(k_parts, axis=0)
            else:
                x_f = b_f - boundary_rows(b_f, m - 1, 2 * m)
                x_b = b_b - boundary_rows(b_b, m, 2 * m)
                second = ((row >> (m.bit_length() - 1)) & 1) == 1
                d_q = jnp.where(second, x_f, x_b)
                d_k = -jnp.where(second, x_b, x_f)
                k_sel = jnp.where(second, k_b, k_f)
            s = jnp.where(level == m.bit_length(), level_scores(d_q, d_k, k_sel), s)

        q_fb = jnp.concatenate([q * jnp.exp2(b_f), q * jnp.exp2(b_b)], axis=1).astype(BF16)
        k_fb = jnp.concatenate([k_f * jnp.exp2(b_f[c - 1:c] - b_f),
                                k_b * jnp.exp2(b_b[0:1] - b_b)], axis=1).astype(BF16)
        dec = jnp.concatenate([jnp.exp2(b_f[c - 1:c]), jnp.exp2(b_b[0:1])], axis=1)

        o_scr[rows, :] = jnp.dot(s.astype(BF16), v, preferred_element_type=F32)
        qfb_scr[rows, :] = q_fb
        dst_scr[ci] = lax.dot_general(v, k_fb, _TN, preferred_element_type=F32)
        dec_scr[ci] = jnp.broadcast_to(dec, (8, 2 * HEAD_D))
        return carry

    lax.fori_loop(0, n_chunks, local_pass, 0, unroll=2)

    if has_s0:
        st_f0 = s0_ref[0, 0, 0, 0].T
        st_b0 = s0_ref[0, 0, 1, 0].T
    else:
        st_f0 = jnp.zeros((HEAD_D, HEAD_D), F32)
        st_b0 = jnp.zeros((HEAD_D, HEAD_D), F32)

    def fwd_scan(ci, st):
        st_scr[ci, :, 0:HEAD_D] = st.astype(BF16)
        return st * dec_scr[ci, 0:1, 0:HEAD_D] + dst_scr[ci, :, 0:HEAD_D]

    def bwd_scan(k, st):
        ci = n_chunks - 1 - k
        st_scr[ci, :, HEAD_D:2 * HEAD_D] = st.astype(BF16)
        return st * dec_scr[ci, 0:1, HEAD_D:2 * HEAD_D] + dst_scr[ci, :, HEAD_D:2 * HEAD_D]

    st_f = lax.fori_loop(0, n_chunks, fwd_scan, st_f0)
    st_b = lax.fori_loop(0, n_chunks, bwd_scan, st_b0)
    if want_final:
        sfin_ref[0, 0, 0] = st_f.T
        sfin_ref[0, 1, 0] = st_b.T

    def output_pass(ci, carry):
        rows = chunk_rows(ci)
        o = o_scr[rows, :] + lax.dot_general(qfb_scr[rows, :], st_scr[ci], _NT,
                                             preferred_element_type=F32)
        y_ref[rows, :] = (_rms(o, gon_ref[...]) * _silu(ga_ref[rows, :].astype(F32))).astype(BF16)
        return carry

    lax.fori_loop(0, n_chunks, output_pass, 0, unroll=min(n_chunks, 4))


def _hgrn_mixer(gates, proj, lb, g_onorm, state, layer_j, seq_len, want_final):
    n_tok = proj.shape[0]
    n_seq = n_tok // seq_len
    n_chunks = seq_len // HGRN_CHUNK
    has_s0 = state is not None
    blk = lambda part: pl.BlockSpec(
        (seq_len, HEAD_D), lambda b, h: (b, part * N_HEADS + h))
    in_specs = [blk(AB_PART["q"]), blk(0), blk(1), blk(AB_PART["i"]), blk(AB_PART["gate_a"]),
                pl.BlockSpec((2, HEAD_D), lambda b, h: (0, h)),
                pl.BlockSpec((1, HEAD_D), lambda b, h: (0, 0)),
                pl.BlockSpec(_HGRN_TRI.shape, lambda b, h: (0, 0)),
                pl.BlockSpec(_HGRN_LEVEL.shape, lambda b, h: (0, 0))]
    args = [proj, gates, gates, proj, proj, lb, g_onorm.reshape(1, HEAD_D),
            jnp.asarray(_HGRN_TRI, BF16), jnp.asarray(_HGRN_LEVEL)]
    if has_s0:
        in_specs.append(pl.BlockSpec((1, 1, 2, 1, HEAD_D, HEAD_D),
                                     lambda b, h: (b, layer_j, 0, h, 0, 0)))
        args.append(state)
    out_specs = [pl.BlockSpec((seq_len, HEAD_D), lambda b, h: (b, h))]
    out_shape = [jax.ShapeDtypeStruct((n_tok, BRANCH_W), BF16)]
    if want_final:
        out_specs.append(pl.BlockSpec((1, 2, 1, HEAD_D, HEAD_D), lambda b, h: (b, 0, h, 0, 0)))
        out_shape.append(jax.ShapeDtypeStruct((n_seq, 2, N_HEADS, HEAD_D, HEAD_D), F32))
    outs = pl.pallas_call(
        functools.partial(_hgrn_kernel, n_chunks=n_chunks, has_s0=has_s0, want_final=want_final),
        grid=(n_seq, N_HEADS),
        in_specs=in_specs,
        out_specs=out_specs,
        out_shape=out_shape,
        scratch_shapes=[
            pltpu.VMEM((seq_len, HEAD_D), F32),
            pltpu.VMEM((seq_len, 2 * HEAD_D), BF16),
            pltpu.VMEM((n_chunks, HEAD_D, 2 * HEAD_D), F32),
            pltpu.VMEM((n_chunks, 8, 2 * HEAD_D), F32),
            pltpu.VMEM((n_chunks, HEAD_D, 2 * HEAD_D), BF16),
        ],
        compiler_params=_cparams(2, 48),
        name="hgrn_mixer",
    )(*args)
    return outs if want_final else (outs[0], None)


def kernel(x_prompt, x_sample, c, state_hgrn, c_ctx, w_ada, b_ada, g_pre, g_post, w_in_ab, w_out_ab,
           lb_logits, g_onorm_a, w_pool, pool_scale, w_in_c, w_out_c, ln_v_g, ln_v_b, w_spatial,
           b_spatial):
    n_batch, seq_p, _ = x_prompt.shape
    dec_batch, seq_s, _ = x_sample.shape
    tm_in, tm_out, tn = 1024, 512, 1024

    cond = jnp.concatenate(
        [c_ctx[None, :], c, jnp.zeros((N_COND_ROWS - 1 - dec_batch, D_MODEL), F32)], axis=0)
    mods = _modulations(cond, w_ada, b_ada).reshape(DEPTH, N_COND_ROWS, 3, D_MODEL)
    lb_all = _lower_bounds(lb_logits)

    flows = [
        dict(x=x_prompt.reshape(n_batch * seq_p, D_MODEL), seq=seq_p, grid=False, state=None,
             row=lambda tm: (lambda i: 0)),
        dict(x=x_sample.reshape(dec_batch * seq_s, D_MODEL), seq=seq_s, grid=True, state=state_hgrn,
             row=lambda tm: (lambda i: 1 + i // (seq_s // tm))),
    ]
    ctx_states = []
    for l in range(DEPTH):
        j = l // 2
        if l % 2 == 0:
            w = w_in_ab[j]
            w_in = jnp.concatenate([w[:, BRANCH_W:3 * BRANCH_W], w[:, :BRANCH_W], w[:, 3 * BRANCH_W:]],
                                   axis=1).astype(BF16)
            w_out = w_out_ab[j].astype(BF16)
            for fi, fl in enumerate(flows):
                gates, proj = _in_proj(fl["x"], mods[l], g_pre[l], w_in, fl["row"](tm_in), tm_in, tn,
                                       n_f32_tiles=AB_F32_PARTS * BRANCH_W // tn)
                y_a, s_fin = _hgrn_mixer(gates, proj, lb_all[j], g_onorm_a[j], fl["state"], j,
                                         fl["seq"], want_final=(fi == 0))
                y_b = _pool_mixer(proj, w_pool[j], pool_scale[j], fl["seq"], fl["grid"])
                fl["x"] = _out_proj([y_a, y_b], [w_out[:BRANCH_W], w_out[BRANCH_W:]], fl["x"],
                                    mods[l], g_post[l], fl["row"](tm_out), tm_out)
                if fi == 0:
                    ctx_states.append(s_fin)
        else:
            w_in, w_out = w_in_c[j].astype(BF16), w_out_c[j].astype(BF16)
            for fl in flows:
                (proj,) = _in_proj(fl["x"], mods[l], g_pre[l], w_in, fl["row"](tm_in), tm_in, tn)
                y = _mix_c(proj, ln_v_g[j], ln_v_b[j], w_spatial[j], b_spatial[j], 256)
                fl["x"] = _out_proj([y], [w_out], fl["x"], mods[l], g_post[l], fl["row"](tm_out),
                                    tm_out)
    y_p = flows[0]["x"].reshape(n_batch, seq_p, D_MODEL)
    y_s = flows[1]["x"].reshape(dec_batch, seq_s, D_MODEL)
    return (y_p, y_s, jnp.stack(ctx_states, axis=1))
```

```python
import functools

import numpy as np
import jax
import jax.numpy as jnp
from jax import lax
from jax.experimental import pallas as pl
from jax.experimental.pallas import tpu as pltpu

F32 = jnp.float32
BF16 = jnp.bfloat16

D_MODEL = 1024
DEPTH = 4
MIX_WIDTH = 2 * D_MODEL
BRANCH_W = MIX_WIDTH // 2
HEAD_D = 128
N_HEADS = BRANCH_W // HEAD_D
GRID_W = 64
POOL_WINDOWS = (2, 4, 8, 16)
POOL_GROUP_W = BRANCH_W // len(POOL_WINDOWS)
CHUNK_C = 128
N_GROUPS_C = 8
GROUP_W_C = MIX_WIDTH // N_GROUPS_C
AB_IN = 7 * BRANCH_W
C_IN = 3 * MIX_WIDTH
AB_F32_PARTS = 2
AB_PART = {"q": 0, "i": 1, "gate_a": 2, "pool_in": 3, "gate_b": 4}
EPS = 1e-6
N_COND_ROWS = 8

HGRN_CHUNK = 128
HGRN_LEVELS = (64, 32, 16, 8, 4)
MIB = 1024 * 1024


def _cparams(n_axes, vmem_mib):
    return pltpu.CompilerParams(
        dimension_semantics=("arbitrary",) * n_axes, vmem_limit_bytes=int(vmem_mib * MIB))


def _resident(shape):
    zeros = (0,) * len(shape)
    return pl.BlockSpec(shape, lambda *_: zeros, pipeline_mode=pl.Buffered(1))


def _silu(x):
    return x * jax.nn.sigmoid(x)


def _rms(x, g):
    return x * lax.rsqrt(jnp.mean(x * x, axis=-1, keepdims=True) + EPS) * g


def _modulated_norm(x, mod, g):
    return _rms(x, g) * (1.0 + mod[1:2]) + mod[0:1]


def _mod_kernel(cond_ref, w_ref, b_ref, o_ref):
    a = _silu(cond_ref[...])
    o_ref[0] = jnp.dot(a, w_ref[0], preferred_element_type=F32,
                       precision=lax.Precision.HIGHEST) + b_ref[0]


def _modulations(cond, w_ada, b_ada):
    tn = 1024
    n3 = 3 * D_MODEL
    return pl.pallas_call(
        _mod_kernel,
        grid=(DEPTH, n3 // tn),
        in_specs=[
            pl.BlockSpec((N_COND_ROWS, D_MODEL), lambda l, j: (0, 0)),
            pl.BlockSpec((1, D_MODEL, tn), lambda l, j: (l, 0, j)),
            pl.BlockSpec((1, 1, tn), lambda l, j: (l, 0, j)),
        ],
        out_specs=pl.BlockSpec((1, N_COND_ROWS, tn), lambda l, j: (l, 0, j)),
        out_shape=jax.ShapeDtypeStruct((DEPTH, N_COND_ROWS, n3), F32),
        compiler_params=_cparams(2, 24),
        name="adaln_modulation",
    )(cond, w_ada, b_ada.reshape(DEPTH, 1, n3))


def _lb_kernel(x_ref, o_ref):
    x = x_ref[...]
    e = jnp.exp(x - jnp.max(x, axis=0, keepdims=True))
    p = e / jnp.sum(e, axis=0, keepdims=True)
    run = p[0]
    o_ref[0] = run - p[0]
    for l in range(1, x.shape[0]):
        run = run + p[l]
        o_ref[l] = run - p[0]


def _lower_bounds(lb_logits):
    return pl.pallas_call(
        _lb_kernel, out_shape=jax.ShapeDtypeStruct(lb_logits.shape, F32), name="hgrn_lower_bounds",
    )(lb_logits)


def _in_kernel(x_ref, mod_ref, g_ref, w_ref, *rest, n_f32_tiles):
    out_refs, h_ref = rest[:-1], rest[-1]
    j = pl.program_id(1)

    @pl.when(j == 0)
    def _():
        h_ref[...] = _modulated_norm(x_ref[...], mod_ref[0], g_ref[...]).astype(BF16)

    def project(o_ref):
        o_ref[...] = jnp.dot(h_ref[...], w_ref[...], preferred_element_type=F32).astype(o_ref.dtype)

    if n_f32_tiles:
        pl.when(j < n_f32_tiles)(lambda: project(out_refs[0]))
        pl.when(j >= n_f32_tiles)(lambda: project(out_refs[1]))
    else:
        project(out_refs[0])


def _in_proj(x, mod, g, w, row_of_tile, tm, tn, n_f32_tiles=0):
    n_tok, n_out = x.shape[0], w.shape[1]
    n_tiles = n_out // tn
    out_specs, out_shape = [], []
    if n_f32_tiles:
        out_specs.append(pl.BlockSpec((tm, tn), lambda i, j: (i, jnp.minimum(j, n_f32_tiles - 1))))
        out_shape.append(jax.ShapeDtypeStruct((n_tok, n_f32_tiles * tn), F32))
    out_specs.append(pl.BlockSpec((tm, tn), lambda i, j: (i, jnp.maximum(j - n_f32_tiles, 0))))
    out_shape.append(jax.ShapeDtypeStruct((n_tok, (n_tiles - n_f32_tiles) * tn), BF16))
    return pl.pallas_call(
        functools.partial(_in_kernel, n_f32_tiles=n_f32_tiles),
        grid=(n_tok // tm, n_tiles),
        in_specs=[
            pl.BlockSpec((tm, D_MODEL), lambda i, j: (i, 0)),
            pl.BlockSpec((1, 3, D_MODEL), lambda i, j: (row_of_tile(i), 0, 0)),
            pl.BlockSpec((1, D_MODEL), lambda i, j: (0, 0)),
            pl.BlockSpec((D_MODEL, tn), lambda i, j: (0, j)),
        ],
        out_specs=out_specs,
        out_shape=out_shape,
        scratch_shapes=[pltpu.VMEM((tm, D_MODEL), BF16)],
        compiler_params=_cparams(2, 40),
        name="in_projection",
    )(x, mod, g.reshape(1, D_MODEL), w)


def _out_kernel(*refs, n_parts):
    y_refs, w_refs = refs[:n_parts], refs[n_parts:2 * n_parts]
    x_ref, mod_ref, g_ref, o_ref = refs[2 * n_parts:]
    acc = jnp.dot(y_refs[0][...], w_refs[0][...], preferred_element_type=F32)
    for y_ref, w_ref in zip(y_refs[1:], w_refs[1:]):
        acc = acc + jnp.dot(y_ref[...], w_ref[...], preferred_element_type=F32)
    o_ref[...] = x_ref[...] + mod_ref[0][2:3] * _rms(acc, g_ref[...])


def _out_proj(ys, ws, x, mod, g, row_of_tile, tm):
    n_tok = x.shape[0]
    n_parts = len(ys)
    in_specs = [pl.BlockSpec((tm, y.shape[1]), lambda i: (i, 0)) for y in ys]
    in_specs += [pl.BlockSpec(w.shape, lambda i: (0, 0)) for w in ws]
    in_specs += [
        pl.BlockSpec((tm, D_MODEL), lambda i: (i, 0)),
        pl.BlockSpec((1, 3, D_MODEL), lambda i: (row_of_tile(i), 0, 0)),
        pl.BlockSpec((1, D_MODEL), lambda i: (0, 0)),
    ]
    return pl.pallas_call(
        functools.partial(_out_kernel, n_parts=n_parts),
        grid=(n_tok // tm,),
        in_specs=in_specs,
        out_specs=pl.BlockSpec((tm, D_MODEL), lambda i: (i, 0)),
        out_shape=jax.ShapeDtypeStruct((n_tok, D_MODEL), F32),
        compiler_params=_cparams(1, 40),
        name="out_projection",
    )(*ys, *ws, x, mod, g.reshape(1, D_MODEL))


def _layer_c_kernel(x_ref, mod_ref, gpre_ref, win_ref, lng_ref, lnb_ref, ws_ref, bs_ref, wout_ref,
                    gpost_ref, o_ref, y_scr, *, tm):
    x = x_ref[...]
    mod = mod_ref[0]
    h = _modulated_norm(x, mod, gpre_ref[...]).astype(BF16)

    v = jnp.dot(h, win_ref[:, MIX_WIDTH:2 * MIX_WIDTH], preferred_element_type=F32)
    mu = jnp.mean(v, axis=-1, keepdims=True)
    vc = v - mu
    var = jnp.mean(vc * vc, axis=-1, keepdims=True)
    vn = (vc * lax.rsqrt(var + EPS) * lng_ref[...] + lnb_ref[...]).astype(BF16)

    for gi in range(N_GROUPS_C):
        cols = slice(gi * GROUP_W_C, (gi + 1) * GROUP_W_C)
        u = jnp.dot(h, win_ref[:, cols], preferred_element_type=F32)
        gate = _silu(jnp.dot(h, win_ref[:, 2 * MIX_WIDTH + gi * GROUP_W_C:
                                        2 * MIX_WIDTH + (gi + 1) * GROUP_W_C],
                             preferred_element_type=F32))
        for n in range(tm // CHUNK_C):
            rows = slice(n * CHUNK_C, (n + 1) * CHUNK_C)
            sp = jnp.dot(ws_ref[gi], vn[rows, cols], preferred_element_type=F32) + bs_ref[:, gi:gi + 1]
            y_scr[rows, cols] = (u[rows] * sp * gate[rows]).astype(BF16)

    acc = jnp.dot(y_scr[...], wout_ref[...], preferred_element_type=F32)
    o_ref[...] = x + mod[2:3] * _rms(acc, gpost_ref[...])


def _layer_c(x, mod, g_pre, g_post, w_in, w_out, ln_g, ln_b, w_s, b_s, row_of_tile, tm):
    n_tok = x.shape[0]
    return pl.pallas_call(
        functools.partial(_layer_c_kernel, tm=tm),
        grid=(n_tok // tm,),
        in_specs=[
            pl.BlockSpec((tm, D_MODEL), lambda i: (i, 0)),
            pl.BlockSpec((1, 3, D_MODEL), lambda i: (row_of_tile(i), 0, 0)),
            _resident((1, D_MODEL)),
            _resident((D_MODEL, C_IN)),
            _resident((1, MIX_WIDTH)),
            _resident((1, MIX_WIDTH)),
            _resident((N_GROUPS_C, CHUNK_C, CHUNK_C)),
            _resident((CHUNK_C, N_GROUPS_C)),
            _resident((MIX_WIDTH, D_MODEL)),
            _resident((1, D_MODEL)),
        ],
        out_specs=pl.BlockSpec((tm, D_MODEL), lambda i: (i, 0)),
        out_shape=jax.ShapeDtypeStruct((n_tok, D_MODEL), F32),
        scratch_shapes=[pltpu.VMEM((tm, MIX_WIDTH), BF16)],
        compiler_params=_cparams(1, 48),
        name="gmlp_layer",
    )(x, mod, g_pre.reshape(1, D_MODEL), w_in, ln_g.reshape(1, MIX_WIDTH), ln_b.reshape(1, MIX_WIDTH),
      w_s.astype(BF16), b_s.T, w_out, g_post.reshape(1, D_MODEL))


def _centred_mean(x, w, stride, n):
    rows = x.shape[0]
    t = lax.broadcasted_iota(jnp.int32, x.shape, 0)
    idx = (t >> (stride.bit_length() - 1)) & (n - 1)
    half = w // 2
    trail, lead = x, x
    size = 1
    while size < half:
        sh = size * stride
        trail = trail + jnp.where(idx >= size, pltpu.roll(trail, sh, 0), 0.0)
        lead = lead + jnp.where(idx + size < n, pltpu.roll(lead, rows - sh, 0), 0.0)
        size *= 2
    s = jnp.where(idx >= 1, pltpu.roll(trail, stride, 0), 0.0) + lead
    cnt = jnp.minimum(idx - half + w, n) - jnp.maximum(idx - half, 0)
    return s / cnt.astype(F32)


def _pool_kernel(p_ref, gate_ref, wp_ref, ps_ref, y_ref, *, seq_len, grid_mode):
    grp = pl.program_id(1)
    for k, w in enumerate(POOL_WINDOWS):
        @pl.when(grp == k)
        def _(w=w):
            x = p_ref[...].astype(F32)
            if grid_mode:
                m = _centred_mean(x, w, GRID_W, seq_len // GRID_W)
                m = _centred_mean(m, w, 1, GRID_W)
            else:
                m = _centred_mean(x, w, 1, seq_len)
            dlt = (m - x).astype(BF16)
            y = jnp.dot(dlt, wp_ref[0], preferred_element_type=F32) * ps_ref[...]
            y_ref[...] = (y * _silu(gate_ref[...].astype(F32))).astype(BF16)


def _pool_mixer(proj, w_pool, pool_scale, seq_len, grid_mode, rows_per_step):
    n_tok = proj.shape[0]
    n_grp = len(POOL_WINDOWS)
    p_blk0 = AB_PART["pool_in"] * BRANCH_W // POOL_GROUP_W
    g_blk0 = AB_PART["gate_b"] * BRANCH_W // POOL_GROUP_W
    return pl.pallas_call(
        functools.partial(_pool_kernel, seq_len=seq_len, grid_mode=grid_mode),
        grid=(n_tok // rows_per_step, n_grp),
        in_specs=[
            pl.BlockSpec((rows_per_step, POOL_GROUP_W), lambda b, g: (b, p_blk0 + g)),
            pl.BlockSpec((rows_per_step, POOL_GROUP_W), lambda b, g: (b, g_blk0 + g)),
            pl.BlockSpec((1, POOL_GROUP_W, POOL_GROUP_W), lambda b, g: (g, 0, 0)),
            pl.BlockSpec((1, POOL_GROUP_W), lambda b, g: (0, g)),
        ],
        out_specs=pl.BlockSpec((rows_per_step, POOL_GROUP_W), lambda b, g: (b, g)),
        out_shape=jax.ShapeDtypeStruct((n_tok, BRANCH_W), BF16),
        compiler_params=_cparams(2, 56),
        name="pool_mixer",
    )(proj, proj, w_pool.astype(BF16), pool_scale.reshape(1, BRANCH_W))


def _hgrn_triangles():
    ones = np.ones((HGRN_CHUNK, HGRN_CHUNK), np.float32)
    return np.concatenate([np.tril(ones), np.triu(ones)], axis=0)


def _hgrn_level_index():
    i = np.arange(HGRN_CHUNK)
    x = i[:, None] ^ i[None, :]
    lv = np.zeros_like(x)
    nz = x > 0
    lv[nz] = np.floor(np.log2(x[nz])).astype(x.dtype) + 1
    return lv.astype(np.int32)


_HGRN_TRI = _hgrn_triangles()
_HGRN_LEVEL = _hgrn_level_index()

_NT = (((1,), (1,)), ((), ()))
_TN = (((0,), (0,)), ((), ()))


def _hgrn_kernel(*refs, n_chunks, has_s0, want_final):
    q_ref, ff_ref, fb_ref, v_ref, ga_ref, lb_ref, gon_ref, tri_ref, lvl_ref = refs[:9]
    pos = 9
    s0_ref = sfin_ref = None
    if has_s0:
        s0_ref = refs[pos]
        pos += 1
    y_ref = refs[pos]
    pos += 1
    if want_final:
        sfin_ref = refs[pos]
        pos += 1
    o_scr, qfb_scr, dst_scr, dec_scr, st_scr = refs[pos:]

    c = HGRN_CHUNK
    lbv = lb_ref[...]
    lb_f, lb_b = lbv[0:1], lbv[1:2]
    level = lvl_ref[...]
    row = lax.broadcasted_iota(jnp.int32, (c, HEAD_D), 0)

    def chunk_rows(ci):
        return pl.ds(pl.multiple_of(ci * c, c), c)

    def cumulative(tri, x):
        hi = x.astype(BF16)
        mid = (x - hi.astype(F32)).astype(BF16)
        dd = jnp.dot(tri, jnp.concatenate([hi, mid], axis=1), preferred_element_type=F32)
        return dd[:, :HEAD_D] + dd[:, HEAD_D:]

    def boundary_rows(b, first, block):
        return jnp.concatenate(
            [jnp.broadcast_to(b[r:r + 1, :], (block, HEAD_D)) for r in range(first, c, block)], axis=0)

    def local_pass(ci, carry):
        rows = chunk_rows(ci)
        f_f = lb_f + (1.0 - lb_f) * jax.nn.sigmoid(ff_ref[rows, :])
        f_b = lb_b + (1.0 - lb_b) * jax.nn.sigmoid(fb_ref[rows, :])
        k_f, k_b = 1.0 - f_f, 1.0 - f_b
        l_f, l_b = jnp.log2(f_f), jnp.log2(f_b)
        b_f = cumulative(tri_ref[0:c, :], l_f)
        b_b = cumulative(tri_ref[c:2 * c, :], l_b)

        q = _silu(q_ref[rows, :].astype(F32))
        v = v_ref[rows, :]

        def level_scores(d_q, d_k, k_sel):
            qt = (q * jnp.exp2(d_q)).astype(BF16)
            kt = (k_sel * jnp.exp2(d_k)).astype(BF16)
            return lax.dot_general(qt, kt, _NT, preferred_element_type=F32)

        s = lax.dot_general(q.astype(BF16), (k_f + k_b).astype(BF16), _NT,
                            preferred_element_type=F32)
        s = jnp.where(level == 0, s, 0.0)
        odd = (row & 1) == 1
        q1 = (q * jnp.where(odd, f_f, f_b)).astype(BF16)
        k1 = jnp.where(odd, k_b, k_f).astype(BF16)
        s = jnp.where(level == 1, lax.dot_general(q1, k1, _NT, preferred_element_type=F32), s)
        r4 = row & 3
        lf_prev, lf_next = pltpu.roll(l_f, 1, 0), pltpu.roll(l_f, c - 1, 0)
        lb_prev, lb_next = pltpu.roll(l_b, 1, 0), pltpu.roll(l_b, c - 1, 0)
        d_q = jnp.where(r4 == 0, l_b + lb_next,
                        jnp.where(r4 == 1, l_b, jnp.where(r4 == 2, l_f, lf_prev + l_f)))
        d_k = jnp.where(r4 == 0, lf_next, jnp.where(r4 == 3, lb_prev, 0.0))
        s = jnp.where(level == 2, level_scores(d_q, d_k, jnp.where(r4 >= 2, k_b, k_f)), s)
        for m in HGRN_LEVELS:
            if m % 8 == 0:
                dq_parts, dk_parts, k_parts = [], [], []
                for lo in range(0, c, 2 * m):
                    mid_row, hi_row = lo + m, lo + 2 * m
                    edge_f = b_f[mid_row - 1:mid_row]
                    edge_b = b_b[mid_row:mid_row + 1]
                    dq_parts += [b_b[lo:mid_row] - edge_b, b_f[mid_row:hi_row] - edge_f]
                    dk_parts += [edge_f - b_f[lo:mid_row], edge_b - b_b[mid_row:hi_row]]
                    k_parts += [k_f[lo:mid_row], k_b[mid_row:hi_row]]
                d_q = jnp.concatenate(dq_parts, axis=0)
                d_k = jnp.concatenate(dk_parts, axis=0)
                k_sel = jnp.concatenate(k_parts, axis=0)
            else:
                x_f = b_f - boundary_rows(b_f, m - 1, 2 * m)
                x_b = b_b - boundary_rows(b_b, m, 2 * m)
                second = ((row >> (m.bit_length() - 1)) & 1) == 1
                d_q = jnp.where(second, x_f, x_b)
                d_k = -jnp.where(second, x_b, x_f)
                k_sel = jnp.where(second, k_b, k_f)
            s = jnp.where(level == m.bit_length(), level_scores(d_q, d_k, k_sel), s)

        q_fb = jnp.concatenate([q * jnp.exp2(b_f), q * jnp.exp2(b_b)], axis=1).astype(BF16)
        k_fb = jnp.concatenate([k_f * jnp.exp2(b_f[c - 1:c] - b_f),
                                k_b * jnp.exp2(b_b[0:1] - b_b)], axis=1).astype(BF16)
        dec = jnp.concatenate([jnp.exp2(b_f[c - 1:c]), jnp.exp2(b_b[0:1])], axis=1)

        o_scr[rows, :] = jnp.dot(s.astype(BF16), v, preferred_element_type=F32)
        qfb_scr[rows, :] = q_fb
        dst_scr[ci] = lax.dot_general(v, k_fb, _TN, preferred_element_type=F32)
        dec_scr[ci] = jnp.broadcast_to(dec, (8, 2 * HEAD_D))
        return carry

    lax.fori_loop(0, n_chunks, local_pass, 0, unroll=min(n_chunks, 4))

    if has_s0:
        st_f0 = s0_ref[0, 0, 0, 0].T
        st_b0 = s0_ref[0, 0, 1, 0].T
    else:
        st_f0 = jnp.zeros((HEAD_D, HEAD_D), F32)
        st_b0 = jnp.zeros((HEAD_D, HEAD_D), F32)

    def fwd_scan(ci, st):
        st_scr[ci, :, 0:HEAD_D] = st.astype(BF16)
        return st * dec_scr[ci, 0:1, 0:HEAD_D] + dst_scr[ci, :, 0:HEAD_D]

    def bwd_scan(k, st):
        ci = n_chunks - 1 - k
        st_scr[ci, :, HEAD_D:2 * HEAD_D] = st.astype(BF16)
        return st * dec_scr[ci, 0:1, HEAD_D:2 * HEAD_D] + dst_scr[ci, :, HEAD_D:2 * HEAD_D]

    st_f = lax.fori_loop(0, n_chunks, fwd_scan, st_f0)
    st_b = lax.fori_loop(0, n_chunks, bwd_scan, st_b0)
    if want_final:
        sfin_ref[0, 0, 0] = st_f.T
        sfin_ref[0, 1, 0] = st_b.T

    def output_pass(ci, carry):
        rows = chunk_rows(ci)
        o = o_scr[rows, :] + lax.dot_general(qfb_scr[rows, :], st_scr[ci], _NT,
                                             preferred_element_type=F32)
        y_ref[rows, :] = (_rms(o, gon_ref[...]) * _silu(ga_ref[rows, :].astype(F32))).astype(BF16)
        return carry

    lax.fori_loop(0, n_chunks, output_pass, 0, unroll=min(n_chunks, 4))


def _hgrn_mixer(gates, proj, lb, g_onorm, state, layer_j, seq_len, want_final):
    n_tok = proj.shape[0]
    n_seq = n_tok // seq_len
    n_chunks = seq_len // HGRN_CHUNK
    has_s0 = state is not None
    blk = lambda part: pl.BlockSpec(
        (seq_len, HEAD_D), lambda b, h: (b, part * N_HEADS + h))
    in_specs = [blk(AB_PART["q"]), blk(0), blk(1), blk(AB_PART["i"]), blk(AB_PART["gate_a"]),
                pl.BlockSpec((2, HEAD_D), lambda b, h: (0, h)),
                pl.BlockSpec((1, HEAD_D), lambda b, h: (0, 0)),
                pl.BlockSpec(_HGRN_TRI.shape, lambda b, h: (0, 0)),
                pl.BlockSpec(_HGRN_LEVEL.shape, lambda b, h: (0, 0))]
    args = [proj, gates, gates, proj, proj, lb, g_onorm.reshape(1, HEAD_D),
            jnp.asarray(_HGRN_TRI, BF16), jnp.asarray(_HGRN_LEVEL)]
    if has_s0:
        in_specs.append(pl.BlockSpec((1, 1, 2, 1, HEAD_D, HEAD_D),
                                     lambda b, h: (b, layer_j, 0, h, 0, 0)))
        args.append(state)
    out_specs = [pl.BlockSpec((seq_len, HEAD_D), lambda b, h: (b, h))]
    out_shape = [jax.ShapeDtypeStruct((n_tok, BRANCH_W), BF16)]
    if want_final:
        out_specs.append(pl.BlockSpec((1, 2, 1, HEAD_D, HEAD_D), lambda b, h: (b, 0, h, 0, 0)))
        out_shape.append(jax.ShapeDtypeStruct((n_seq, 2, N_HEADS, HEAD_D, HEAD_D), F32))
    outs = pl.pallas_call(
        functools.partial(_hgrn_kernel, n_chunks=n_chunks, has_s0=has_s0, want_final=want_final),
        grid=(n_seq, N_HEADS),
        in_specs=in_specs,
        out_specs=out_specs,
        out_shape=out_shape,
        scratch_shapes=[
            pltpu.VMEM((seq_len, HEAD_D), F32),
            pltpu.VMEM((seq_len, 2 * HEAD_D), BF16),
            pltpu.VMEM((n_chunks, HEAD_D, 2 * HEAD_D), F32),
            pltpu.VMEM((n_chunks, 8, 2 * HEAD_D), F32),
            pltpu.VMEM((n_chunks, HEAD_D, 2 * HEAD_D), BF16),
        ],
        compiler_params=_cparams(2, 48),
        name="hgrn_mixer",
    )(*args)
    return outs if want_final else (outs[0], None)


def kernel(x_prompt, x_sample, c, state_hgrn, c_ctx, w_ada, b_ada, g_pre, g_post, w_in_ab, w_out_ab,
           lb_logits, g_onorm_a, w_pool, pool_scale, w_in_c, w_out_c, ln_v_g, ln_v_b, w_spatial,
           b_spatial):
    n_batch, seq_p, _ = x_prompt.shape
    dec_batch, seq_s, _ = x_sample.shape
    tm_in, tm_out, tn, tm_c, pool_rows = 1024, 512, 1024, 256, 2048

    cond = jnp.concatenate(
        [c_ctx[None, :], c, jnp.zeros((N_COND_ROWS - 1 - dec_batch, D_MODEL), F32)], axis=0)
    mods = _modulations(cond, w_ada, b_ada).reshape(DEPTH, N_COND_ROWS, 3, D_MODEL)
    lb_all = _lower_bounds(lb_logits)

    flows = [
        dict(x=x_prompt.reshape(n_batch * seq_p, D_MODEL), seq=seq_p, grid=False, state=None,
             row=lambda tm: (lambda i: 0)),
        dict(x=x_sample.reshape(dec_batch * seq_s, D_MODEL), seq=seq_s, grid=True, state=state_hgrn,
             row=lambda tm: (lambda i: 1 + i // (seq_s // tm))),
    ]
    ctx_states = []
    for l in range(DEPTH):
        j = l // 2
        if l % 2 == 0:
            w = w_in_ab[j]
            w_in = jnp.concatenate([w[:, BRANCH_W:3 * BRANCH_W], w[:, :BRANCH_W], w[:, 3 * BRANCH_W:]],
                                   axis=1).astype(BF16)
            w_out = w_out_ab[j].astype(BF16)
            for fi, fl in enumerate(flows):
                gates, proj = _in_proj(fl["x"], mods[l], g_pre[l], w_in, fl["row"](tm_in), tm_in, tn,
                                       n_f32_tiles=AB_F32_PARTS * BRANCH_W // tn)
                y_a, s_fin = _hgrn_mixer(gates, proj, lb_all[j], g_onorm_a[j], fl["state"], j,
                                         fl["seq"], want_final=(fi == 0))
                y_b = _pool_mixer(proj, w_pool[j], pool_scale[j], fl["seq"], fl["grid"], pool_rows)
                fl["x"] = _out_proj([y_a, y_b], [w_out[:BRANCH_W], w_out[BRANCH_W:]], fl["x"],
                                    mods[l], g_post[l], fl["row"](tm_out), tm_out)
                if fi == 0:
                    ctx_states.append(s_fin)
        else:
            w_in, w_out = w_in_c[j].astype(BF16), w_out_c[j].astype(BF16)
            for fl in flows:
                fl["x"] = _layer_c(fl["x"], mods[l], g_pre[l], g_post[l], w_in, w_out, ln_v_g[j],
                                   ln_v_b[j], w_spatial[j], b_spatial[j], fl["row"](tm_c), tm_c)
    y_p = flows[0]["x"].reshape(n_batch, seq_p, D_MODEL)
    y_s = flows[1]["x"].reshape(dec_batch, seq_s, D_MODEL)
    return (y_p, y_s, jnp.stack(ctx_states, axis=1))
```

```python
import functools

import numpy as np
import jax
import jax.numpy as jnp
from jax import lax
from jax.experimental import pallas as pl
from jax.experimental.pallas import tpu as pltpu

F32 = jnp.float32
BF16 = jnp.bfloat16

D_MODEL = 1024
DEPTH = 4
N_AB = (DEPTH + 1) // 2
MIX_WIDTH = 2 * D_MODEL
BRANCH_W = MIX_WIDTH // 2
HEAD_D = 128
N_HEADS = BRANCH_W // HEAD_D
GRID_W = 64
POOL_WINDOWS = (2, 4, 8, 16)
POOL_GROUP_W = BRANCH_W // len(POOL_WINDOWS)
CHUNK_C = 128
N_GROUPS_C = 8
GROUP_W_C = MIX_WIDTH // N_GROUPS_C
AB_IN = 7 * BRANCH_W
C_IN = 3 * MIX_WIDTH
AB_F32_PARTS = 2
AB_PART = {"q": 0, "i": 1, "gate_a": 2, "pool_in": 3, "gate_b": 4}
EPS = 1e-6
N_COND_ROWS = 8

HGRN_CHUNK = 128
HGRN_LEVELS = (64, 32, 16, 8, 4)
MIB = 1024 * 1024


def _cparams(n_axes, vmem_mib):
    return pltpu.CompilerParams(
        dimension_semantics=("arbitrary",) * n_axes, vmem_limit_bytes=int(vmem_mib * MIB))


def _resident(shape):
    zeros = (0,) * len(shape)
    return pl.BlockSpec(shape, lambda *_: zeros, pipeline_mode=pl.Buffered(1))


def _silu(x):
    return x * jax.nn.sigmoid(x)


def _rms(x, g):
    return x * lax.rsqrt(jnp.mean(x * x, axis=-1, keepdims=True) + EPS) * g


def _modulated_norm(x, mod, g):
    return _rms(x, g) * (1.0 + mod[1:2]) + mod[0:1]


def _mod_kernel(cond_ref, w_ref, b_ref, o_ref):
    a = _silu(cond_ref[...])
    o_ref[0] = jnp.dot(a, w_ref[0], preferred_element_type=F32,
                       precision=lax.Precision.HIGHEST) + b_ref[0]


def _modulations(cond, w_ada, b_ada):
    tn = 1024
    n3 = 3 * D_MODEL
    return pl.pallas_call(
        _mod_kernel,
        grid=(DEPTH, n3 // tn),
        in_specs=[
            pl.BlockSpec((N_COND_ROWS, D_MODEL), lambda l, j: (0, 0)),
            pl.BlockSpec((1, D_MODEL, tn), lambda l, j: (l, 0, j)),
            pl.BlockSpec((1, 1, tn), lambda l, j: (l, 0, j)),
        ],
        out_specs=pl.BlockSpec((1, N_COND_ROWS, tn), lambda l, j: (l, 0, j)),
        out_shape=jax.ShapeDtypeStruct((DEPTH, N_COND_ROWS, n3), F32),
        compiler_params=_cparams(2, 24),
        name="adaln_modulation",
    )(cond, w_ada, b_ada.reshape(DEPTH, 1, n3))


def _lb_kernel(x_ref, o_ref):
    x = x_ref[...]
    e = jnp.exp(x - jnp.max(x, axis=0, keepdims=True))
    p = e / jnp.sum(e, axis=0, keepdims=True)
    run = p[0]
    o_ref[0] = run - p[0]
    for l in range(1, x.shape[0]):
        run = run + p[l]
        o_ref[l] = run - p[0]


def _lower_bounds(lb_logits):
    return pl.pallas_call(
        _lb_kernel, out_shape=jax.ShapeDtypeStruct(lb_logits.shape, F32), name="hgrn_lower_bounds",
    )(lb_logits)


def _in_kernel(x_ref, mod_ref, g_ref, w_ref, *rest, n_f32_tiles):
    out_refs, h_ref = rest[:-1], rest[-1]
    j = pl.program_id(1)

    @pl.when(j == 0)
    def _():
        h_ref[...] = _modulated_norm(x_ref[...], mod_ref[0], g_ref[...]).astype(BF16)

    def project(o_ref):
        o_ref[...] = jnp.dot(h_ref[...], w_ref[...], preferred_element_type=F32).astype(o_ref.dtype)

    if n_f32_tiles:
        pl.when(j < n_f32_tiles)(lambda: project(out_refs[0]))
        pl.when(j >= n_f32_tiles)(lambda: project(out_refs[1]))
    else:
        project(out_refs[0])


def _in_proj(x, mod, g, w, row_of_tile, tm, tn, n_f32_tiles=0):
    n_tok, n_out = x.shape[0], w.shape[1]
    n_tiles = n_out // tn
    out_specs, out_shape = [], []
    if n_f32_tiles:
        out_specs.append(pl.BlockSpec((tm, tn), lambda i, j: (i, jnp.minimum(j, n_f32_tiles - 1))))
        out_shape.append(jax.ShapeDtypeStruct((n_tok, n_f32_tiles * tn), F32))
    out_specs.append(pl.BlockSpec((tm, tn), lambda i, j: (i, jnp.maximum(j - n_f32_tiles, 0))))
    out_shape.append(jax.ShapeDtypeStruct((n_tok, (n_tiles - n_f32_tiles) * tn), BF16))
    return pl.pallas_call(
        functools.partial(_in_kernel, n_f32_tiles=n_f32_tiles),
        grid=(n_tok // tm, n_tiles),
        in_specs=[
            pl.BlockSpec((tm, D_MODEL), lambda i, j: (i, 0)),
            pl.BlockSpec((1, 3, D_MODEL), lambda i, j: (row_of_tile(i), 0, 0)),
            pl.BlockSpec((1, D_MODEL), lambda i, j: (0, 0)),
            pl.BlockSpec((D_MODEL, tn), lambda i, j: (0, j)),
        ],
        out_specs=out_specs,
        out_shape=out_shape,
        scratch_shapes=[pltpu.VMEM((tm, D_MODEL), BF16)],
        compiler_params=_cparams(2, 40),
        name="in_projection",
    )(x, mod, g.reshape(1, D_MODEL), w)


def _out_kernel(*refs, n_parts):
    y_refs, w_refs = refs[:n_parts], refs[n_parts:2 * n_parts]
    x_ref, mod_ref, g_ref, o_ref = refs[2 * n_parts:]
    acc = jnp.dot(y_refs[0][...], w_refs[0][...], preferred_element_type=F32)
    for y_ref, w_ref in zip(y_refs[1:], w_refs[1:]):
        acc = acc + jnp.dot(y_ref[...], w_ref[...], preferred_element_type=F32)
    o_ref[...] = x_ref[...] + mod_ref[0][2:3] * _rms(acc, g_ref[...])


def _out_proj(ys, ws, x, mod, g, row_of_tile, tm):
    n_tok = x.shape[0]
    n_parts = len(ys)
    in_specs = [pl.BlockSpec((tm, y.shape[1]), lambda i: (i, 0)) for y in ys]
    in_specs += [pl.BlockSpec(w.shape, lambda i: (0, 0)) for w in ws]
    in_specs += [
        pl.BlockSpec((tm, D_MODEL), lambda i: (i, 0)),
        pl.BlockSpec((1, 3, D_MODEL), lambda i: (row_of_tile(i), 0, 0)),
        pl.BlockSpec((1, D_MODEL), lambda i: (0, 0)),
    ]
    return pl.pallas_call(
        functools.partial(_out_kernel, n_parts=n_parts),
        grid=(n_tok // tm,),
        in_specs=in_specs,
        out_specs=pl.BlockSpec((tm, D_MODEL), lambda i: (i, 0)),
        out_shape=jax.ShapeDtypeStruct((n_tok, D_MODEL), F32),
        compiler_params=_cparams(1, 40),
        name="out_projection",
    )(*ys, *ws, x, mod, g.reshape(1, D_MODEL))


def _layer_c_kernel(x_ref, mod_ref, gpre_ref, win_ref, lng_ref, lnb_ref, ws_ref, bs_ref, wout_ref,
                    gpost_ref, o_ref, y_scr, *, tm):
    x = x_ref[...]
    mod = mod_ref[0]
    h = _modulated_norm(x, mod, gpre_ref[...]).astype(BF16)

    v = jnp.dot(h, win_ref[:, MIX_WIDTH:2 * MIX_WIDTH], preferred_element_type=F32)
    mu = jnp.mean(v, axis=-1, keepdims=True)
    vc = v - mu
    var = jnp.mean(vc * vc, axis=-1, keepdims=True)
    vn = (vc * lax.rsqrt(var + EPS) * lng_ref[...] + lnb_ref[...]).astype(BF16)

    for gi in range(N_GROUPS_C):
        cols = slice(gi * GROUP_W_C, (gi + 1) * GROUP_W_C)
        u = jnp.dot(h, win_ref[:, cols], preferred_element_type=F32)
        gate = _silu(jnp.dot(h, win_ref[:, 2 * MIX_WIDTH + gi * GROUP_W_C:
                                        2 * MIX_WIDTH + (gi + 1) * GROUP_W_C],
                             preferred_element_type=F32))
        for n in range(tm // CHUNK_C):
            rows = slice(n * CHUNK_C, (n + 1) * CHUNK_C)
            sp = jnp.dot(ws_ref[gi], vn[rows, cols], preferred_element_type=F32) + bs_ref[:, gi:gi + 1]
            y_scr[rows, cols] = (u[rows] * sp * gate[rows]).astype(BF16)

    acc = jnp.dot(y_scr[...], wout_ref[...], preferred_element_type=F32)
    o_ref[...] = x + mod[2:3] * _rms(acc, gpost_ref[...])


def _layer_c(x, mod, g_pre, g_post, w_in, w_out, ln_g, ln_b, w_s, b_s, row_of_tile, tm):
    n_tok = x.shape[0]
    return pl.pallas_call(
        functools.partial(_layer_c_kernel, tm=tm),
        grid=(n_tok // tm,),
        in_specs=[
            pl.BlockSpec((tm, D_MODEL), lambda i: (i, 0)),
            pl.BlockSpec((1, 3, D_MODEL), lambda i: (row_of_tile(i), 0, 0)),
            _resident((1, D_MODEL)),
            _resident((D_MODEL, C_IN)),
            _resident((1, MIX_WIDTH)),
            _resident((1, MIX_WIDTH)),
            _resident((N_GROUPS_C, CHUNK_C, CHUNK_C)),
            _resident((CHUNK_C, N_GROUPS_C)),
            _resident((MIX_WIDTH, D_MODEL)),
            _resident((1, D_MODEL)),
        ],
        out_specs=pl.BlockSpec((tm, D_MODEL), lambda i: (i, 0)),
        out_shape=jax.ShapeDtypeStruct((n_tok, D_MODEL), F32),
        scratch_shapes=[pltpu.VMEM((tm, MIX_WIDTH), BF16)],
        compiler_params=_cparams(1, 48),
        name="gmlp_layer",
    )(x, mod, g_pre.reshape(1, D_MODEL), w_in, ln_g.reshape(1, MIX_WIDTH), ln_b.reshape(1, MIX_WIDTH),
      w_s.astype(BF16), b_s.T, w_out, g_post.reshape(1, D_MODEL))


def _window_counts(n, w):
    return [min(i - w // 2 + w, n) - max(i - w // 2, 0) for i in range(n)]


def _mean_minor(x3, w):
    g, n, ch = x3.shape
    idx = lax.broadcasted_iota(jnp.int32, (1, n, ch), 1)

    def shifted(a, k):
        return pltpu.roll(a.reshape(g * n, ch), k % (g * n), 0).reshape(g, n, ch)

    trail, lead = x3, x3
    size = 1
    while size < w // 2:
        trail = trail + jnp.where(idx >= size, shifted(trail, size), 0.0)
        lead = lead + jnp.where(idx + size < n, shifted(lead, -size), 0.0)
        size *= 2
    s = jnp.where(idx >= 1, shifted(trail, 1), 0.0) + lead
    half = w // 2
    cnt = jnp.minimum(idx - half + w, n) - jnp.maximum(idx - half, 0)
    return s * (1.0 / cnt.astype(F32))


def _mean_major(x3, w):
    n = x3.shape[0]
    zeros = lambda k: jnp.zeros((k,) + x3.shape[1:], F32)
    back = lambda a, k: jnp.concatenate([zeros(k), a[:n - k]], axis=0)
    ahead = lambda a, k: jnp.concatenate([a[k:], zeros(k)], axis=0)
    trail, lead = x3, x3
    size = 1
    while size < w // 2:
        trail = trail + back(trail, size)
        lead = lead + ahead(lead, size)
        size *= 2
    s = back(trail, 1) + lead
    return jnp.concatenate(
        [s[i:i + 1] * (1.0 / cnt) for i, cnt in enumerate(_window_counts(n, w))], axis=0)


def _pool_kernel(p_ref, gate_ref, wp_ref, ps_ref, y_ref, *, seq_len, grid_mode):
    grp = pl.program_id(1)
    rows, ch = p_ref.shape
    for k, w in enumerate(POOL_WINDOWS):
        @pl.when(grp == k)
        def _(w=w):
            x = p_ref[...].astype(F32)
            if grid_mode:
                m = _mean_major(x.reshape(rows // GRID_W, GRID_W, ch), w)
                m = _mean_minor(m, w).reshape(rows, ch)
            else:
                m = _mean_minor(x.reshape(rows // seq_len, seq_len, ch), w).reshape(rows, ch)
            dlt = (m - x).astype(BF16)
            y = jnp.dot(dlt, wp_ref[0], preferred_element_type=F32) * ps_ref[...]
            y_ref[...] = (y * _silu(gate_ref[...].astype(F32))).astype(BF16)


def _pool_mixer(proj, w_pool, pool_scale, seq_len, grid_mode, rows_per_step):
    n_tok = proj.shape[0]
    n_grp = len(POOL_WINDOWS)
    p_blk0 = AB_PART["pool_in"] * BRANCH_W // POOL_GROUP_W
    g_blk0 = AB_PART["gate_b"] * BRANCH_W // POOL_GROUP_W
    return pl.pallas_call(
        functools.partial(_pool_kernel, seq_len=seq_len, grid_mode=grid_mode),
        grid=(n_tok // rows_per_step, n_grp),
        in_specs=[
            pl.BlockSpec((rows_per_step, POOL_GROUP_W), lambda b, g: (b, p_blk0 + g)),
            pl.BlockSpec((rows_per_step, POOL_GROUP_W), lambda b, g: (b, g_blk0 + g)),
            pl.BlockSpec((1, POOL_GROUP_W, POOL_GROUP_W), lambda b, g: (g, 0, 0)),
            pl.BlockSpec((1, POOL_GROUP_W), lambda b, g: (0, g)),
        ],
        out_specs=pl.BlockSpec((rows_per_step, POOL_GROUP_W), lambda b, g: (b, g)),
        out_shape=jax.ShapeDtypeStruct((n_tok, BRANCH_W), BF16),
        compiler_params=_cparams(2, 56),
        name="pool_mixer",
    )(proj, proj, w_pool.astype(BF16), pool_scale.reshape(1, BRANCH_W))


def _hgrn_triangles():
    ones = np.ones((HGRN_CHUNK, HGRN_CHUNK), np.float32)
    return np.concatenate([np.tril(ones), np.triu(ones)], axis=0)


def _hgrn_level_index():
    i = np.arange(HGRN_CHUNK)
    x = i[:, None] ^ i[None, :]
    lv = np.zeros_like(x)
    nz = x > 0
    lv[nz] = np.floor(np.log2(x[nz])).astype(x.dtype) + 1
    return lv.astype(np.int32)


_HGRN_TRI = _hgrn_triangles()
_HGRN_LEVEL = _hgrn_level_index()

_NT = (((1,), (1,)), ((), ()))
_TN = (((0,), (0,)), ((), ()))


def _hgrn_kernel(*refs, n_chunks, has_s0, has_prev, want_final):
    q_ref, ff_ref, fb_ref, v_ref, ga_ref, lb_ref, gon_ref, tri_ref, lvl_ref = refs[:9]
    pos = 9
    s0_ref = sfin_ref = None
    if has_s0:
        s0_ref = refs[pos]
        pos += 1
    if has_prev:
        pos += 1
    y_ref = refs[pos]
    pos += 1
    if want_final:
        sfin_ref = refs[pos]
        pos += 1
    o_scr, qfb_scr, dst_scr, dec_scr, st_scr = refs[pos:]

    c = HGRN_CHUNK
    lbv = lb_ref[...]
    lb_f, lb_b = lbv[0:1], lbv[1:2]
    level = lvl_ref[...]
    row = lax.broadcasted_iota(jnp.int32, (c, HEAD_D), 0)

    def chunk_rows(ci):
        return pl.ds(pl.multiple_of(ci * c, c), c)

    def cumulative(tri, x):
        hi = x.astype(BF16)
        mid = (x - hi.astype(F32)).astype(BF16)
        dd = jnp.dot(tri, jnp.concatenate([hi, mid], axis=1), preferred_element_type=F32)
        return dd[:, :HEAD_D] + dd[:, HEAD_D:]

    def boundary_rows(b, first, block):
        return jnp.concatenate(
            [jnp.broadcast_to(b[r:r + 1, :], (block, HEAD_D)) for r in range(first, c, block)], axis=0)

    def local_pass(ci, carry):
        rows = chunk_rows(ci)
        f_f = lb_f + (1.0 - lb_f) * jax.nn.sigmoid(ff_ref[rows, :])
        f_b = lb_b + (1.0 - lb_b) * jax.nn.sigmoid(fb_ref[rows, :])
        k_f, k_b = 1.0 - f_f, 1.0 - f_b
        l_f, l_b = jnp.log2(f_f), jnp.log2(f_b)
        b_f = cumulative(tri_ref[0:c, :], l_f)
        b_b = cumulative(tri_ref[c:2 * c, :], l_b)

        q = _silu(q_ref[rows, :].astype(F32))
        v = v_ref[rows, :]

        def level_scores(d_q, d_k, k_sel):
            qt = (q * jnp.exp2(d_q)).astype(BF16)
            kt = (k_sel * jnp.exp2(d_k)).astype(BF16)
            return lax.dot_general(qt, kt, _NT, preferred_element_type=F32)

        s = lax.dot_general(q.astype(BF16), (k_f + k_b).astype(BF16), _NT,
                            preferred_element_type=F32)
        s = jnp.where(level == 0, s, 0.0)
        odd = (row & 1) == 1
        q1 = (q * jnp.where(odd, f_f, f_b)).astype(BF16)
        k1 = jnp.where(odd, k_b, k_f).astype(BF16)
        s = jnp.where(level == 1, lax.dot_general(q1, k1, _NT, preferred_element_type=F32), s)
        r4 = row & 3
        lf_prev, lf_next = pltpu.roll(l_f, 1, 0), pltpu.roll(l_f, c - 1, 0)
        lb_prev, lb_next = pltpu.roll(l_b, 1, 0), pltpu.roll(l_b, c - 1, 0)
        d_q = jnp.where(r4 == 0, l_b + lb_next,
                        jnp.where(r4 == 1, l_b, jnp.where(r4 == 2, l_f, lf_prev + l_f)))
        d_k = jnp.where(r4 == 0, lf_next, jnp.where(r4 == 3, lb_prev, 0.0))
        s = jnp.where(level == 2, level_scores(d_q, d_k, jnp.where(r4 >= 2, k_b, k_f)), s)
        for m in HGRN_LEVELS:
            if m % 8 == 0:
                dq_parts, dk_parts, k_parts = [], [], []
                for lo in range(0, c, 2 * m):
                    mid_row, hi_row = lo + m, lo + 2 * m
                    edge_f = b_f[mid_row - 1:mid_row]
                    edge_b = b_b[mid_row:mid_row + 1]
                    dq_parts += [b_b[lo:mid_row] - edge_b, b_f[mid_row:hi_row] - edge_f]
                    dk_parts += [edge_f - b_f[lo:mid_row], edge_b - b_b[mid_row:hi_row]]
                    k_parts += [k_f[lo:mid_row], k_b[mid_row:hi_row]]
                d_q = jnp.concatenate(dq_parts, axis=0)
                d_k = jnp.concatenate(dk_parts, axis=0)
                k_sel = jnp.concatenate(k_parts, axis=0)
            else:
                x_f = b_f - boundary_rows(b_f, m - 1, 2 * m)
                x_b = b_b - boundary_rows(b_b, m, 2 * m)
                second = ((row >> (m.bit_length() - 1)) & 1) == 1
                d_q = jnp.where(second, x_f, x_b)
                d_k = -jnp.where(second, x_b, x_f)
                k_sel = jnp.where(second, k_b, k_f)
            s = jnp.where(level == m.bit_length(), level_scores(d_q, d_k, k_sel), s)

        q_fb = jnp.concatenate([q * jnp.exp2(b_f), q * jnp.exp2(b_b)], axis=1).astype(BF16)
        k_fb = jnp.concatenate([k_f * jnp.exp2(b_f[c - 1:c] - b_f),
                                k_b * jnp.exp2(b_b[0:1] - b_b)], axis=1).astype(BF16)
        dec = jnp.concatenate([jnp.exp2(b_f[c - 1:c]), jnp.exp2(b_b[0:1])], axis=1)

        o_scr[rows, :] = jnp.dot(s.astype(BF16), v, preferred_element_type=F32)
        qfb_scr[rows, :] = q_fb
        dst_scr[ci] = lax.dot_general(v, k_fb, _TN, preferred_element_type=F32)
        dec_scr[ci] = jnp.broadcast_to(dec, (8, 2 * HEAD_D))
        return carry

    lax.fori_loop(0, n_chunks, local_pass, 0, unroll=min(n_chunks, 4))

    if has_s0:
        st_f0 = s0_ref[0, 0, 0, 0].T
        st_b0 = s0_ref[0, 0, 1, 0].T
    else:
        st_f0 = jnp.zeros((HEAD_D, HEAD_D), F32)
        st_b0 = jnp.zeros((HEAD_D, HEAD_D), F32)

    def fwd_scan(ci, st):
        st_scr[ci, :, 0:HEAD_D] = st.astype(BF16)
        return st * dec_scr[ci, 0:1, 0:HEAD_D] + dst_scr[ci, :, 0:HEAD_D]

    def bwd_scan(k, st):
        ci = n_chunks - 1 - k
        st_scr[ci, :, HEAD_D:2 * HEAD_D] = st.astype(BF16)
        return st * dec_scr[ci, 0:1, HEAD_D:2 * HEAD_D] + dst_scr[ci, :, HEAD_D:2 * HEAD_D]

    st_f = lax.fori_loop(0, n_chunks, fwd_scan, st_f0)
    st_b = lax.fori_loop(0, n_chunks, bwd_scan, st_b0)
    if want_final:
        sfin_ref[0, 0, 0, 0] = st_f.T
        sfin_ref[0, 0, 1, 0] = st_b.T

    def output_pass(ci, carry):
        rows = chunk_rows(ci)
        o = o_scr[rows, :] + lax.dot_general(qfb_scr[rows, :], st_scr[ci], _NT,
                                             preferred_element_type=F32)
        y_ref[rows, :] = (_rms(o, gon_ref[...]) * _silu(ga_ref[rows, :].astype(F32))).astype(BF16)
        return carry

    lax.fori_loop(0, n_chunks, output_pass, 0, unroll=min(n_chunks, 4))


def _hgrn_mixer(gates, proj, lb, g_onorm, state, layer_j, seq_len, want_final, final_prev=None):
    n_tok = proj.shape[0]
    n_seq = n_tok // seq_len
    n_chunks = seq_len // HGRN_CHUNK
    has_s0 = state is not None
    has_prev = final_prev is not None
    blk = lambda part: pl.BlockSpec(
        (seq_len, HEAD_D), lambda b, h: (b, part * N_HEADS + h))
    in_specs = [blk(AB_PART["q"]), blk(0), blk(1), blk(AB_PART["i"]), blk(AB_PART["gate_a"]),
                pl.BlockSpec((2, HEAD_D), lambda b, h: (0, h)),
                pl.BlockSpec((1, HEAD_D), lambda b, h: (0, 0)),
                pl.BlockSpec(_HGRN_TRI.shape, lambda b, h: (0, 0)),
                pl.BlockSpec(_HGRN_LEVEL.shape, lambda b, h: (0, 0))]
    args = [proj, gates, gates, proj, proj, lb, g_onorm.reshape(1, HEAD_D),
            jnp.asarray(_HGRN_TRI, BF16), jnp.asarray(_HGRN_LEVEL)]
    if has_s0:
        in_specs.append(pl.BlockSpec((1, 1, 2, 1, HEAD_D, HEAD_D),
                                     lambda b, h: (b, layer_j, 0, h, 0, 0)))
        args.append(state)
    aliases = {}
    if has_prev:
        aliases[len(args)] = 1
        in_specs.append(pl.BlockSpec(memory_space=pl.ANY))
        args.append(final_prev)
    out_specs = [pl.BlockSpec((seq_len, HEAD_D), lambda b, h: (b, h))]
    out_shape = [jax.ShapeDtypeStruct((n_tok, BRANCH_W), BF16)]
    if want_final:
        out_specs.append(pl.BlockSpec((1, 1, 2, 1, HEAD_D, HEAD_D),
                                      lambda b, h: (b, layer_j, 0, h, 0, 0)))
        out_shape.append(jax.ShapeDtypeStruct((n_seq, N_AB, 2, N_HEADS, HEAD_D, HEAD_D), F32))
    outs = pl.pallas_call(
        functools.partial(_hgrn_kernel, n_chunks=n_chunks, has_s0=has_s0, has_prev=has_prev,
                          want_final=want_final),
        grid=(n_seq, N_HEADS),
        in_specs=in_specs,
        out_specs=out_specs,
        out_shape=out_shape,
        input_output_aliases=aliases,
        scratch_shapes=[
            pltpu.VMEM((seq_len, HEAD_D), F32),
            pltpu.VMEM((seq_len, 2 * HEAD_D), BF16),
            pltpu.VMEM((n_chunks, HEAD_D, 2 * HEAD_D), F32),
            pltpu.VMEM((n_chunks, 8, 2 * HEAD_D), F32),
            pltpu.VMEM((n_chunks, HEAD_D, 2 * HEAD_D), BF16),
        ],
        compiler_params=_cparams(2, 48),
        name="hgrn_mixer",
    )(*args)
    return outs if want_final else (outs[0], None)


def kernel(x_prompt, x_sample, c, state_hgrn, c_ctx, w_ada, b_ada, g_pre, g_post, w_in_ab, w_out_ab,
           lb_logits, g_onorm_a, w_pool, pool_scale, w_in_c, w_out_c, ln_v_g, ln_v_b, w_spatial,
           b_spatial):
    n_batch, seq_p, _ = x_prompt.shape
    dec_batch, seq_s, _ = x_sample.shape
    tm_in, tm_out, tn, tm_c, pool_rows = 1024, 512, 1024, 512, 2048

    cond = jnp.concatenate(
        [c_ctx[None, :], c, jnp.zeros((N_COND_ROWS - 1 - dec_batch, D_MODEL), F32)], axis=0)
    mods = _modulations(cond, w_ada, b_ada).reshape(DEPTH, N_COND_ROWS, 3, D_MODEL)
    lb_all = _lower_bounds(lb_logits)

    flows = [
        dict(x=x_prompt.reshape(n_batch * seq_p, D_MODEL), seq=seq_p, grid=False, state=None,
             row=lambda tm: (lambda i: 0)),
        dict(x=x_sample.reshape(dec_batch * seq_s, D_MODEL), seq=seq_s, grid=True, state=state_hgrn,
             row=lambda tm: (lambda i: 1 + i // (seq_s // tm))),
    ]
    new_state = None
    for l in range(DEPTH):
        j = l // 2
        if l % 2 == 0:
            w = w_in_ab[j]
            w_in = jnp.concatenate([w[:, BRANCH_W:3 * BRANCH_W], w[:, :BRANCH_W], w[:, 3 * BRANCH_W:]],
                                   axis=1).astype(BF16)
            w_out = w_out_ab[j].astype(BF16)
            for fi, fl in enumerate(flows):
                gates, proj = _in_proj(fl["x"], mods[l], g_pre[l], w_in, fl["row"](tm_in), tm_in, tn,
                                       n_f32_tiles=AB_F32_PARTS * BRANCH_W // tn)
                if fi == 0:
                    y_a, new_state = _hgrn_mixer(gates, proj, lb_all[j], g_onorm_a[j], None, j,
                                                 fl["seq"], want_final=True, final_prev=new_state)
                else:
                    y_a, _ = _hgrn_mixer(gates, proj, lb_all[j], g_onorm_a[j], fl["state"], j,
                                         fl["seq"], want_final=False)
                y_b = _pool_mixer(proj, w_pool[j], pool_scale[j], fl["seq"], fl["grid"], pool_rows)
                fl["x"] = _out_proj([y_a, y_b], [w_out[:BRANCH_W], w_out[BRANCH_W:]], fl["x"],
                                    mods[l], g_post[l], fl["row"](tm_out), tm_out)
        else:
            w_in, w_out = w_in_c[j].astype(BF16), w_out_c[j].astype(BF16)
            for fl in flows:
                fl["x"] = _layer_c(fl["x"], mods[l], g_pre[l], g_post[l], w_in, w_out, ln_v_g[j],
                                   ln_v_b[j], w_spatial[j], b_spatial[j], fl["row"](tm_c), tm_c)
    y_p = flows[0]["x"].reshape(n_batch, seq_p, D_MODEL)
    y_s = flows[1]["x"].reshape(dec_batch, seq_s, D_MODEL)
    return (y_p, y_s, new_state)
```

```python
import functools

import numpy as np
import jax
import jax.numpy as jnp
from jax import lax
from jax.experimental import pallas as pl
from jax.experimental.pallas import tpu as pltpu

F32 = jnp.float32
BF16 = jnp.bfloat16

D_MODEL = 1024
DEPTH = 4
N_AB = (DEPTH + 1) // 2
MIX_WIDTH = 2 * D_MODEL
BRANCH_W = MIX_WIDTH // 2
HEAD_D = 128
N_HEADS = BRANCH_W // HEAD_D
GRID_W = 64
POOL_WINDOWS = (2, 4, 8, 16)
POOL_GROUP_W = BRANCH_W // len(POOL_WINDOWS)
CHUNK_C = 128
N_GROUPS_C = 8
GROUP_W_C = MIX_WIDTH // N_GROUPS_C
AB_IN = 7 * BRANCH_W
C_IN = 3 * MIX_WIDTH
AB_F32_PARTS = 2
AB_PART = {"q": 0, "i": 1, "gate_a": 2, "pool_in": 3, "gate_b": 4}
EPS = 1e-6
N_COND_ROWS = 8

HGRN_CHUNK = 128
HGRN_LEVELS = (64, 32, 16, 8)
HGRN_PSUM_LEVELS = (2, 4)
MIB = 1024 * 1024


def _cparams(n_axes, vmem_mib):
    return pltpu.CompilerParams(
        dimension_semantics=("arbitrary",) * n_axes, vmem_limit_bytes=int(vmem_mib * MIB))


def _resident(shape):
    zeros = (0,) * len(shape)
    return pl.BlockSpec(shape, lambda *_: zeros, pipeline_mode=pl.Buffered(1))


def _silu(x):
    return x * jax.nn.sigmoid(x)


def _rms(x, g):
    return x * lax.rsqrt(jnp.mean(x * x, axis=-1, keepdims=True) + EPS) * g


def _modulated_norm(x, mod, g):
    return _rms(x, g) * (1.0 + mod[1:2]) + mod[0:1]


def _mod_kernel(cond_ref, w_ref, b_ref, o_ref):
    a = _silu(cond_ref[...])
    o_ref[0] = jnp.dot(a, w_ref[0], preferred_element_type=F32,
                       precision=lax.Precision.HIGHEST) + b_ref[0]


def _modulations(cond, w_ada, b_ada):
    tn = 1024
    n3 = 3 * D_MODEL
    return pl.pallas_call(
        _mod_kernel,
        grid=(DEPTH, n3 // tn),
        in_specs=[
            pl.BlockSpec((N_COND_ROWS, D_MODEL), lambda l, j: (0, 0)),
            pl.BlockSpec((1, D_MODEL, tn), lambda l, j: (l, 0, j)),
            pl.BlockSpec((1, 1, tn), lambda l, j: (l, 0, j)),
        ],
        out_specs=pl.BlockSpec((1, N_COND_ROWS, tn), lambda l, j: (l, 0, j)),
        out_shape=jax.ShapeDtypeStruct((DEPTH, N_COND_ROWS, n3), F32),
        compiler_params=_cparams(2, 24),
        name="adaln_modulation",
    )(cond, w_ada, b_ada.reshape(DEPTH, 1, n3))


def _lb_kernel(x_ref, o_ref):
    x = x_ref[...]
    e = jnp.exp(x - jnp.max(x, axis=0, keepdims=True))
    p = e / jnp.sum(e, axis=0, keepdims=True)
    run = p[0]
    o_ref[0] = run - p[0]
    for l in range(1, x.shape[0]):
        run = run + p[l]
        o_ref[l] = run - p[0]


def _lower_bounds(lb_logits):
    return pl.pallas_call(
        _lb_kernel, out_shape=jax.ShapeDtypeStruct(lb_logits.shape, F32), name="hgrn_lower_bounds",
    )(lb_logits)


def _in_kernel(x_ref, mod_ref, g_ref, w_ref, *rest, n_f32_tiles):
    out_refs, h_ref = rest[:-1], rest[-1]
    j = pl.program_id(1)

    @pl.when(j == 0)
    def _():
        h_ref[...] = _modulated_norm(x_ref[...], mod_ref[0], g_ref[...]).astype(BF16)

    def project(o_ref):
        o_ref[...] = jnp.dot(h_ref[...], w_ref[...], preferred_element_type=F32).astype(o_ref.dtype)

    if n_f32_tiles:
        pl.when(j < n_f32_tiles)(lambda: project(out_refs[0]))
        pl.when(j >= n_f32_tiles)(lambda: project(out_refs[1]))
    else:
        project(out_refs[0])


def _in_proj(x, mod, g, w, row_of_tile, tm, tn, n_f32_tiles=0):
    n_tok, n_out = x.shape[0], w.shape[1]
    n_tiles = n_out // tn
    out_specs, out_shape = [], []
    if n_f32_tiles:
        out_specs.append(pl.BlockSpec((tm, tn), lambda i, j: (i, jnp.minimum(j, n_f32_tiles - 1))))
        out_shape.append(jax.ShapeDtypeStruct((n_tok, n_f32_tiles * tn), F32))
    out_specs.append(pl.BlockSpec((tm, tn), lambda i, j: (i, jnp.maximum(j - n_f32_tiles, 0))))
    out_shape.append(jax.ShapeDtypeStruct((n_tok, (n_tiles - n_f32_tiles) * tn), BF16))
    return pl.pallas_call(
        functools.partial(_in_kernel, n_f32_tiles=n_f32_tiles),
        grid=(n_tok // tm, n_tiles),
        in_specs=[
            pl.BlockSpec((tm, D_MODEL), lambda i, j: (i, 0)),
            pl.BlockSpec((1, 3, D_MODEL), lambda i, j: (row_of_tile(i), 0, 0)),
            pl.BlockSpec((1, D_MODEL), lambda i, j: (0, 0)),
            pl.BlockSpec((D_MODEL, tn), lambda i, j: (0, j)),
        ],
        out_specs=out_specs,
        out_shape=out_shape,
        scratch_shapes=[pltpu.VMEM((tm, D_MODEL), BF16)],
        compiler_params=_cparams(2, 40),
        name="in_projection",
    )(x, mod, g.reshape(1, D_MODEL), w)


def _out_kernel(*refs, n_parts):
    y_refs, w_refs = refs[:n_parts], refs[n_parts:2 * n_parts]
    x_ref, mod_ref, g_ref, o_ref = refs[2 * n_parts:]
    acc = jnp.dot(y_refs[0][...], w_refs[0][...], preferred_element_type=F32)
    for y_ref, w_ref in zip(y_refs[1:], w_refs[1:]):
        acc = acc + jnp.dot(y_ref[...], w_ref[...], preferred_element_type=F32)
    o_ref[...] = x_ref[...] + mod_ref[0][2:3] * _rms(acc, g_ref[...])


def _out_proj(ys, ws, x, mod, g, row_of_tile, tm):
    n_tok = x.shape[0]
    n_parts = len(ys)
    in_specs = [pl.BlockSpec((tm, y.shape[1]), lambda i: (i, 0)) for y in ys]
    in_specs += [pl.BlockSpec(w.shape, lambda i: (0, 0)) for w in ws]
    in_specs += [
        pl.BlockSpec((tm, D_MODEL), lambda i: (i, 0)),
        pl.BlockSpec((1, 3, D_MODEL), lambda i: (row_of_tile(i), 0, 0)),
        pl.BlockSpec((1, D_MODEL), lambda i: (0, 0)),
    ]
    return pl.pallas_call(
        functools.partial(_out_kernel, n_parts=n_parts),
        grid=(n_tok // tm,),
        in_specs=in_specs,
        out_specs=pl.BlockSpec((tm, D_MODEL), lambda i: (i, 0)),
        out_shape=jax.ShapeDtypeStruct((n_tok, D_MODEL), F32),
        compiler_params=_cparams(1, 40),
        name="out_projection",
    )(*ys, *ws, x, mod, g.reshape(1, D_MODEL))


def _layer_c_kernel(x_ref, mod_ref, gpre_ref, win_ref, lng_ref, lnb_ref, ws_ref, bs_ref, wout_ref,
                    gpost_ref, o_ref, y_scr, *, tm):
    x = x_ref[...]
    mod = mod_ref[0]
    h = _modulated_norm(x, mod, gpre_ref[...]).astype(BF16)

    v = jnp.dot(h, win_ref[:, MIX_WIDTH:2 * MIX_WIDTH], preferred_element_type=F32)
    mu = jnp.mean(v, axis=-1, keepdims=True)
    vc = v - mu
    var = jnp.mean(vc * vc, axis=-1, keepdims=True)
    vn = (vc * lax.rsqrt(var + EPS) * lng_ref[...] + lnb_ref[...]).astype(BF16)

    for gi in range(N_GROUPS_C):
        cols = slice(gi * GROUP_W_C, (gi + 1) * GROUP_W_C)
        u = jnp.dot(h, win_ref[:, cols], preferred_element_type=F32)
        gate = _silu(jnp.dot(h, win_ref[:, 2 * MIX_WIDTH + gi * GROUP_W_C:
                                        2 * MIX_WIDTH + (gi + 1) * GROUP_W_C],
                             preferred_element_type=F32))
        for n in range(tm // CHUNK_C):
            rows = slice(n * CHUNK_C, (n + 1) * CHUNK_C)
            sp = jnp.dot(ws_ref[gi], vn[rows, cols], preferred_element_type=F32) + bs_ref[:, gi:gi + 1]
            y_scr[rows, cols] = (u[rows] * sp * gate[rows]).astype(BF16)

    acc = jnp.dot(y_scr[...], wout_ref[...], preferred_element_type=F32)
    o_ref[...] = x + mod[2:3] * _rms(acc, gpost_ref[...])


def _layer_c(x, mod, g_pre, g_post, w_in, w_out, ln_g, ln_b, w_s, b_s, row_of_tile, tm):
    n_tok = x.shape[0]
    return pl.pallas_call(
        functools.partial(_layer_c_kernel, tm=tm),
        grid=(n_tok // tm,),
        in_specs=[
            pl.BlockSpec((tm, D_MODEL), lambda i: (i, 0)),
            pl.BlockSpec((1, 3, D_MODEL), lambda i: (row_of_tile(i), 0, 0)),
            _resident((1, D_MODEL)),
            _resident((D_MODEL, C_IN)),
            _resident((1, MIX_WIDTH)),
            _resident((1, MIX_WIDTH)),
            _resident((N_GROUPS_C, CHUNK_C, CHUNK_C)),
            _resident((CHUNK_C, N_GROUPS_C)),
            _resident((MIX_WIDTH, D_MODEL)),
            _resident((1, D_MODEL)),
        ],
        out_specs=pl.BlockSpec((tm, D_MODEL), lambda i: (i, 0)),
        out_shape=jax.ShapeDtypeStruct((n_tok, D_MODEL), F32),
        scratch_shapes=[pltpu.VMEM((tm, MIX_WIDTH), BF16)],
        compiler_params=_cparams(1, 48),
        name="gmlp_layer",
    )(x, mod, g_pre.reshape(1, D_MODEL), w_in, ln_g.reshape(1, MIX_WIDTH), ln_b.reshape(1, MIX_WIDTH),
      w_s.astype(BF16), b_s.T, w_out, g_post.reshape(1, D_MODEL))


def _window_counts(n, w):
    return [min(i - w // 2 + w, n) - max(i - w // 2, 0) for i in range(n)]


def _mean_minor(x3, w):
    g, n, ch = x3.shape
    idx = lax.broadcasted_iota(jnp.int32, (1, n, ch), 1)

    def shifted(a, k):
        return pltpu.roll(a.reshape(g * n, ch), k % (g * n), 0).reshape(g, n, ch)

    trail, lead = x3, x3
    size = 1
    while size < w // 2:
        trail = trail + jnp.where(idx >= size, shifted(trail, size), 0.0)
        lead = lead + jnp.where(idx + size < n, shifted(lead, -size), 0.0)
        size *= 2
    s = jnp.where(idx >= 1, shifted(trail, 1), 0.0) + lead
    half = w // 2
    cnt = jnp.minimum(idx - half + w, n) - jnp.maximum(idx - half, 0)
    return s * (1.0 / cnt.astype(F32))


def _mean_major(x3, w):
    n = x3.shape[0]
    zeros = lambda k: jnp.zeros((k,) + x3.shape[1:], F32)
    back = lambda a, k: jnp.concatenate([zeros(k), a[:n - k]], axis=0)
    ahead = lambda a, k: jnp.concatenate([a[k:], zeros(k)], axis=0)
    trail, lead = x3, x3
    size = 1
    while size < w // 2:
        trail = trail + back(trail, size)
        lead = lead + ahead(lead, size)
        size *= 2
    s = back(trail, 1) + lead
    return jnp.concatenate(
        [s[i:i + 1] * (1.0 / cnt) for i, cnt in enumerate(_window_counts(n, w))], axis=0)


def _pool_kernel(p_ref, gate_ref, wp_ref, ps_ref, y_ref, *, seq_len, grid_mode):
    grp = pl.program_id(1)
    rows, ch = p_ref.shape
    for k, w in enumerate(POOL_WINDOWS):
        @pl.when(grp == k)
        def _(w=w):
            x = p_ref[...].astype(F32)
            if grid_mode:
                m = _mean_major(x.reshape(rows // GRID_W, GRID_W, ch), w)
                m = _mean_minor(m, w).reshape(rows, ch)
            else:
                m = _mean_minor(x.reshape(rows // seq_len, seq_len, ch), w).reshape(rows, ch)
            dlt = (m - x).astype(BF16)
            y = jnp.dot(dlt, wp_ref[0], preferred_element_type=F32) * ps_ref[...]
            y_ref[...] = (y * _silu(gate_ref[...].astype(F32))).astype(BF16)


def _pool_mixer(proj, w_pool, pool_scale, seq_len, grid_mode, rows_per_step):
    n_tok = proj.shape[0]
    n_grp = len(POOL_WINDOWS)
    p_blk0 = AB_PART["pool_in"] * BRANCH_W // POOL_GROUP_W
    g_blk0 = AB_PART["gate_b"] * BRANCH_W // POOL_GROUP_W
    return pl.pallas_call(
        functools.partial(_pool_kernel, seq_len=seq_len, grid_mode=grid_mode),
        grid=(n_tok // rows_per_step, n_grp),
        in_specs=[
            pl.BlockSpec((rows_per_step, POOL_GROUP_W), lambda b, g: (b, p_blk0 + g)),
            pl.BlockSpec((rows_per_step, POOL_GROUP_W), lambda b, g: (b, g_blk0 + g)),
            pl.BlockSpec((1, POOL_GROUP_W, POOL_GROUP_W), lambda b, g: (g, 0, 0)),
            pl.BlockSpec((1, POOL_GROUP_W), lambda b, g: (0, g)),
        ],
        out_specs=pl.BlockSpec((rows_per_step, POOL_GROUP_W), lambda b, g: (b, g)),
        out_shape=jax.ShapeDtypeStruct((n_tok, BRANCH_W), BF16),
        compiler_params=_cparams(2, 56),
        name="pool_mixer",
    )(proj, proj, w_pool.astype(BF16), pool_scale.reshape(1, BRANCH_W))


def _hgrn_triangles():
    ones = np.ones((HGRN_CHUNK, HGRN_CHUNK), np.float32)
    return np.concatenate([np.tril(ones), np.triu(ones)], axis=0)


def _hgrn_partial_sum_matrix():
    c = HGRN_CHUNK
    blocks = []
    for m in HGRN_PSUM_LEVELS:
        mq = np.zeros((c, 2 * c), np.float32)
        mk = np.zeros((c, 2 * c), np.float32)
        for i in range(c):
            start = (i // (2 * m)) * (2 * m)
            r1, r2 = start + m - 1, start + m
            if i >= r2:
                mq[i, r2:i + 1] = 1
                mk[i, c + r2:c + i] = 1
            else:
                mq[i, c + i:c + r1 + 1] = 1
                mk[i, i + 1:r1 + 1] = 1
        blocks += [mq, mk]
    return np.concatenate(blocks, axis=0)


def _hgrn_level_index():
    i = np.arange(HGRN_CHUNK)
    x = i[:, None] ^ i[None, :]
    lv = np.zeros_like(x)
    nz = x > 0
    lv[nz] = np.floor(np.log2(x[nz])).astype(x.dtype) + 1
    return lv.astype(np.int32)


_HGRN_TRI = _hgrn_triangles()
_HGRN_PSUM = _hgrn_partial_sum_matrix()
_HGRN_LEVEL = _hgrn_level_index()

_NT = (((1,), (1,)), ((), ()))
_TN = (((0,), (0,)), ((), ()))


def _hgrn_kernel(*refs, n_chunks, has_s0, has_prev, want_final):
    q_ref, ff_ref, fb_ref, v_ref, ga_ref, lb_ref, gon_ref, tri_ref, psum_ref, lvl_ref = refs[:10]
    pos = 10
    s0_ref = sfin_ref = None
    if has_s0:
        s0_ref = refs[pos]
        pos += 1
    if has_prev:
        pos += 1
    y_ref = refs[pos]
    pos += 1
    if want_final:
        sfin_ref = refs[pos]
        pos += 1
    o_scr, qfb_scr, dst_scr, dec_scr, st_scr = refs[pos:]

    c = HGRN_CHUNK
    lbv = lb_ref[...]
    lb_f, lb_b = lbv[0:1], lbv[1:2]
    level = lvl_ref[...]
    row8 = jnp.concatenate([lax.broadcasted_iota(jnp.int32, (8, HEAD_D), 0)] * (c // 8), axis=0)

    def chunk_rows(ci):
        return pl.ds(pl.multiple_of(ci * c, c), c)

    def hi_mid(x):
        hi = x.astype(BF16)
        return hi, (x - hi.astype(F32)).astype(BF16)

    def summed(zero_one, hi, mid):
        dd = jnp.dot(zero_one, jnp.concatenate([hi, mid], axis=1), preferred_element_type=F32)
        return dd[:, :HEAD_D] + dd[:, HEAD_D:]

    def local_pass(ci, carry):
        rows = chunk_rows(ci)
        f_f = lb_f + (1.0 - lb_f) * jax.nn.sigmoid(ff_ref[rows, :])
        f_b = lb_b + (1.0 - lb_b) * jax.nn.sigmoid(fb_ref[rows, :])
        k_f, k_b = 1.0 - f_f, 1.0 - f_b
        hi_f, mid_f = hi_mid(jnp.log2(f_f))
        hi_b, mid_b = hi_mid(jnp.log2(f_b))
        b_f = summed(tri_ref[0:c, :], hi_f, mid_f)
        b_b = summed(tri_ref[c:2 * c, :], hi_b, mid_b)
        small = summed(psum_ref[...], jnp.concatenate([hi_f, hi_b], axis=0),
                       jnp.concatenate([mid_f, mid_b], axis=0))

        q = _silu(q_ref[rows, :].astype(F32))
        v = v_ref[rows, :]

        def level_scores(d_q, d_k, k_sel):
            qt = (q * jnp.exp2(d_q)).astype(BF16)
            kt = (k_sel * jnp.exp2(d_k)).astype(BF16)
            return lax.dot_general(qt, kt, _NT, preferred_element_type=F32)

        s = lax.dot_general(q.astype(BF16), (k_f + k_b).astype(BF16), _NT,
                            preferred_element_type=F32)
        s = jnp.where(level == 0, s, 0.0)
        odd = (row8 & 1) == 1
        q1 = (q * jnp.where(odd, f_f, f_b)).astype(BF16)
        k1 = jnp.where(odd, k_b, k_f).astype(BF16)
        s = jnp.where(level == 1, lax.dot_general(q1, k1, _NT, preferred_element_type=F32), s)
        for li, m in enumerate(HGRN_PSUM_LEVELS):
            second = (row8 & m) == m
            s_l = level_scores(small[2 * li * c:(2 * li + 1) * c], small[(2 * li + 1) * c:(2 * li + 2) * c],
                               jnp.where(second, k_b, k_f))
            s = jnp.where(level == m.bit_length(), s_l, s)
        for m in HGRN_LEVELS:
            dq_parts, dk_parts, k_parts = [], [], []
            for lo in range(0, c, 2 * m):
                mid_row, hi_row = lo + m, lo + 2 * m
                edge_f = b_f[mid_row - 1:mid_row]
                edge_b = b_b[mid_row:mid_row + 1]
                dq_parts += [b_b[lo:mid_row] - edge_b, b_f[mid_row:hi_row] - edge_f]
                dk_parts += [edge_f - b_f[lo:mid_row], edge_b - b_b[mid_row:hi_row]]
                k_parts += [k_f[lo:mid_row], k_b[mid_row:hi_row]]
            s_l = level_scores(jnp.concatenate(dq_parts, axis=0), jnp.concatenate(dk_parts, axis=0),
                               jnp.concatenate(k_parts, axis=0))
            s = jnp.where(level == m.bit_length(), s_l, s)

        q_fb = jnp.concatenate([q * jnp.exp2(b_f), q * jnp.exp2(b_b)], axis=1).astype(BF16)
        k_fb = jnp.concatenate([k_f * jnp.exp2(b_f[c - 1:c] - b_f),
                                k_b * jnp.exp2(b_b[0:1] - b_b)], axis=1).astype(BF16)
        dec = jnp.concatenate([jnp.exp2(b_f[c - 1:c]), jnp.exp2(b_b[0:1])], axis=1)

        o_scr[rows, :] = jnp.dot(s.astype(BF16), v, preferred_element_type=F32)
        qfb_scr[rows, :] = q_fb
        dst_scr[ci] = lax.dot_general(v, k_fb, _TN, preferred_element_type=F32)
        dec_scr[ci] = jnp.broadcast_to(dec, (8, 2 * HEAD_D))
        return carry

    lax.fori_loop(0, n_chunks, local_pass, 0, unroll=min(n_chunks, 8))

    if has_s0:
        st_f0 = s0_ref[0, 0, 0, 0].T
        st_b0 = s0_ref[0, 0, 1, 0].T
    else:
        st_f0 = jnp.zeros((HEAD_D, HEAD_D), F32)
        st_b0 = jnp.zeros((HEAD_D, HEAD_D), F32)

    def fwd_scan(ci, st):
        st_scr[ci, :, 0:HEAD_D] = st.astype(BF16)
        return st * dec_scr[ci, 0:1, 0:HEAD_D] + dst_scr[ci, :, 0:HEAD_D]

    def bwd_scan(k, st):
        ci = n_chunks - 1 - k
        st_scr[ci, :, HEAD_D:2 * HEAD_D] = st.astype(BF16)
        return st * dec_scr[ci, 0:1, HEAD_D:2 * HEAD_D] + dst_scr[ci, :, HEAD_D:2 * HEAD_D]

    st_f = lax.fori_loop(0, n_chunks, fwd_scan, st_f0)
    st_b = lax.fori_loop(0, n_chunks, bwd_scan, st_b0)
    if want_final:
        sfin_ref[0, 0, 0, 0] = st_f.T
        sfin_ref[0, 0, 1, 0] = st_b.T

    def output_pass(ci, carry):
        rows = chunk_rows(ci)
        o = o_scr[rows, :] + lax.dot_general(qfb_scr[rows, :], st_scr[ci], _NT,
                                             preferred_element_type=F32)
        y_ref[rows, :] = (_rms(o, gon_ref[...]) * _silu(ga_ref[rows, :].astype(F32))).astype(BF16)
        return carry

    lax.fori_loop(0, n_chunks, output_pass, 0, unroll=min(n_chunks, 4))


def _hgrn_mixer(gates, proj, lb, g_onorm, state, layer_j, seq_len, want_final, final_prev=None):
    n_tok = proj.shape[0]
    n_seq = n_tok // seq_len
    n_chunks = seq_len // HGRN_CHUNK
    has_s0 = state is not None
    has_prev = final_prev is not None
    blk = lambda part: pl.BlockSpec(
        (seq_len, HEAD_D), lambda b, h: (b, part * N_HEADS + h))
    in_specs = [blk(AB_PART["q"]), blk(0), blk(1), blk(AB_PART["i"]), blk(AB_PART["gate_a"]),
                pl.BlockSpec((2, HEAD_D), lambda b, h: (0, h)),
                pl.BlockSpec((1, HEAD_D), lambda b, h: (0, 0)),
                pl.BlockSpec(_HGRN_TRI.shape, lambda b, h: (0, 0)),
                pl.BlockSpec(_HGRN_PSUM.shape, lambda b, h: (0, 0)),
                pl.BlockSpec(_HGRN_LEVEL.shape, lambda b, h: (0, 0))]
    args = [proj, gates, gates, proj, proj, lb, g_onorm.reshape(1, HEAD_D),
            jnp.asarray(_HGRN_TRI, BF16), jnp.asarray(_HGRN_PSUM, BF16), jnp.asarray(_HGRN_LEVEL)]
    if has_s0:
        in_specs.append(pl.BlockSpec((1, 1, 2, 1, HEAD_D, HEAD_D),
                                     lambda b, h: (b, layer_j, 0, h, 0, 0)))
        args.append(state)
    aliases = {}
    if has_prev:
        aliases[len(args)] = 1
        in_specs.append(pl.BlockSpec(memory_space=pl.ANY))
        args.append(final_prev)
    out_specs = [pl.BlockSpec((seq_len, HEAD_D), lambda b, h: (b, h))]
    out_shape = [jax.ShapeDtypeStruct((n_tok, BRANCH_W), BF16)]
    if want_final:
        out_specs.append(pl.BlockSpec((1, 1, 2, 1, HEAD_D, HEAD_D),
                                      lambda b, h: (b, layer_j, 0, h, 0, 0)))
        out_shape.append(jax.ShapeDtypeStruct((n_seq, N_AB, 2, N_HEADS, HEAD_D, HEAD_D), F32))
    outs = pl.pallas_call(
        functools.partial(_hgrn_kernel, n_chunks=n_chunks, has_s0=has_s0, has_prev=has_prev,
                          want_final=want_final),
        grid=(n_seq, N_HEADS),
        in_specs=in_specs,
        out_specs=out_specs,
        out_shape=out_shape,
        input_output_aliases=aliases,
        scratch_shapes=[
            pltpu.VMEM((seq_len, HEAD_D), F32),
            pltpu.VMEM((seq_len, 2 * HEAD_D), BF16),
            pltpu.VMEM((n_chunks, HEAD_D, 2 * HEAD_D), F32),
            pltpu.VMEM((n_chunks, 8, 2 * HEAD_D), F32),
            pltpu.VMEM((n_chunks, HEAD_D, 2 * HEAD_D), BF16),
        ],
        compiler_params=_cparams(2, 48),
        name="hgrn_mixer",
    )(*args)
    return outs if want_final else (outs[0], None)


def kernel(x_prompt, x_sample, c, state_hgrn, c_ctx, w_ada, b_ada, g_pre, g_post, w_in_ab, w_out_ab,
           lb_logits, g_onorm_a, w_pool, pool_scale, w_in_c, w_out_c, ln_v_g, ln_v_b, w_spatial,
           b_spatial):
    n_batch, seq_p, _ = x_prompt.shape
    dec_batch, seq_s, _ = x_sample.shape
    tm_in, tm_out, tn, tm_c, pool_rows = 1024, 512, 1024, 512, 2048

    cond = jnp.concatenate(
        [c_ctx[None, :], c, jnp.zeros((N_COND_ROWS - 1 - dec_batch, D_MODEL), F32)], axis=0)
    mods = _modulations(cond, w_ada, b_ada).reshape(DEPTH, N_COND_ROWS, 3, D_MODEL)
    lb_all = _lower_bounds(lb_logits)

    flows = [
        dict(x=x_prompt.reshape(n_batch * seq_p, D_MODEL), seq=seq_p, grid=False, state=None,
             row=lambda tm: (lambda i: 0)),
        dict(x=x_sample.reshape(dec_batch * seq_s, D_MODEL), seq=seq_s, grid=True, state=state_hgrn,
             row=lambda tm: (lambda i: 1 + i // (seq_s // tm))),
    ]
    new_state = None
    for l in range(DEPTH):
        j = l // 2
        if l % 2 == 0:
            w = w_in_ab[j]
            w_in = jnp.concatenate([w[:, BRANCH_W:3 * BRANCH_W], w[:, :BRANCH_W], w[:, 3 * BRANCH_W:]],
                                   axis=1).astype(BF16)
            w_out = w_out_ab[j].astype(BF16)
            for fi, fl in enumerate(flows):
                gates, proj = _in_proj(fl["x"], mods[l], g_pre[l], w_in, fl["row"](tm_in), tm_in, tn,
                                       n_f32_tiles=AB_F32_PARTS * BRANCH_W // tn)
                if fi == 0:
                    y_a, new_state = _hgrn_mixer(gates, proj, lb_all[j], g_onorm_a[j], None, j,
                                                 fl["seq"], want_final=True, final_prev=new_state)
                else:
                    y_a, _ = _hgrn_mixer(gates, proj, lb_all[j], g_onorm_a[j], fl["state"], j,
                                         fl["seq"], want_final=False)
                y_b = _pool_mixer(proj, w_pool[j], pool_scale[j], fl["seq"], fl["grid"], pool_rows)
                fl["x"] = _out_proj([y_a, y_b], [w_out[:BRANCH_W], w_out[BRANCH_W:]], fl["x"],
                                    mods[l], g_post[l], fl["row"](tm_out), tm_out)
        else:
            w_in, w_out = w_in_c[j].astype(BF16), w_out_c[j].astype(BF16)
            for fl in flows:
                fl["x"] = _layer_c(fl["x"], mods[l], g_pre[l], g_post[l], w_in, w_out, ln_v_g[j],
                                   ln_v_b[j], w_spatial[j], b_spatial[j], fl["row"](tm_c), tm_c)
    y_p = flows[0]["x"].reshape(n_batch, seq_p, D_MODEL)
    y_s = flows[1]["x"].reshape(dec_batch, seq_s, D_MODEL)
    return (y_p, y_s, new_state)
```

```python
import functools

import numpy as np
import jax
import jax.numpy as jnp
from jax import lax
from jax.experimental import pallas as pl
from jax.experimental.pallas import tpu as pltpu

F32 = jnp.float32
BF16 = jnp.bfloat16

D_MODEL = 1024
DEPTH = 4
N_AB = (DEPTH + 1) // 2
MIX_WIDTH = 2 * D_MODEL
BRANCH_W = MIX_WIDTH // 2
HEAD_D = 128
N_HEADS = BRANCH_W // HEAD_D
GRID_W = 64
POOL_WINDOWS = (2, 4, 8, 16)
POOL_GROUP_W = BRANCH_W // len(POOL_WINDOWS)
CHUNK_C = 128
N_GROUPS_C = 8
GROUP_W_C = MIX_WIDTH // N_GROUPS_C
AB_IN = 7 * BRANCH_W
C_IN = 3 * MIX_WIDTH
AB_F32_PARTS = 2
AB_PART = {"q": 0, "i": 1, "gate_a": 2, "pool_in": 3, "gate_b": 4}
EPS = 1e-6
N_COND_ROWS = 8

HGRN_CHUNK = 128
HGRN_LEVELS = (64, 32, 16, 8)
HGRN_PSUM_LEVELS = (2, 4)
MIB = 1024 * 1024


def _cparams(n_axes, vmem_mib):
    return pltpu.CompilerParams(
        dimension_semantics=("arbitrary",) * n_axes, vmem_limit_bytes=int(vmem_mib * MIB))


def _resident(shape):
    zeros = (0,) * len(shape)
    return pl.BlockSpec(shape, lambda *_: zeros, pipeline_mode=pl.Buffered(1))


def _silu(x):
    return x * jax.nn.sigmoid(x)


def _rms(x, g):
    return x * lax.rsqrt(jnp.mean(x * x, axis=-1, keepdims=True) + EPS) * g


def _modulated_norm(x, mod, g):
    return _rms(x, g) * (1.0 + mod[1:2]) + mod[0:1]


def _mod_kernel(cond_ref, w_ref, b_ref, o_ref):
    a = _silu(cond_ref[...])
    o_ref[0] = jnp.dot(a, w_ref[0], preferred_element_type=F32,
                       precision=lax.Precision.HIGHEST) + b_ref[0]


def _modulations(cond, w_ada, b_ada):
    tn = 1024
    n3 = 3 * D_MODEL
    return pl.pallas_call(
        _mod_kernel,
        grid=(DEPTH, n3 // tn),
        in_specs=[
            pl.BlockSpec((N_COND_ROWS, D_MODEL), lambda l, j: (0, 0)),
            pl.BlockSpec((1, D_MODEL, tn), lambda l, j: (l, 0, j)),
            pl.BlockSpec((1, 1, tn), lambda l, j: (l, 0, j)),
        ],
        out_specs=pl.BlockSpec((1, N_COND_ROWS, tn), lambda l, j: (l, 0, j)),
        out_shape=jax.ShapeDtypeStruct((DEPTH, N_COND_ROWS, n3), F32),
        compiler_params=_cparams(2, 24),
        name="adaln_modulation",
    )(cond, w_ada, b_ada.reshape(DEPTH, 1, n3))


def _lb_kernel(x_ref, o_ref):
    x = x_ref[...]
    e = jnp.exp(x - jnp.max(x, axis=0, keepdims=True))
    p = e / jnp.sum(e, axis=0, keepdims=True)
    run = p[0]
    o_ref[0] = run - p[0]
    for l in range(1, x.shape[0]):
        run = run + p[l]
        o_ref[l] = run - p[0]


def _lower_bounds(lb_logits):
    return pl.pallas_call(
        _lb_kernel, out_shape=jax.ShapeDtypeStruct(lb_logits.shape, F32), name="hgrn_lower_bounds",
    )(lb_logits)


def _in_kernel(x_ref, mod_ref, g_ref, w_ref, *rest, n_f32_tiles):
    out_refs, h_ref = rest[:-1], rest[-1]
    j = pl.program_id(1)

    @pl.when(j == 0)
    def _():
        h_ref[...] = _modulated_norm(x_ref[...], mod_ref[0], g_ref[...]).astype(BF16)

    def project(o_ref):
        o_ref[...] = jnp.dot(h_ref[...], w_ref[...], preferred_element_type=F32).astype(o_ref.dtype)

    if n_f32_tiles:
        pl.when(j < n_f32_tiles)(lambda: project(out_refs[0]))
        pl.when(j >= n_f32_tiles)(lambda: project(out_refs[1]))
    else:
        project(out_refs[0])


def _in_proj(x, mod, g, w, row_of_tile, tm, tn, n_f32_tiles=0):
    n_tok, n_out = x.shape[0], w.shape[1]
    n_tiles = n_out // tn
    out_specs, out_shape = [], []
    if n_f32_tiles:
        out_specs.append(pl.BlockSpec((tm, tn), lambda i, j: (i, jnp.minimum(j, n_f32_tiles - 1))))
        out_shape.append(jax.ShapeDtypeStruct((n_tok, n_f32_tiles * tn), F32))
    out_specs.append(pl.BlockSpec((tm, tn), lambda i, j: (i, jnp.maximum(j - n_f32_tiles, 0))))
    out_shape.append(jax.ShapeDtypeStruct((n_tok, (n_tiles - n_f32_tiles) * tn), BF16))
    return pl.pallas_call(
        functools.partial(_in_kernel, n_f32_tiles=n_f32_tiles),
        grid=(n_tok // tm, n_tiles),
        in_specs=[
            pl.BlockSpec((tm, D_MODEL), lambda i, j: (i, 0)),
            pl.BlockSpec((1, 3, D_MODEL), lambda i, j: (row_of_tile(i), 0, 0)),
            pl.BlockSpec((1, D_MODEL), lambda i, j: (0, 0)),
            pl.BlockSpec((D_MODEL, tn), lambda i, j: (0, j)),
        ],
        out_specs=out_specs,
        out_shape=out_shape,
        scratch_shapes=[pltpu.VMEM((tm, D_MODEL), BF16)],
        compiler_params=_cparams(2, 40),
        name="in_projection",
    )(x, mod, g.reshape(1, D_MODEL), w)


def _out_kernel(*refs, n_parts):
    y_refs, w_refs = refs[:n_parts], refs[n_parts:2 * n_parts]
    x_ref, mod_ref, g_ref, o_ref = refs[2 * n_parts:]
    acc = jnp.dot(y_refs[0][...], w_refs[0][...], preferred_element_type=F32)
    for y_ref, w_ref in zip(y_refs[1:], w_refs[1:]):
        acc = acc + jnp.dot(y_ref[...], w_ref[...], preferred_element_type=F32)
    o_ref[...] = x_ref[...] + mod_ref[0][2:3] * _rms(acc, g_ref[...])


def _out_proj(ys, ws, x, mod, g, row_of_tile, tm):
    n_tok = x.shape[0]
    n_parts = len(ys)
    in_specs = [pl.BlockSpec((tm, y.shape[1]), lambda i: (i, 0)) for y in ys]
    in_specs += [pl.BlockSpec(w.shape, lambda i: (0, 0)) for w in ws]
    in_specs += [
        pl.BlockSpec((tm, D_MODEL), lambda i: (i, 0)),
        pl.BlockSpec((1, 3, D_MODEL), lambda i: (row_of_tile(i), 0, 0)),
        pl.BlockSpec((1, D_MODEL), lambda i: (0, 0)),
    ]
    return pl.pallas_call(
        functools.partial(_out_kernel, n_parts=n_parts),
        grid=(n_tok // tm,),
        in_specs=in_specs,
        out_specs=pl.BlockSpec((tm, D_MODEL), lambda i: (i, 0)),
        out_shape=jax.ShapeDtypeStruct((n_tok, D_MODEL), F32),
        compiler_params=_cparams(1, 40),
        name="out_projection",
    )(*ys, *ws, x, mod, g.reshape(1, D_MODEL))


def _layer_c_kernel(x_ref, mod_ref, gpre_ref, win_ref, lng_ref, lnb_ref, ws_ref, bs_ref, wout_ref,
                    gpost_ref, o_ref, y_scr, *, tm):
    x = x_ref[...]
    mod = mod_ref[0]
    h = _modulated_norm(x, mod, gpre_ref[...]).astype(BF16)

    v = jnp.dot(h, win_ref[:, MIX_WIDTH:2 * MIX_WIDTH], preferred_element_type=F32)
    mu = jnp.mean(v, axis=-1, keepdims=True)
    vc = v - mu
    var = jnp.mean(vc * vc, axis=-1, keepdims=True)
    vn = (vc * lax.rsqrt(var + EPS) * lng_ref[...] + lnb_ref[...]).astype(BF16)

    for gi in range(N_GROUPS_C):
        cols = slice(gi * GROUP_W_C, (gi + 1) * GROUP_W_C)
        u = jnp.dot(h, win_ref[:, cols], preferred_element_type=F32)
        gate = _silu(jnp.dot(h, win_ref[:, 2 * MIX_WIDTH + gi * GROUP_W_C:
                                        2 * MIX_WIDTH + (gi + 1) * GROUP_W_C],
                             preferred_element_type=F32))
        for n in range(tm // CHUNK_C):
            rows = slice(n * CHUNK_C, (n + 1) * CHUNK_C)
            sp = jnp.dot(ws_ref[gi], vn[rows, cols], preferred_element_type=F32) + bs_ref[:, gi:gi + 1]
            y_scr[rows, cols] = (u[rows] * sp * gate[rows]).astype(BF16)

    acc = jnp.dot(y_scr[...], wout_ref[...], preferred_element_type=F32)
    o_ref[...] = x + mod[2:3] * _rms(acc, gpost_ref[...])


def _layer_c(x, mod, g_pre, g_post, w_in, w_out, ln_g, ln_b, w_s, b_s, row_of_tile, tm):
    n_tok = x.shape[0]
    return pl.pallas_call(
        functools.partial(_layer_c_kernel, tm=tm),
        grid=(n_tok // tm,),
        in_specs=[
            pl.BlockSpec((tm, D_MODEL), lambda i: (i, 0)),
            pl.BlockSpec((1, 3, D_MODEL), lambda i: (row_of_tile(i), 0, 0)),
            _resident((1, D_MODEL)),
            _resident((D_MODEL, C_IN)),
            _resident((1, MIX_WIDTH)),
            _resident((1, MIX_WIDTH)),
            _resident((N_GROUPS_C, CHUNK_C, CHUNK_C)),
            _resident((CHUNK_C, N_GROUPS_C)),
            _resident((MIX_WIDTH, D_MODEL)),
            _resident((1, D_MODEL)),
        ],
        out_specs=pl.BlockSpec((tm, D_MODEL), lambda i: (i, 0)),
        out_shape=jax.ShapeDtypeStruct((n_tok, D_MODEL), F32),
        scratch_shapes=[pltpu.VMEM((tm, MIX_WIDTH), BF16)],
        compiler_params=_cparams(1, 48),
        name="gmlp_layer",
    )(x, mod, g_pre.reshape(1, D_MODEL), w_in, ln_g.reshape(1, MIX_WIDTH), ln_b.reshape(1, MIX_WIDTH),
      w_s.astype(BF16), b_s.T, w_out, g_post.reshape(1, D_MODEL))


def _window_counts(n, w):
    return [min(i - w // 2 + w, n) - max(i - w // 2, 0) for i in range(n)]


def _mean_minor(x3, w):
    g, n, ch = x3.shape
    idx = lax.broadcasted_iota(jnp.int32, (1, n, ch), 1)

    def shifted(a, k):
        return pltpu.roll(a.reshape(g * n, ch), k % (g * n), 0).reshape(g, n, ch)

    trail, lead = x3, x3
    size = 1
    while size < w // 2:
        trail = trail + jnp.where(idx >= size, shifted(trail, size), 0.0)
        lead = lead + jnp.where(idx + size < n, shifted(lead, -size), 0.0)
        size *= 2
    s = jnp.where(idx >= 1, shifted(trail, 1), 0.0) + lead
    half = w // 2
    cnt = jnp.minimum(idx - half + w, n) - jnp.maximum(idx - half, 0)
    return s * (1.0 / cnt.astype(F32))


def _mean_major(x3, w):
    n = x3.shape[0]
    zeros = lambda k: jnp.zeros((k,) + x3.shape[1:], F32)
    back = lambda a, k: jnp.concatenate([zeros(k), a[:n - k]], axis=0)
    ahead = lambda a, k: jnp.concatenate([a[k:], zeros(k)], axis=0)
    trail, lead = x3, x3
    size = 1
    while size < w // 2:
        trail = trail + back(trail, size)
        lead = lead + ahead(lead, size)
        size *= 2
    s = back(trail, 1) + lead
    return jnp.concatenate(
        [s[i:i + 1] * (1.0 / cnt) for i, cnt in enumerate(_window_counts(n, w))], axis=0)


def _pool_kernel(p_ref, gate_ref, wp_ref, ps_ref, y_ref, *, seq_len, grid_mode):
    grp = pl.program_id(1)
    rows, ch = p_ref.shape
    for k, w in enumerate(POOL_WINDOWS):
        @pl.when(grp == k)
        def _(w=w):
            x = p_ref[...].astype(F32)
            if grid_mode:
                m = _mean_major(x.reshape(rows // GRID_W, GRID_W, ch), w)
                m = _mean_minor(m, w).reshape(rows, ch)
            else:
                m = _mean_minor(x.reshape(rows // seq_len, seq_len, ch), w).reshape(rows, ch)
            dlt = (m - x).astype(BF16)
            y = jnp.dot(dlt, wp_ref[0], preferred_element_type=F32) * ps_ref[...]
            y_ref[...] = (y * _silu(gate_ref[...].astype(F32))).astype(BF16)


def _pool_mixer(proj, w_pool, pool_scale, seq_len, grid_mode, rows_per_step):
    n_tok = proj.shape[0]
    n_grp = len(POOL_WINDOWS)
    p_blk0 = AB_PART["pool_in"] * BRANCH_W // POOL_GROUP_W
    g_blk0 = AB_PART["gate_b"] * BRANCH_W // POOL_GROUP_W
    return pl.pallas_call(
        functools.partial(_pool_kernel, seq_len=seq_len, grid_mode=grid_mode),
        grid=(n_tok // rows_per_step, n_grp),
        in_specs=[
            pl.BlockSpec((rows_per_step, POOL_GROUP_W), lambda b, g: (b, p_blk0 + g)),
            pl.BlockSpec((rows_per_step, POOL_GROUP_W), lambda b, g: (b, g_blk0 + g)),
            pl.BlockSpec((1, POOL_GROUP_W, POOL_GROUP_W), lambda b, g: (g, 0, 0)),
            pl.BlockSpec((1, POOL_GROUP_W), lambda b, g: (0, g)),
        ],
        out_specs=pl.BlockSpec((rows_per_step, POOL_GROUP_W), lambda b, g: (b, g)),
        out_shape=jax.ShapeDtypeStruct((n_tok, BRANCH_W), BF16),
        compiler_params=_cparams(2, 56),
        name="pool_mixer",
    )(proj, proj, w_pool.astype(BF16), pool_scale.reshape(1, BRANCH_W))


def _hgrn_triangles():
    ones = np.ones((HGRN_CHUNK, HGRN_CHUNK), np.float32)
    return np.concatenate([np.tril(ones), np.triu(ones)], axis=0)


def _hgrn_partial_sum_matrix():
    c = HGRN_CHUNK
    blocks = []
    for m in HGRN_PSUM_LEVELS:
        mq = np.zeros((c, 2 * c), np.float32)
        mk = np.zeros((c, 2 * c), np.float32)
        for i in range(c):
            start = (i // (2 * m)) * (2 * m)
            r1, r2 = start + m - 1, start + m
            if i >= r2:
                mq[i, r2:i + 1] = 1
                mk[i, c + r2:c + i] = 1
            else:
                mq[i, c + i:c + r1 + 1] = 1
                mk[i, i + 1:r1 + 1] = 1
        blocks += [mq, mk]
    return np.concatenate(blocks, axis=0)


def _hgrn_level_index():
    i = np.arange(HGRN_CHUNK)
    x = i[:, None] ^ i[None, :]
    lv = np.zeros_like(x)
    nz = x > 0
    lv[nz] = np.floor(np.log2(x[nz])).astype(x.dtype) + 1
    return lv.astype(np.int32)


_HGRN_TRI = _hgrn_triangles()
_HGRN_PSUM = _hgrn_partial_sum_matrix()
_HGRN_LEVEL = _hgrn_level_index()

_NT = (((1,), (1,)), ((), ()))
_TN = (((0,), (0,)), ((), ()))


def _hgrn_kernel(*refs, n_chunks, heads, has_s0, has_prev, want_final):
    q_ref, ff_ref, fb_ref, v_ref, ga_ref, lb_ref, gon_ref, tri_ref, psum_ref, lvl_ref = refs[:10]
    pos = 10
    s0_ref = sfin_ref = None
    if has_s0:
        s0_ref = refs[pos]
        pos += 1
    if has_prev:
        pos += 1
    y_ref = refs[pos]
    pos += 1
    if want_final:
        sfin_ref = refs[pos]
        pos += 1
    o_scr, qfb_scr, dst_scr, dec_scr, st_scr = refs[pos:]

    c = HGRN_CHUNK
    inline = heads * n_chunks <= 8
    level = lvl_ref[...]
    row8 = jnp.concatenate([lax.broadcasted_iota(jnp.int32, (8, HEAD_D), 0)] * (c // 8), axis=0)

    def chunk_rows(ci):
        if isinstance(ci, int):
            return pl.ds(ci * c, c)
        return pl.ds(pl.multiple_of(ci * c, c), c)

    def over_chunks(body, init, unroll):
        if inline:
            carry = init
            for ci in range(n_chunks):
                carry = body(ci, carry)
            return carry
        return lax.fori_loop(0, n_chunks, body, init, unroll=unroll)

    def hi_mid(x):
        hi = x.astype(BF16)
        return hi, (x - hi.astype(F32)).astype(BF16)

    def summed(zero_one, hi, mid):
        dd = jnp.dot(zero_one, jnp.concatenate([hi, mid], axis=1), preferred_element_type=F32)
        return dd[:, :HEAD_D] + dd[:, HEAD_D:]

    def local_pass(hh, ci, carry):
        rows = chunk_rows(ci)
        lanes = slice(hh * HEAD_D, (hh + 1) * HEAD_D)
        lb_f, lb_b = lb_ref[0:1, lanes], lb_ref[1:2, lanes]
        f_f = lb_f + (1.0 - lb_f) * jax.nn.sigmoid(ff_ref[rows, lanes])
        f_b = lb_b + (1.0 - lb_b) * jax.nn.sigmoid(fb_ref[rows, lanes])
        k_f, k_b = 1.0 - f_f, 1.0 - f_b
        hi_f, mid_f = hi_mid(jnp.log2(f_f))
        hi_b, mid_b = hi_mid(jnp.log2(f_b))
        b_f = summed(tri_ref[0:c, :], hi_f, mid_f)
        b_b = summed(tri_ref[c:2 * c, :], hi_b, mid_b)
        small = summed(psum_ref[...], jnp.concatenate([hi_f, hi_b], axis=0),
                       jnp.concatenate([mid_f, mid_b], axis=0))

        q = _silu(q_ref[rows, lanes].astype(F32))
        v = v_ref[rows, lanes]

        def level_scores(d_q, d_k, k_sel):
            qt = (q * jnp.exp2(d_q)).astype(BF16)
            kt = (k_sel * jnp.exp2(d_k)).astype(BF16)
            return lax.dot_general(qt, kt, _NT, preferred_element_type=F32)

        s = lax.dot_general(q.astype(BF16), (k_f + k_b).astype(BF16), _NT,
                            preferred_element_type=F32)
        s = jnp.where(level == 0, s, 0.0)
        odd = (row8 & 1) == 1
        q1 = (q * jnp.where(odd, f_f, f_b)).astype(BF16)
        k1 = jnp.where(odd, k_b, k_f).astype(BF16)
        s = jnp.where(level == 1, lax.dot_general(q1, k1, _NT, preferred_element_type=F32), s)
        for li, m in enumerate(HGRN_PSUM_LEVELS):
            second = (row8 & m) == m
            s_l = level_scores(small[2 * li * c:(2 * li + 1) * c], small[(2 * li + 1) * c:(2 * li + 2) * c],
                               jnp.where(second, k_b, k_f))
            s = jnp.where(level == m.bit_length(), s_l, s)
        for m in HGRN_LEVELS:
            dq_parts, dk_parts, k_parts = [], [], []
            for lo in range(0, c, 2 * m):
                mid_row, hi_row = lo + m, lo + 2 * m
                edge_f = b_f[mid_row - 1:mid_row]
                edge_b = b_b[mid_row:mid_row + 1]
                dq_parts += [b_b[lo:mid_row] - edge_b, b_f[mid_row:hi_row] - edge_f]
                dk_parts += [edge_f - b_f[lo:mid_row], edge_b - b_b[mid_row:hi_row]]
                k_parts += [k_f[lo:mid_row], k_b[mid_row:hi_row]]
            s_l = level_scores(jnp.concatenate(dq_parts, axis=0), jnp.concatenate(dk_parts, axis=0),
                               jnp.concatenate(k_parts, axis=0))
            s = jnp.where(level == m.bit_length(), s_l, s)

        q_fb = jnp.concatenate([q * jnp.exp2(b_f), q * jnp.exp2(b_b)], axis=1).astype(BF16)
        k_fb = jnp.concatenate([k_f * jnp.exp2(b_f[c - 1:c] - b_f),
                                k_b * jnp.exp2(b_b[0:1] - b_b)], axis=1).astype(BF16)
        dec = jnp.concatenate([jnp.exp2(b_f[c - 1:c]), jnp.exp2(b_b[0:1])], axis=1)

        o_scr[hh, rows, :] = jnp.dot(s.astype(BF16), v, preferred_element_type=F32)
        qfb_scr[hh, rows, :] = q_fb
        dst_scr[hh, ci] = lax.dot_general(v, k_fb, _TN, preferred_element_type=F32)
        dec_scr[hh, ci] = jnp.broadcast_to(dec, (8, 2 * HEAD_D))
        return carry

    for hh in range(heads):
        over_chunks(functools.partial(local_pass, hh), 0, unroll=min(n_chunks, 8))

    for hh in range(heads):
        if has_s0:
            st_f0 = s0_ref[0, 0, 0, hh].T
            st_b0 = s0_ref[0, 0, 1, hh].T
        else:
            st_f0 = jnp.zeros((HEAD_D, HEAD_D), F32)
            st_b0 = jnp.zeros((HEAD_D, HEAD_D), F32)

        def fwd_scan(ci, st, hh=hh):
            st_scr[hh, ci, :, 0:HEAD_D] = st.astype(BF16)
            return st * dec_scr[hh, ci, 0:1, 0:HEAD_D] + dst_scr[hh, ci, :, 0:HEAD_D]

        def bwd_scan(k, st, hh=hh):
            ci = n_chunks - 1 - k
            st_scr[hh, ci, :, HEAD_D:2 * HEAD_D] = st.astype(BF16)
            return (st * dec_scr[hh, ci, 0:1, HEAD_D:2 * HEAD_D]
                    + dst_scr[hh, ci, :, HEAD_D:2 * HEAD_D])

        st_f = over_chunks(fwd_scan, st_f0, unroll=1)
        st_b = over_chunks(bwd_scan, st_b0, unroll=1)
        if want_final:
            sfin_ref[0, 0, 0, hh] = st_f.T
            sfin_ref[0, 0, 1, hh] = st_b.T

    def output_pass(hh, ci, carry):
        rows = chunk_rows(ci)
        lanes = slice(hh * HEAD_D, (hh + 1) * HEAD_D)
        o = o_scr[hh, rows, :] + lax.dot_general(qfb_scr[hh, rows, :], st_scr[hh, ci], _NT,
                                                 preferred_element_type=F32)
        y_ref[rows, lanes] = (_rms(o, gon_ref[...])
                              * _silu(ga_ref[rows, lanes].astype(F32))).astype(BF16)
        return carry

    for hh in range(heads):
        over_chunks(functools.partial(output_pass, hh), 0, unroll=min(n_chunks, 4))


def _hgrn_mixer(gates, proj, lb, g_onorm, state, layer_j, seq_len, heads, want_final,
                final_prev=None):
    n_tok = proj.shape[0]
    n_seq = n_tok // seq_len
    n_chunks = seq_len // HGRN_CHUNK
    has_s0 = state is not None
    has_prev = final_prev is not None
    width = heads * HEAD_D
    blk = lambda part: pl.BlockSpec(
        (seq_len, width), lambda b, h: (b, part * (N_HEADS // heads) + h))
    in_specs = [blk(AB_PART["q"]), blk(0), blk(1), blk(AB_PART["i"]), blk(AB_PART["gate_a"]),
                pl.BlockSpec((2, width), lambda b, h: (0, h)),
                pl.BlockSpec((1, HEAD_D), lambda b, h: (0, 0)),
                pl.BlockSpec(_HGRN_TRI.shape, lambda b, h: (0, 0)),
                pl.BlockSpec(_HGRN_PSUM.shape, lambda b, h: (0, 0)),
                pl.BlockSpec(_HGRN_LEVEL.shape, lambda b, h: (0, 0))]
    args = [proj, gates, gates, proj, proj, lb, g_onorm.reshape(1, HEAD_D),
            jnp.asarray(_HGRN_TRI, BF16), jnp.asarray(_HGRN_PSUM, BF16), jnp.asarray(_HGRN_LEVEL)]
    state_blk = pl.BlockSpec((1, 1, 2, heads, HEAD_D, HEAD_D), lambda b, h: (b, layer_j, 0, h, 0, 0))
    if has_s0:
        in_specs.append(state_blk)
        args.append(state)
    aliases = {}
    if has_prev:
        aliases[len(args)] = 1
        in_specs.append(pl.BlockSpec(memory_space=pl.ANY))
        args.append(final_prev)
    out_specs = [pl.BlockSpec((seq_len, width), lambda b, h: (b, h))]
    out_shape = [jax.ShapeDtypeStruct((n_tok, BRANCH_W), BF16)]
    if want_final:
        out_specs.append(state_blk)
        out_shape.append(jax.ShapeDtypeStruct((n_seq, N_AB, 2, N_HEADS, HEAD_D, HEAD_D), F32))
    outs = pl.pallas_call(
        functools.partial(_hgrn_kernel, n_chunks=n_chunks, heads=heads, has_s0=has_s0,
                          has_prev=has_prev, want_final=want_final),
        grid=(n_seq, N_HEADS // heads),
        in_specs=in_specs,
        out_specs=out_specs,
        out_shape=out_shape,
        input_output_aliases=aliases,
        scratch_shapes=[
            pltpu.VMEM((heads, seq_len, HEAD_D), F32),
            pltpu.VMEM((heads, seq_len, 2 * HEAD_D), BF16),
            pltpu.VMEM((heads, n_chunks, HEAD_D, 2 * HEAD_D), F32),
            pltpu.VMEM((heads, n_chunks, 8, 2 * HEAD_D), F32),
            pltpu.VMEM((heads, n_chunks, HEAD_D, 2 * HEAD_D), BF16),
        ],
        compiler_params=_cparams(2, 48),
        name="hgrn_mixer",
    )(*args)
    return outs if want_final else (outs[0], None)


def kernel(x_prompt, x_sample, c, state_hgrn, c_ctx, w_ada, b_ada, g_pre, g_post, w_in_ab, w_out_ab,
           lb_logits, g_onorm_a, w_pool, pool_scale, w_in_c, w_out_c, ln_v_g, ln_v_b, w_spatial,
           b_spatial):
    n_batch, seq_p, _ = x_prompt.shape
    dec_batch, seq_s, _ = x_sample.shape
    tm_in, tm_out, tn, tm_c, pool_rows = 1024, 512, 1024, 512, 2048

    cond = jnp.concatenate(
        [c_ctx[None, :], c, jnp.zeros((N_COND_ROWS - 1 - dec_batch, D_MODEL), F32)], axis=0)
    mods = _modulations(cond, w_ada, b_ada).reshape(DEPTH, N_COND_ROWS, 3, D_MODEL)
    lb_all = _lower_bounds(lb_logits)

    flows = [
        dict(x=x_prompt.reshape(n_batch * seq_p, D_MODEL), seq=seq_p, grid=False, state=None,
             row=lambda tm: (lambda i: 0)),
        dict(x=x_sample.reshape(dec_batch * seq_s, D_MODEL), seq=seq_s, grid=True, state=state_hgrn,
             row=lambda tm: (lambda i: 1 + i // (seq_s // tm))),
    ]
    new_state = None
    for l in range(DEPTH):
        j = l // 2
        if l % 2 == 0:
            w = w_in_ab[j]
            w_in = jnp.concatenate([w[:, BRANCH_W:3 * BRANCH_W], w[:, :BRANCH_W], w[:, 3 * BRANCH_W:]],
                                   axis=1).astype(BF16)
            w_out = w_out_ab[j].astype(BF16)
            for fi, fl in enumerate(flows):
                gates, proj = _in_proj(fl["x"], mods[l], g_pre[l], w_in, fl["row"](tm_in), tm_in, tn,
                                       n_f32_tiles=AB_F32_PARTS * BRANCH_W // tn)
                heads = max(1, min(N_HEADS, 8 * HGRN_CHUNK // fl["seq"]))
                if fi == 0:
                    y_a, new_state = _hgrn_mixer(gates, proj, lb_all[j], g_onorm_a[j], None, j,
                                                 fl["seq"], heads, want_final=True,
                                                 final_prev=new_state)
                else:
                    y_a, _ = _hgrn_mixer(gates, proj, lb_all[j], g_onorm_a[j], fl["state"], j,
                                         fl["seq"], heads, want_final=False)
                y_b = _pool_mixer(proj, w_pool[j], pool_scale[j], fl["seq"], fl["grid"], pool_rows)
                fl["x"] = _out_proj([y_a, y_b], [w_out[:BRANCH_W], w_out[BRANCH_W:]], fl["x"],
                                    mods[l], g_post[l], fl["row"](tm_out), tm_out)
        else:
            w_in, w_out = w_in_c[j].astype(BF16), w_out_c[j].astype(BF16)
            for fl in flows:
                fl["x"] = _layer_c(fl["x"], mods[l], g_pre[l], g_post[l], w_in, w_out, ln_v_g[j],
                                   ln_v_b[j], w_spatial[j], b_spatial[j], fl["row"](tm_c), tm_c)
    y_p = flows[0]["x"].reshape(n_batch, seq_p, D_MODEL)
    y_s = flows[1]["x"].reshape(dec_batch, seq_s, D_MODEL)
    return (y_p, y_s, new_state)
```

```python
import functools

import numpy as np
import jax
import jax.numpy as jnp
from jax import lax
from jax.experimental import pallas as pl
from jax.experimental.pallas import tpu as pltpu

F32 = jnp.float32
BF16 = jnp.bfloat16

D_MODEL = 1024
DEPTH = 4
N_AB = (DEPTH + 1) // 2
MIX_WIDTH = 2 * D_MODEL
BRANCH_W = MIX_WIDTH // 2
HEAD_D = 128
N_HEADS = BRANCH_W // HEAD_D
GRID_W = 64
POOL_WINDOWS = (2, 4, 8, 16)
POOL_GROUP_W = BRANCH_W // len(POOL_WINDOWS)
CHUNK_C = 128
N_GROUPS_C = 8
GROUP_W_C = MIX_WIDTH // N_GROUPS_C
AB_IN = 7 * BRANCH_W
C_IN = 3 * MIX_WIDTH
AB_F32_PARTS = 2
AB_PART = {"q": 0, "i": 1, "gate_a": 2, "pool_in": 3, "gate_b": 4}
EPS = 1e-6
N_COND_ROWS = 8

HGRN_CHUNK = 128
HGRN_LEVELS = (64, 32, 16, 8)
HGRN_PSUM_LEVELS = (2, 4)
MIB = 1024 * 1024


def _cparams(n_axes, vmem_mib):
    return pltpu.CompilerParams(
        dimension_semantics=("arbitrary",) * n_axes, vmem_limit_bytes=int(vmem_mib * MIB))


def _resident(shape):
    zeros = (0,) * len(shape)
    return pl.BlockSpec(shape, lambda *_: zeros, pipeline_mode=pl.Buffered(1))


def _silu(x):
    return x * jax.nn.sigmoid(x)


def _rms(x, g):
    return x * lax.rsqrt(jnp.mean(x * x, axis=-1, keepdims=True) + EPS) * g


def _modulated_norm(x, mod, g):
    return _rms(x, g) * (1.0 + mod[1:2]) + mod[0:1]


def _mod_kernel(cond_ref, w_ref, b_ref, o_ref):
    a = _silu(cond_ref[...])
    o_ref[0] = jnp.dot(a, w_ref[0], preferred_element_type=F32,
                       precision=lax.Precision.HIGHEST) + b_ref[0]


def _modulations(cond, w_ada, b_ada):
    tn = 1024
    n3 = 3 * D_MODEL
    return pl.pallas_call(
        _mod_kernel,
        grid=(DEPTH, n3 // tn),
        in_specs=[
            pl.BlockSpec((N_COND_ROWS, D_MODEL), lambda l, j: (0, 0)),
            pl.BlockSpec((1, D_MODEL, tn), lambda l, j: (l, 0, j)),
            pl.BlockSpec((1, 1, tn), lambda l, j: (l, 0, j)),
        ],
        out_specs=pl.BlockSpec((1, N_COND_ROWS, tn), lambda l, j: (l, 0, j)),
        out_shape=jax.ShapeDtypeStruct((DEPTH, N_COND_ROWS, n3), F32),
        compiler_params=_cparams(2, 24),
        name="adaln_modulation",
    )(cond, w_ada, b_ada.reshape(DEPTH, 1, n3))


def _lb_kernel(x_ref, o_ref):
    x = x_ref[...]
    e = jnp.exp(x - jnp.max(x, axis=0, keepdims=True))
    p = e / jnp.sum(e, axis=0, keepdims=True)
    run = p[0]
    o_ref[0] = run - p[0]
    for l in range(1, x.shape[0]):
        run = run + p[l]
        o_ref[l] = run - p[0]


def _lower_bounds(lb_logits):
    return pl.pallas_call(
        _lb_kernel, out_shape=jax.ShapeDtypeStruct(lb_logits.shape, F32), name="hgrn_lower_bounds",
    )(lb_logits)


def _in_kernel(x_ref, mod_ref, g_ref, w_ref, *rest, n_f32_tiles):
    out_refs, h_ref = rest[:-1], rest[-1]
    j = pl.program_id(1)

    @pl.when(j == 0)
    def _():
        h_ref[...] = _modulated_norm(x_ref[...], mod_ref[0], g_ref[...]).astype(BF16)

    def project(o_ref):
        o_ref[...] = jnp.dot(h_ref[...], w_ref[...], preferred_element_type=F32).astype(o_ref.dtype)

    if n_f32_tiles:
        pl.when(j < n_f32_tiles)(lambda: project(out_refs[0]))
        pl.when(j >= n_f32_tiles)(lambda: project(out_refs[1]))
    else:
        project(out_refs[0])


def _in_proj(x, mod, g, w, row_of_tile, tm, tn, n_f32_tiles=0, w_tile_of=lambda j: j):
    n_tok, n_out = x.shape[0], w.shape[1]
    n_tiles = n_out // tn
    out_specs, out_shape = [], []
    if n_f32_tiles:
        out_specs.append(pl.BlockSpec((tm, tn), lambda i, j: (i, jnp.minimum(j, n_f32_tiles - 1))))
        out_shape.append(jax.ShapeDtypeStruct((n_tok, n_f32_tiles * tn), F32))
    out_specs.append(pl.BlockSpec((tm, tn), lambda i, j: (i, jnp.maximum(j - n_f32_tiles, 0))))
    out_shape.append(jax.ShapeDtypeStruct((n_tok, (n_tiles - n_f32_tiles) * tn), BF16))
    return pl.pallas_call(
        functools.partial(_in_kernel, n_f32_tiles=n_f32_tiles),
        grid=(n_tok // tm, n_tiles),
        in_specs=[
            pl.BlockSpec((tm, D_MODEL), lambda i, j: (i, 0)),
            pl.BlockSpec((1, 3, D_MODEL), lambda i, j: (row_of_tile(i), 0, 0)),
            pl.BlockSpec((1, D_MODEL), lambda i, j: (0, 0)),
            pl.BlockSpec((D_MODEL, tn), lambda i, j: (0, w_tile_of(j))),
        ],
        out_specs=out_specs,
        out_shape=out_shape,
        scratch_shapes=[pltpu.VMEM((tm, D_MODEL), BF16)],
        compiler_params=_cparams(2, 40),
        name="in_projection",
    )(x, mod, g.reshape(1, D_MODEL), w)


def _out_kernel(*refs, n_parts):
    y_refs, w_refs = refs[:n_parts], refs[n_parts:2 * n_parts]
    x_ref, mod_ref, g_ref, o_ref = refs[2 * n_parts:]
    acc = jnp.dot(y_refs[0][...], w_refs[0][...], preferred_element_type=F32)
    for y_ref, w_ref in zip(y_refs[1:], w_refs[1:]):
        acc = acc + jnp.dot(y_ref[...], w_ref[...], preferred_element_type=F32)
    o_ref[...] = x_ref[...] + mod_ref[0][2:3] * _rms(acc, g_ref[...])


def _out_proj(ys, w, x, mod, g, row_of_tile, tm):
    n_tok = x.shape[0]
    n_parts = len(ys)
    width = ys[0].shape[1]
    in_specs = [pl.BlockSpec((tm, width), lambda i: (i, 0)) for _ in ys]
    in_specs += [pl.BlockSpec((width, D_MODEL), lambda i, k=k: (k, 0)) for k in range(n_parts)]
    in_specs += [
        pl.BlockSpec((tm, D_MODEL), lambda i: (i, 0)),
        pl.BlockSpec((1, 3, D_MODEL), lambda i: (row_of_tile(i), 0, 0)),
        pl.BlockSpec((1, D_MODEL), lambda i: (0, 0)),
    ]
    return pl.pallas_call(
        functools.partial(_out_kernel, n_parts=n_parts),
        grid=(n_tok // tm,),
        in_specs=in_specs,
        out_specs=pl.BlockSpec((tm, D_MODEL), lambda i: (i, 0)),
        out_shape=jax.ShapeDtypeStruct((n_tok, D_MODEL), F32),
        compiler_params=_cparams(1, 40),
        name="out_projection",
    )(*ys, *([w] * n_parts), x, mod, g.reshape(1, D_MODEL))


def _layer_c_kernel(x_ref, mod_ref, gpre_ref, win_ref, lng_ref, lnb_ref, ws_ref, bs_ref, wout_ref,
                    gpost_ref, o_ref, y_scr, *, tm):
    x = x_ref[...]
    mod = mod_ref[0]
    h = _modulated_norm(x, mod, gpre_ref[...]).astype(BF16)

    v = jnp.dot(h, win_ref[:, MIX_WIDTH:2 * MIX_WIDTH], preferred_element_type=F32)
    mu = jnp.mean(v, axis=-1, keepdims=True)
    vc = v - mu
    var = jnp.mean(vc * vc, axis=-1, keepdims=True)
    vn = (vc * lax.rsqrt(var + EPS) * lng_ref[...] + lnb_ref[...]).astype(BF16)

    for gi in range(N_GROUPS_C):
        cols = slice(gi * GROUP_W_C, (gi + 1) * GROUP_W_C)
        u = jnp.dot(h, win_ref[:, cols], preferred_element_type=F32)
        gate = _silu(jnp.dot(h, win_ref[:, 2 * MIX_WIDTH + gi * GROUP_W_C:
                                        2 * MIX_WIDTH + (gi + 1) * GROUP_W_C],
                             preferred_element_type=F32))
        for n in range(tm // CHUNK_C):
            rows = slice(n * CHUNK_C, (n + 1) * CHUNK_C)
            sp = jnp.dot(ws_ref[gi], vn[rows, cols], preferred_element_type=F32) + bs_ref[:, gi:gi + 1]
            y_scr[rows, cols] = (u[rows] * sp * gate[rows]).astype(BF16)

    acc = jnp.dot(y_scr[...], wout_ref[...], preferred_element_type=F32)
    o_ref[...] = x + mod[2:3] * _rms(acc, gpost_ref[...])


def _layer_c(x, mod, g_pre, g_post, w_in, w_out, ln_g, ln_b, w_s, b_s, row_of_tile, tm):
    n_tok = x.shape[0]
    return pl.pallas_call(
        functools.partial(_layer_c_kernel, tm=tm),
        grid=(n_tok // tm,),
        in_specs=[
            pl.BlockSpec((tm, D_MODEL), lambda i: (i, 0)),
            pl.BlockSpec((1, 3, D_MODEL), lambda i: (row_of_tile(i), 0, 0)),
            _resident((1, D_MODEL)),
            _resident((D_MODEL, C_IN)),
            _resident((1, MIX_WIDTH)),
            _resident((1, MIX_WIDTH)),
            _resident((N_GROUPS_C, CHUNK_C, CHUNK_C)),
            _resident((CHUNK_C, N_GROUPS_C)),
            _resident((MIX_WIDTH, D_MODEL)),
            _resident((1, D_MODEL)),
        ],
        out_specs=pl.BlockSpec((tm, D_MODEL), lambda i: (i, 0)),
        out_shape=jax.ShapeDtypeStruct((n_tok, D_MODEL), F32),
        scratch_shapes=[pltpu.VMEM((tm, MIX_WIDTH), BF16)],
        compiler_params=_cparams(1, 48),
        name="gmlp_layer",
    )(x, mod, g_pre.reshape(1, D_MODEL), w_in, ln_g.reshape(1, MIX_WIDTH), ln_b.reshape(1, MIX_WIDTH),
      w_s.astype(BF16), b_s.T, w_out, g_post.reshape(1, D_MODEL))


def _window_counts(n, w):
    return [min(i - w // 2 + w, n) - max(i - w // 2, 0) for i in range(n)]


def _mean_minor(x3, w):
    g, n, ch = x3.shape
    idx = lax.broadcasted_iota(jnp.int32, (1, n, ch), 1)

    def shifted(a, k):
        return pltpu.roll(a.reshape(g * n, ch), k % (g * n), 0).reshape(g, n, ch)

    trail, lead = x3, x3
    size = 1
    while size < w // 2:
        trail = trail + jnp.where(idx >= size, shifted(trail, size), 0.0)
        lead = lead + jnp.where(idx + size < n, shifted(lead, -size), 0.0)
        size *= 2
    s = jnp.where(idx >= 1, shifted(trail, 1), 0.0) + lead
    half = w // 2
    cnt = jnp.minimum(idx - half + w, n) - jnp.maximum(idx - half, 0)
    return s * (1.0 / cnt.astype(F32))


def _mean_major(x3, w):
    n = x3.shape[0]
    zeros = lambda k: jnp.zeros((k,) + x3.shape[1:], F32)
    back = lambda a, k: jnp.concatenate([zeros(k), a[:n - k]], axis=0)
    ahead = lambda a, k: jnp.concatenate([a[k:], zeros(k)], axis=0)
    trail, lead = x3, x3
    size = 1
    while size < w // 2:
        trail = trail + back(trail, size)
        lead = lead + ahead(lead, size)
        size *= 2
    s = back(trail, 1) + lead
    return jnp.concatenate(
        [s[i:i + 1] * (1.0 / cnt) for i, cnt in enumerate(_window_counts(n, w))], axis=0)


def _pool_kernel(p_ref, gate_ref, wp_ref, ps_ref, y_ref, *, seq_len, grid_mode):
    grp = pl.program_id(1)
    rows, ch = p_ref.shape
    for k, w in enumerate(POOL_WINDOWS):
        @pl.when(grp == k)
        def _(w=w):
            x = p_ref[...].astype(F32)
            if grid_mode:
                m = _mean_major(x.reshape(rows // GRID_W, GRID_W, ch), w)
                m = _mean_minor(m, w).reshape(rows, ch)
            else:
                m = _mean_minor(x.reshape(rows // seq_len, seq_len, ch), w).reshape(rows, ch)
            dlt = (m - x).astype(BF16)
            y = jnp.dot(dlt, wp_ref[0], preferred_element_type=F32) * ps_ref[...]
            y_ref[...] = (y * _silu(gate_ref[...].astype(F32))).astype(BF16)


def _pool_mixer(proj, w_pool, pool_scale, seq_len, grid_mode, rows_per_step):
    n_tok = proj.shape[0]
    n_grp = len(POOL_WINDOWS)
    p_blk0 = AB_PART["pool_in"] * BRANCH_W // POOL_GROUP_W
    g_blk0 = AB_PART["gate_b"] * BRANCH_W // POOL_GROUP_W
    return pl.pallas_call(
        functools.partial(_pool_kernel, seq_len=seq_len, grid_mode=grid_mode),
        grid=(n_tok // rows_per_step, n_grp),
        in_specs=[
            pl.BlockSpec((rows_per_step, POOL_GROUP_W), lambda b, g: (b, p_blk0 + g)),
            pl.BlockSpec((rows_per_step, POOL_GROUP_W), lambda b, g: (b, g_blk0 + g)),
            pl.BlockSpec((1, POOL_GROUP_W, POOL_GROUP_W), lambda b, g: (g, 0, 0)),
            pl.BlockSpec((1, POOL_GROUP_W), lambda b, g: (0, g)),
        ],
        out_specs=pl.BlockSpec((rows_per_step, POOL_GROUP_W), lambda b, g: (b, g)),
        out_shape=jax.ShapeDtypeStruct((n_tok, BRANCH_W), BF16),
        compiler_params=_cparams(2, 56),
        name="pool_mixer",
    )(proj, proj, w_pool.astype(BF16), pool_scale.reshape(1, BRANCH_W))


def _hgrn_triangles():
    ones = np.ones((HGRN_CHUNK, HGRN_CHUNK), np.float32)
    return np.concatenate([np.tril(ones), np.triu(ones)], axis=0)


def _hgrn_partial_sum_matrix():
    c = HGRN_CHUNK
    blocks = []
    for m in HGRN_PSUM_LEVELS:
        mq = np.zeros((c, 2 * c), np.float32)
        mk = np.zeros((c, 2 * c), np.float32)
        for i in range(c):
            start = (i // (2 * m)) * (2 * m)
            r1, r2 = start + m - 1, start + m
            if i >= r2:
                mq[i, r2:i + 1] = 1
                mk[i, c + r2:c + i] = 1
            else:
                mq[i, c + i:c + r1 + 1] = 1
                mk[i, i + 1:r1 + 1] = 1
        blocks += [mq, mk]
    return np.concatenate(blocks, axis=0)


def _hgrn_level_index():
    i = np.arange(HGRN_CHUNK)
    x = i[:, None] ^ i[None, :]
    lv = np.zeros_like(x)
    nz = x > 0
    lv[nz] = np.floor(np.log2(x[nz])).astype(x.dtype) + 1
    return lv.astype(np.int32)


_HGRN_TRI = _hgrn_triangles()
_HGRN_PSUM = _hgrn_partial_sum_matrix()
_HGRN_LEVEL = _hgrn_level_index()

_NT = (((1,), (1,)), ((), ()))
_TN = (((0,), (0,)), ((), ()))


def _hgrn_kernel(*refs, n_chunks, heads, has_s0, has_prev, want_final, final_own):
    q_ref, ff_ref, fb_ref, v_ref, ga_ref, lb_ref, gon_ref, tri_ref, psum_ref, lvl_ref = refs[:10]
    pos = 10
    s0_ref = sfin_ref = None
    if has_s0:
        s0_ref = refs[pos]
        pos += 1
    if has_prev:
        pos += 1
    y_ref = refs[pos]
    pos += 1
    if want_final:
        sfin_ref = refs[pos]
        pos += 1
    o_scr, qfb_scr, dst_scr, dec_scr, st_scr = refs[pos:]

    c = HGRN_CHUNK
    inline = heads * n_chunks <= 8
    level = lvl_ref[...]
    row8 = jnp.concatenate([lax.broadcasted_iota(jnp.int32, (8, HEAD_D), 0)] * (c // 8), axis=0)

    def chunk_rows(ci):
        if isinstance(ci, int):
            return pl.ds(ci * c, c)
        return pl.ds(pl.multiple_of(ci * c, c), c)

    def over_chunks(body, init, unroll):
        if inline:
            carry = init
            for ci in range(n_chunks):
                carry = body(ci, carry)
            return carry
        return lax.fori_loop(0, n_chunks, body, init, unroll=unroll)

    def over_heads_and_chunks(fn, in_flight):
        def body(ci, carry):
            for hh in range(heads):
                fn(hh, ci, carry)
            return carry
        over_chunks(body, 0, unroll=max(1, min(n_chunks, in_flight // heads)))

    def hi_mid(x):
        hi = x.astype(BF16)
        return hi, (x - hi.astype(F32)).astype(BF16)

    def summed(zero_one, hi, mid):
        dd = jnp.dot(zero_one, jnp.concatenate([hi, mid], axis=1), preferred_element_type=F32)
        return dd[:, :HEAD_D] + dd[:, HEAD_D:]

    def local_pass(hh, ci, carry):
        rows = chunk_rows(ci)
        lanes = slice(hh * HEAD_D, (hh + 1) * HEAD_D)
        lb_f, lb_b = lb_ref[0:1, lanes], lb_ref[1:2, lanes]
        f_f = lb_f + (1.0 - lb_f) * jax.nn.sigmoid(ff_ref[rows, lanes])
        f_b = lb_b + (1.0 - lb_b) * jax.nn.sigmoid(fb_ref[rows, lanes])
        k_f, k_b = 1.0 - f_f, 1.0 - f_b
        hi_f, mid_f = hi_mid(jnp.log2(f_f))
        hi_b, mid_b = hi_mid(jnp.log2(f_b))
        b_f = summed(tri_ref[0:c, :], hi_f, mid_f)
        b_b = summed(tri_ref[c:2 * c, :], hi_b, mid_b)
        small = summed(psum_ref[...], jnp.concatenate([hi_f, hi_b], axis=0),
                       jnp.concatenate([mid_f, mid_b], axis=0))

        q = _silu(q_ref[rows, lanes].astype(F32))
        v = v_ref[rows, lanes]

        def level_scores(d_q, d_k, k_sel):
            qt = (q * jnp.exp2(d_q)).astype(BF16)
            kt = (k_sel * jnp.exp2(d_k)).astype(BF16)
            return lax.dot_general(qt, kt, _NT, preferred_element_type=F32)

        s = lax.dot_general(q.astype(BF16), (k_f + k_b).astype(BF16), _NT,
                            preferred_element_type=F32)
        s = jnp.where(level == 0, s, 0.0)
        odd = (row8 & 1) == 1
        q1 = (q * jnp.where(odd, f_f, f_b)).astype(BF16)
        k1 = jnp.where(odd, k_b, k_f).astype(BF16)
        s = jnp.where(level == 1, lax.dot_general(q1, k1, _NT, preferred_element_type=F32), s)
        for li, m in enumerate(HGRN_PSUM_LEVELS):
            second = (row8 & m) == m
            s_l = level_scores(small[2 * li * c:(2 * li + 1) * c], small[(2 * li + 1) * c:(2 * li + 2) * c],
                               jnp.where(second, k_b, k_f))
            s = jnp.where(level == m.bit_length(), s_l, s)
        for m in HGRN_LEVELS:
            dq_parts, dk_parts, k_parts = [], [], []
            for lo in range(0, c, 2 * m):
                mid_row, hi_row = lo + m, lo + 2 * m
                edge_f = b_f[mid_row - 1:mid_row]
                edge_b = b_b[mid_row:mid_row + 1]
                dq_parts += [b_b[lo:mid_row] - edge_b, b_f[mid_row:hi_row] - edge_f]
                dk_parts += [edge_f - b_f[lo:mid_row], edge_b - b_b[mid_row:hi_row]]
                k_parts += [k_f[lo:mid_row], k_b[mid_row:hi_row]]
            s_l = level_scores(jnp.concatenate(dq_parts, axis=0), jnp.concatenate(dk_parts, axis=0),
                               jnp.concatenate(k_parts, axis=0))
            s = jnp.where(level == m.bit_length(), s_l, s)

        q_fb = jnp.concatenate([q * jnp.exp2(b_f), q * jnp.exp2(b_b)], axis=1).astype(BF16)
        k_fb = jnp.concatenate([k_f * jnp.exp2(b_f[c - 1:c] - b_f),
                                k_b * jnp.exp2(b_b[0:1] - b_b)], axis=1).astype(BF16)
        dec = jnp.concatenate([jnp.exp2(b_f[c - 1:c]), jnp.exp2(b_b[0:1])], axis=1)

        o_scr[hh, rows, :] = jnp.dot(s.astype(BF16), v, preferred_element_type=F32)
        qfb_scr[hh, rows, :] = q_fb
        dst_scr[hh, ci] = lax.dot_general(v, k_fb, _TN, preferred_element_type=F32)
        dec_scr[hh, ci] = jnp.broadcast_to(dec, (8, 2 * HEAD_D))
        return carry

    over_heads_and_chunks(local_pass, 8)

    for hh in range(heads):
        if has_s0:
            st_f0 = s0_ref[0, 0, 0, hh].T
            st_b0 = s0_ref[0, 0, 1, hh].T
        else:
            st_f0 = jnp.zeros((HEAD_D, HEAD_D), F32)
            st_b0 = jnp.zeros((HEAD_D, HEAD_D), F32)

        def fwd_scan(ci, st, hh=hh):
            st_scr[hh, ci, :, 0:HEAD_D] = st.astype(BF16)
            return st * dec_scr[hh, ci, 0:1, 0:HEAD_D] + dst_scr[hh, ci, :, 0:HEAD_D]

        def bwd_scan(k, st, hh=hh):
            ci = n_chunks - 1 - k
            st_scr[hh, ci, :, HEAD_D:2 * HEAD_D] = st.astype(BF16)
            return (st * dec_scr[hh, ci, 0:1, HEAD_D:2 * HEAD_D]
                    + dst_scr[hh, ci, :, HEAD_D:2 * HEAD_D])

        st_f = over_chunks(fwd_scan, st_f0, unroll=1)
        st_b = over_chunks(bwd_scan, st_b0, unroll=1)
        if want_final:
            sfin_ref[0, final_own, 0, hh] = st_f.T
            sfin_ref[0, final_own, 1, hh] = st_b.T
            for other in range(sfin_ref.shape[1]):
                if other != final_own:
                    sfin_ref[0, other, :, hh] = jnp.zeros((2, HEAD_D, HEAD_D), F32)

    def output_pass(hh, ci, carry):
        rows = chunk_rows(ci)
        lanes = slice(hh * HEAD_D, (hh + 1) * HEAD_D)
        o = o_scr[hh, rows, :] + lax.dot_general(qfb_scr[hh, rows, :], st_scr[hh, ci], _NT,
                                                 preferred_element_type=F32)
        y_ref[rows, lanes] = (_rms(o, gon_ref[...])
                              * _silu(ga_ref[rows, lanes].astype(F32))).astype(BF16)
        return carry

    over_heads_and_chunks(output_pass, 8)


def _hgrn_mixer(gates, proj, lb, g_onorm, state, layer_j, seq_len, heads, want_final,
                final_prev=None):
    n_tok = proj.shape[0]
    n_seq = n_tok // seq_len
    n_chunks = seq_len // HGRN_CHUNK
    has_s0 = state is not None
    has_prev = final_prev is not None
    width = heads * HEAD_D
    blk = lambda part: pl.BlockSpec(
        (seq_len, width), lambda b, h: (b, part * (N_HEADS // heads) + h))
    in_specs = [blk(AB_PART["q"]), blk(0), blk(1), blk(AB_PART["i"]), blk(AB_PART["gate_a"]),
                pl.BlockSpec((2, width), lambda b, h: (0, h)),
                pl.BlockSpec((1, HEAD_D), lambda b, h: (0, 0)),
                pl.BlockSpec(_HGRN_TRI.shape, lambda b, h: (0, 0)),
                pl.BlockSpec(_HGRN_PSUM.shape, lambda b, h: (0, 0)),
                pl.BlockSpec(_HGRN_LEVEL.shape, lambda b, h: (0, 0))]
    args = [proj, gates, gates, proj, proj, lb, g_onorm.reshape(1, HEAD_D),
            jnp.asarray(_HGRN_TRI, BF16), jnp.asarray(_HGRN_PSUM, BF16), jnp.asarray(_HGRN_LEVEL)]
    state_blk = pl.BlockSpec((1, 1, 2, heads, HEAD_D, HEAD_D), lambda b, h: (b, layer_j, 0, h, 0, 0))
    if has_s0:
        in_specs.append(state_blk)
        args.append(state)
    aliases = {}
    if has_prev:
        aliases[len(args)] = 1
        in_specs.append(pl.BlockSpec(memory_space=pl.ANY))
        args.append(final_prev)
    out_specs = [pl.BlockSpec((seq_len, width), lambda b, h: (b, h))]
    out_shape = [jax.ShapeDtypeStruct((n_tok, BRANCH_W), BF16)]
    if want_final:
        out_specs.append(state_blk if has_prev else pl.BlockSpec(
            (1, N_AB, 2, heads, HEAD_D, HEAD_D), lambda b, h: (b, 0, 0, h, 0, 0)))
        out_shape.append(jax.ShapeDtypeStruct((n_seq, N_AB, 2, N_HEADS, HEAD_D, HEAD_D), F32))
    outs = pl.pallas_call(
        functools.partial(_hgrn_kernel, n_chunks=n_chunks, heads=heads, has_s0=has_s0,
                          has_prev=has_prev, want_final=want_final,
                          final_own=0 if has_prev else layer_j),
        grid=(n_seq, N_HEADS // heads),
        in_specs=in_specs,
        out_specs=out_specs,
        out_shape=out_shape,
        input_output_aliases=aliases,
        scratch_shapes=[
            pltpu.VMEM((heads, seq_len, HEAD_D), F32),
            pltpu.VMEM((heads, seq_len, 2 * HEAD_D), BF16),
            pltpu.VMEM((heads, n_chunks, HEAD_D, 2 * HEAD_D), F32),
            pltpu.VMEM((heads, n_chunks, 8, 2 * HEAD_D), F32),
            pltpu.VMEM((heads, n_chunks, HEAD_D, 2 * HEAD_D), BF16),
        ],
        compiler_params=_cparams(2, 48),
        name="hgrn_mixer",
    )(*args)
    return outs if want_final else (outs[0], None)


def kernel(x_prompt, x_sample, c, state_hgrn, c_ctx, w_ada, b_ada, g_pre, g_post, w_in_ab, w_out_ab,
           lb_logits, g_onorm_a, w_pool, pool_scale, w_in_c, w_out_c, ln_v_g, ln_v_b, w_spatial,
           b_spatial):
    n_batch, seq_p, _ = x_prompt.shape
    dec_batch, seq_s, _ = x_sample.shape
    tm_in, tm_out, tn, tm_c, pool_rows = 1024, 512, BRANCH_W, 512, 2048

    cond = jnp.concatenate(
        [c_ctx[None, :], c, jnp.zeros((N_COND_ROWS - 1 - dec_batch, D_MODEL), F32)], axis=0)
    mods = _modulations(cond, w_ada, b_ada).reshape(DEPTH, N_COND_ROWS, 3, D_MODEL)
    lb_all = _lower_bounds(lb_logits)

    flows = [
        dict(x=x_prompt.reshape(n_batch * seq_p, D_MODEL), seq=seq_p, grid=False, state=None,
             row=lambda tm: (lambda i: 0)),
        dict(x=x_sample.reshape(dec_batch * seq_s, D_MODEL), seq=seq_s, grid=True, state=state_hgrn,
             row=lambda tm: (lambda i: 1 + i // (seq_s // tm))),
    ]
    new_state = None
    for l in range(DEPTH):
        j = l // 2
        if l % 2 == 0:
            w_in, w_out = w_in_ab[j].astype(BF16), w_out_ab[j].astype(BF16)
            reorder = lambda t: jnp.where(t < AB_F32_PARTS, t + 1, jnp.where(t == AB_F32_PARTS, 0, t))
            for fi, fl in enumerate(flows):
                gates, proj = _in_proj(fl["x"], mods[l], g_pre[l], w_in, fl["row"](tm_in), tm_in, tn,
                                       n_f32_tiles=AB_F32_PARTS, w_tile_of=reorder)
                heads = max(2, min(N_HEADS, 8 * HGRN_CHUNK // fl["seq"]))
                if fi == 0:
                    y_a, new_state = _hgrn_mixer(gates, proj, lb_all[j], g_onorm_a[j], None, j,
                                                 fl["seq"], heads, want_final=True,
                                                 final_prev=new_state)
                else:
                    y_a, _ = _hgrn_mixer(gates, proj, lb_all[j], g_onorm_a[j], fl["state"], j,
                                         fl["seq"], heads, want_final=False)
                y_b = _pool_mixer(proj, w_pool[j], pool_scale[j], fl["seq"], fl["grid"], pool_rows)
                fl["x"] = _out_proj([y_a, y_b], w_out, fl["x"], mods[l], g_post[l],
                                    fl["row"](tm_out), tm_out)
        else:
            w_in, w_out = w_in_c[j].astype(BF16), w_out_c[j].astype(BF16)
            for fl in flows:
                fl["x"] = _layer_c(fl["x"], mods[l], g_pre[l], g_post[l], w_in, w_out, ln_v_g[j],
                                   ln_v_b[j], w_spatial[j], b_spatial[j], fl["row"](tm_c), tm_c)
    y_p = flows[0]["x"].reshape(n_batch, seq_p, D_MODEL)
    y_s = flows[1]["x"].reshape(dec_batch, seq_s, D_MODEL)
    return (y_p, y_s, new_state)
```

```python
import functools

import numpy as np
import jax
import jax.numpy as jnp
from jax import lax
from jax.experimental import pallas as pl
from jax.experimental.pallas import tpu as pltpu

F32 = jnp.float32
BF16 = jnp.bfloat16

D_MODEL = 1024
DEPTH = 4
N_AB = (DEPTH + 1) // 2
MIX_WIDTH = 2 * D_MODEL
BRANCH_W = MIX_WIDTH // 2
HEAD_D = 128
N_HEADS = BRANCH_W // HEAD_D
GRID_W = 64
POOL_WINDOWS = (2, 4, 8, 16)
POOL_GROUP_W = BRANCH_W // len(POOL_WINDOWS)
CHUNK_C = 128
N_GROUPS_C = 8
GROUP_W_C = MIX_WIDTH // N_GROUPS_C
AB_IN = 7 * BRANCH_W
C_IN = 3 * MIX_WIDTH
AB_F32_PARTS = 2
AB_PART = {"q": 0, "i": 1, "gate_a": 2, "pool_in": 3, "gate_b": 4}
EPS = 1e-6
N_COND_ROWS = 8

HGRN_CHUNK = 128
HGRN_LEVELS = (64, 32, 16, 8)
HGRN_PSUM_LEVELS = (2, 4)
MIB = 1024 * 1024


def _cparams(n_axes, vmem_mib):
    return pltpu.CompilerParams(
        dimension_semantics=("arbitrary",) * n_axes, vmem_limit_bytes=int(vmem_mib * MIB))


def _resident(shape):
    zeros = (0,) * len(shape)
    return pl.BlockSpec(shape, lambda *_: zeros, pipeline_mode=pl.Buffered(1))


def _silu(x):
    return x * jax.nn.sigmoid(x)


def _rms(x, g):
    return x * lax.rsqrt(jnp.mean(x * x, axis=-1, keepdims=True) + EPS) * g


def _modulated_norm(x, mod, g):
    return _rms(x, g) * (1.0 + mod[1:2]) + mod[0:1]


def _mod_kernel(cond_ref, w_ref, b_ref, o_ref):
    a = _silu(cond_ref[...])
    o_ref[0] = jnp.dot(a, w_ref[0], preferred_element_type=F32,
                       precision=lax.Precision.HIGHEST) + b_ref[0]


def _modulations(cond, w_ada, b_ada):
    tn = 1024
    n3 = 3 * D_MODEL
    return pl.pallas_call(
        _mod_kernel,
        grid=(DEPTH, n3 // tn),
        in_specs=[
            pl.BlockSpec((N_COND_ROWS, D_MODEL), lambda l, j: (0, 0)),
            pl.BlockSpec((1, D_MODEL, tn), lambda l, j: (l, 0, j)),
            pl.BlockSpec((1, 1, tn), lambda l, j: (l, 0, j)),
        ],
        out_specs=pl.BlockSpec((1, N_COND_ROWS, tn), lambda l, j: (l, 0, j)),
        out_shape=jax.ShapeDtypeStruct((DEPTH, N_COND_ROWS, n3), F32),
        compiler_params=_cparams(2, 24),
        name="adaln_modulation",
    )(cond, w_ada, b_ada.reshape(DEPTH, 1, n3))


def _lb_kernel(x_ref, o_ref):
    x = x_ref[...]
    e = jnp.exp(x - jnp.max(x, axis=0, keepdims=True))
    p = e / jnp.sum(e, axis=0, keepdims=True)
    run = p[0]
    o_ref[0] = run - p[0]
    for l in range(1, x.shape[0]):
        run = run + p[l]
        o_ref[l] = run - p[0]


def _lower_bounds(lb_logits):
    return pl.pallas_call(
        _lb_kernel, out_shape=jax.ShapeDtypeStruct(lb_logits.shape, F32), name="hgrn_lower_bounds",
    )(lb_logits)


def _in_kernel(x_ref, mod_ref, g_ref, w_ref, *rest, n_f32_tiles):
    out_refs, h_ref = rest[:-1], rest[-1]
    j = pl.program_id(1)

    @pl.when(j == 0)
    def _():
        h_ref[...] = _modulated_norm(x_ref[...], mod_ref[0], g_ref[...]).astype(BF16)

    def project(o_ref):
        o_ref[...] = jnp.dot(h_ref[...], w_ref[...], preferred_element_type=F32).astype(o_ref.dtype)

    if n_f32_tiles:
        pl.when(j < n_f32_tiles)(lambda: project(out_refs[0]))
        pl.when(j >= n_f32_tiles)(lambda: project(out_refs[1]))
    else:
        project(out_refs[0])


def _in_proj(x, mod, g, w, row_of_tile, tm, tn, n_f32_tiles=0, w_tile_of=lambda j: j):
    n_tok, n_out = x.shape[0], w.shape[1]
    n_tiles = n_out // tn
    out_specs, out_shape = [], []
    if n_f32_tiles:
        out_specs.append(pl.BlockSpec((tm, tn), lambda i, j: (i, jnp.minimum(j, n_f32_tiles - 1))))
        out_shape.append(jax.ShapeDtypeStruct((n_tok, n_f32_tiles * tn), F32))
    out_specs.append(pl.BlockSpec((tm, tn), lambda i, j: (i, jnp.maximum(j - n_f32_tiles, 0))))
    out_shape.append(jax.ShapeDtypeStruct((n_tok, (n_tiles - n_f32_tiles) * tn), BF16))
    return pl.pallas_call(
        functools.partial(_in_kernel, n_f32_tiles=n_f32_tiles),
        grid=(n_tok // tm, n_tiles),
        in_specs=[
            pl.BlockSpec((tm, D_MODEL), lambda i, j: (i, 0)),
            pl.BlockSpec((1, 3, D_MODEL), lambda i, j: (row_of_tile(i), 0, 0)),
            pl.BlockSpec((1, D_MODEL), lambda i, j: (0, 0)),
            pl.BlockSpec((D_MODEL, tn), lambda i, j: (0, w_tile_of(j))),
        ],
        out_specs=out_specs,
        out_shape=out_shape,
        scratch_shapes=[pltpu.VMEM((tm, D_MODEL), BF16)],
        compiler_params=_cparams(2, 40),
        name="in_projection",
    )(x, mod, g.reshape(1, D_MODEL), w)


def _out_kernel(*refs, n_parts):
    y_refs, w_refs = refs[:n_parts], refs[n_parts:2 * n_parts]
    x_ref, mod_ref, g_ref, o_ref = refs[2 * n_parts:]
    acc = jnp.dot(y_refs[0][...], w_refs[0][...], preferred_element_type=F32)
    for y_ref, w_ref in zip(y_refs[1:], w_refs[1:]):
        acc = acc + jnp.dot(y_ref[...], w_ref[...], preferred_element_type=F32)
    o_ref[...] = x_ref[...] + mod_ref[0][2:3] * _rms(acc, g_ref[...])


def _out_proj(ys, w, x, mod, g, row_of_tile, tm):
    n_tok = x.shape[0]
    n_parts = len(ys)
    width = ys[0].shape[1]
    in_specs = [pl.BlockSpec((tm, width), lambda i: (i, 0)) for _ in ys]
    in_specs += [pl.BlockSpec((width, D_MODEL), lambda i, k=k: (k, 0)) for k in range(n_parts)]
    in_specs += [
        pl.BlockSpec((tm, D_MODEL), lambda i: (i, 0)),
        pl.BlockSpec((1, 3, D_MODEL), lambda i: (row_of_tile(i), 0, 0)),
        pl.BlockSpec((1, D_MODEL), lambda i: (0, 0)),
    ]
    return pl.pallas_call(
        functools.partial(_out_kernel, n_parts=n_parts),
        grid=(n_tok // tm,),
        in_specs=in_specs,
        out_specs=pl.BlockSpec((tm, D_MODEL), lambda i: (i, 0)),
        out_shape=jax.ShapeDtypeStruct((n_tok, D_MODEL), F32),
        compiler_params=_cparams(1, 40),
        name="out_projection",
    )(*ys, *([w] * n_parts), x, mod, g.reshape(1, D_MODEL))


def _layer_c_kernel(x_ref, mod_ref, gpre_ref, win_ref, lng_ref, lnb_ref, ws_ref, bs_ref, wout_ref,
                    gpost_ref, o_ref, y_scr, *, tm):
    x = x_ref[...]
    mod = mod_ref[0]
    h = _modulated_norm(x, mod, gpre_ref[...]).astype(BF16)

    v = jnp.dot(h, win_ref[:, MIX_WIDTH:2 * MIX_WIDTH], preferred_element_type=F32)
    mu = jnp.mean(v, axis=-1, keepdims=True)
    vc = v - mu
    var = jnp.mean(vc * vc, axis=-1, keepdims=True)
    vn = (vc * lax.rsqrt(var + EPS) * lng_ref[...] + lnb_ref[...]).astype(BF16)

    for gi in range(N_GROUPS_C):
        cols = slice(gi * GROUP_W_C, (gi + 1) * GROUP_W_C)
        u = jnp.dot(h, win_ref[:, cols], preferred_element_type=F32)
        gate = _silu(jnp.dot(h, win_ref[:, 2 * MIX_WIDTH + gi * GROUP_W_C:
                                        2 * MIX_WIDTH + (gi + 1) * GROUP_W_C],
                             preferred_element_type=F32))
        for n in range(tm // CHUNK_C):
            rows = slice(n * CHUNK_C, (n + 1) * CHUNK_C)
            sp = jnp.dot(ws_ref[gi], vn[rows, cols], preferred_element_type=F32) + bs_ref[:, gi:gi + 1]
            y_scr[rows, cols] = (u[rows] * sp * gate[rows]).astype(BF16)

    acc = jnp.dot(y_scr[...], wout_ref[...], preferred_element_type=F32)
    o_ref[...] = x + mod[2:3] * _rms(acc, gpost_ref[...])


def _layer_c(x, mod, g_pre, g_post, w_in, w_out, ln_g, ln_b, w_s, b_s, row_of_tile, tm):
    n_tok = x.shape[0]
    return pl.pallas_call(
        functools.partial(_layer_c_kernel, tm=tm),
        grid=(n_tok // tm,),
        in_specs=[
            pl.BlockSpec((tm, D_MODEL), lambda i: (i, 0)),
            pl.BlockSpec((1, 3, D_MODEL), lambda i: (row_of_tile(i), 0, 0)),
            _resident((1, D_MODEL)),
            _resident((D_MODEL, C_IN)),
            _resident((1, MIX_WIDTH)),
            _resident((1, MIX_WIDTH)),
            _resident((N_GROUPS_C, CHUNK_C, CHUNK_C)),
            _resident((CHUNK_C, N_GROUPS_C)),
            _resident((MIX_WIDTH, D_MODEL)),
            _resident((1, D_MODEL)),
        ],
        out_specs=pl.BlockSpec((tm, D_MODEL), lambda i: (i, 0)),
        out_shape=jax.ShapeDtypeStruct((n_tok, D_MODEL), F32),
        scratch_shapes=[pltpu.VMEM((tm, MIX_WIDTH), BF16)],
        compiler_params=_cparams(1, 48),
        name="gmlp_layer",
    )(x, mod, g_pre.reshape(1, D_MODEL), w_in, ln_g.reshape(1, MIX_WIDTH), ln_b.reshape(1, MIX_WIDTH),
      w_s.astype(BF16), b_s.T, w_out, g_post.reshape(1, D_MODEL))


def _window_counts(n, w):
    return [min(i - w // 2 + w, n) - max(i - w // 2, 0) for i in range(n)]


def _mean_minor(x3, w):
    g, n, ch = x3.shape
    idx = lax.broadcasted_iota(jnp.int32, (1, n, ch), 1)

    def shifted(a, k):
        return pltpu.roll(a.reshape(g * n, ch), k % (g * n), 0).reshape(g, n, ch)

    trail, lead = x3, x3
    size = 1
    while size < w // 2:
        trail = trail + jnp.where(idx >= size, shifted(trail, size), 0.0)
        lead = lead + jnp.where(idx + size < n, shifted(lead, -size), 0.0)
        size *= 2
    s = jnp.where(idx >= 1, shifted(trail, 1), 0.0) + lead
    half = w // 2
    cnt = jnp.minimum(idx - half + w, n) - jnp.maximum(idx - half, 0)
    return s * (1.0 / cnt.astype(F32))


def _mean_major(x3, w):
    n = x3.shape[0]
    zeros = lambda k: jnp.zeros((k,) + x3.shape[1:], F32)
    back = lambda a, k: jnp.concatenate([zeros(k), a[:n - k]], axis=0)
    ahead = lambda a, k: jnp.concatenate([a[k:], zeros(k)], axis=0)
    trail, lead = x3, x3
    size = 1
    while size < w // 2:
        trail = trail + back(trail, size)
        lead = lead + ahead(lead, size)
        size *= 2
    s = back(trail, 1) + lead
    return jnp.concatenate(
        [s[i:i + 1] * (1.0 / cnt) for i, cnt in enumerate(_window_counts(n, w))], axis=0)


def _pool_kernel(p_ref, gate_ref, wp_ref, ps_ref, y_ref, *, seq_len, grid_mode):
    grp = pl.program_id(1)
    rows, ch = p_ref.shape
    for k, w in enumerate(POOL_WINDOWS):
        @pl.when(grp == k)
        def _(w=w):
            x = p_ref[...].astype(F32)
            if grid_mode:
                m = _mean_major(x.reshape(rows // GRID_W, GRID_W, ch), w)
                m = _mean_minor(m, w).reshape(rows, ch)
            else:
                m = _mean_minor(x.reshape(rows // seq_len, seq_len, ch), w).reshape(rows, ch)
            dlt = (m - x).astype(BF16)
            y = jnp.dot(dlt, wp_ref[0], preferred_element_type=F32) * ps_ref[...]
            y_ref[...] = (y * _silu(gate_ref[...].astype(F32))).astype(BF16)


def _pool_mixer(proj, w_pool, pool_scale, seq_len, grid_mode, rows_per_step):
    n_tok = proj.shape[0]
    n_grp = len(POOL_WINDOWS)
    p_blk0 = AB_PART["pool_in"] * BRANCH_W // POOL_GROUP_W
    g_blk0 = AB_PART["gate_b"] * BRANCH_W // POOL_GROUP_W
    return pl.pallas_call(
        functools.partial(_pool_kernel, seq_len=seq_len, grid_mode=grid_mode),
        grid=(n_tok // rows_per_step, n_grp),
        in_specs=[
            pl.BlockSpec((rows_per_step, POOL_GROUP_W), lambda b, g: (b, p_blk0 + g)),
            pl.BlockSpec((rows_per_step, POOL_GROUP_W), lambda b, g: (b, g_blk0 + g)),
            pl.BlockSpec((1, POOL_GROUP_W, POOL_GROUP_W), lambda b, g: (g, 0, 0)),
            pl.BlockSpec((1, POOL_GROUP_W), lambda b, g: (0, g)),
        ],
        out_specs=pl.BlockSpec((rows_per_step, POOL_GROUP_W), lambda b, g: (b, g)),
        out_shape=jax.ShapeDtypeStruct((n_tok, BRANCH_W), BF16),
        compiler_params=_cparams(2, 56),
        name="pool_mixer",
    )(proj, proj, w_pool.astype(BF16), pool_scale.reshape(1, BRANCH_W))


def _hgrn_triangles():
    ones = np.ones((HGRN_CHUNK, HGRN_CHUNK), np.float32)
    return np.concatenate([np.tril(ones), np.triu(ones)], axis=0)


def _hgrn_partial_sum_matrix():
    c = HGRN_CHUNK
    blocks = []
    for m in HGRN_PSUM_LEVELS:
        mq = np.zeros((c, 2 * c), np.float32)
        mk = np.zeros((c, 2 * c), np.float32)
        for i in range(c):
            start = (i // (2 * m)) * (2 * m)
            r1, r2 = start + m - 1, start + m
            if i >= r2:
                mq[i, r2:i + 1] = 1
                mk[i, c + r2:c + i] = 1
            else:
                mq[i, c + i:c + r1 + 1] = 1
                mk[i, i + 1:r1 + 1] = 1
        blocks += [mq, mk]
    return np.concatenate(blocks, axis=0)


def _hgrn_level_index():
    i = np.arange(HGRN_CHUNK)
    x = i[:, None] ^ i[None, :]
    lv = np.zeros_like(x)
    nz = x > 0
    lv[nz] = np.floor(np.log2(x[nz])).astype(x.dtype) + 1
    return lv.astype(np.int32)


_HGRN_TRI = _hgrn_triangles()
_HGRN_PSUM = _hgrn_partial_sum_matrix()
_HGRN_LEVEL = _hgrn_level_index()

_NT = (((1,), (1,)), ((), ()))
_TN = (((0,), (0,)), ((), ()))


def _hgrn_kernel(*refs, n_chunks, heads, has_s0, has_prev, want_final, final_own):
    q_ref, ff_ref, fb_ref, v_ref, ga_ref, lb_ref, gon_ref, tri_ref, psum_ref, lvl_ref = refs[:10]
    pos = 10
    s0_ref = sfin_ref = None
    if has_s0:
        s0_ref = refs[pos]
        pos += 1
    if has_prev:
        pos += 1
    y_ref = refs[pos]
    pos += 1
    if want_final:
        sfin_ref = refs[pos]
        pos += 1
    o_scr, qfb_scr, dst_scr, dec_scr, st_scr = refs[pos:]

    c = HGRN_CHUNK
    inline = heads * n_chunks <= 8
    level = lvl_ref[...]
    row8 = jnp.concatenate([lax.broadcasted_iota(jnp.int32, (8, HEAD_D), 0)] * (c // 8), axis=0)

    def chunk_rows(ci):
        if isinstance(ci, int):
            return pl.ds(ci * c, c)
        return pl.ds(pl.multiple_of(ci * c, c), c)

    def over_chunks(body, init, unroll):
        if inline:
            carry = init
            for ci in range(n_chunks):
                carry = body(ci, carry)
            return carry
        return lax.fori_loop(0, n_chunks, body, init, unroll=unroll)

    def over_heads_and_chunks(fn, in_flight):
        def body(ci, carry):
            for hh in range(heads):
                fn(hh, ci, carry)
            return carry
        over_chunks(body, 0, unroll=max(1, min(n_chunks, in_flight // heads)))

    def hi_mid(x):
        hi = x.astype(BF16)
        return hi, (x - hi.astype(F32)).astype(BF16)

    def summed(zero_one, hi, mid):
        dd = jnp.dot(zero_one, jnp.concatenate([hi, mid], axis=1), preferred_element_type=F32)
        return dd[:, :HEAD_D] + dd[:, HEAD_D:]

    def local_pass(hh, ci, carry):
        rows = chunk_rows(ci)
        lanes = slice(hh * HEAD_D, (hh + 1) * HEAD_D)
        lb_f, lb_b = lb_ref[0:1, lanes], lb_ref[1:2, lanes]
        f_f = lb_f + (1.0 - lb_f) * jax.nn.sigmoid(ff_ref[rows, lanes])
        f_b = lb_b + (1.0 - lb_b) * jax.nn.sigmoid(fb_ref[rows, lanes])
        k_f, k_b = 1.0 - f_f, 1.0 - f_b
        hi_f, mid_f = hi_mid(jnp.log2(f_f))
        hi_b, mid_b = hi_mid(jnp.log2(f_b))
        b_f = summed(tri_ref[0:c, :], hi_f, mid_f)
        b_b = summed(tri_ref[c:2 * c, :], hi_b, mid_b)
        small = summed(psum_ref[...], jnp.concatenate([hi_f, hi_b], axis=0),
                       jnp.concatenate([mid_f, mid_b], axis=0))

        q = _silu(q_ref[rows, lanes].astype(F32))
        v = v_ref[rows, lanes]

        def level_scores(d_q, d_k, k_sel):
            qt = (q * jnp.exp2(d_q)).astype(BF16)
            kt = (k_sel * jnp.exp2(d_k)).astype(BF16)
            return lax.dot_general(qt, kt, _NT, preferred_element_type=F32)

        s = lax.dot_general(q.astype(BF16), (k_f + k_b).astype(BF16), _NT,
                            preferred_element_type=F32)
        s = jnp.where(level == 0, s, 0.0)
        odd = (row8 & 1) == 1
        q1 = (q * jnp.where(odd, f_f, f_b)).astype(BF16)
        k1 = jnp.where(odd, k_b, k_f).astype(BF16)
        s = jnp.where(level == 1, lax.dot_general(q1, k1, _NT, preferred_element_type=F32), s)
        for li, m in enumerate(HGRN_PSUM_LEVELS):
            second = (row8 & m) == m
            s_l = level_scores(small[2 * li * c:(2 * li + 1) * c], small[(2 * li + 1) * c:(2 * li + 2) * c],
                               jnp.where(second, k_b, k_f))
            s = jnp.where(level == m.bit_length(), s_l, s)
        for m in HGRN_LEVELS:
            dq_parts, dk_parts, k_parts = [], [], []
            for lo in range(0, c, 2 * m):
                mid_row, hi_row = lo + m, lo + 2 * m
                edge_f = b_f[mid_row - 1:mid_row]
                edge_b = b_b[mid_row:mid_row + 1]
                dq_parts += [b_b[lo:mid_row] - edge_b, b_f[mid_row:hi_row] - edge_f]
                dk_parts += [edge_f - b_f[lo:mid_row], edge_b - b_b[mid_row:hi_row]]
                k_parts += [k_f[lo:mid_row], k_b[mid_row:hi_row]]
            s_l = level_scores(jnp.concatenate(dq_parts, axis=0), jnp.concatenate(dk_parts, axis=0),
                               jnp.concatenate(k_parts, axis=0))
            s = jnp.where(level == m.bit_length(), s_l, s)

        q_fb = jnp.concatenate([q * jnp.exp2(b_f), q * jnp.exp2(b_b)], axis=1).astype(BF16)
        k_fb = jnp.concatenate([k_f * jnp.exp2(b_f[c - 1:c] - b_f),
                                k_b * jnp.exp2(b_b[0:1] - b_b)], axis=1).astype(BF16)
        dec = jnp.concatenate([jnp.exp2(b_f[c - 1:c]), jnp.exp2(b_b[0:1])], axis=1)

        o_scr[hh, rows, :] = jnp.dot(s.astype(BF16), v, preferred_element_type=F32)
        qfb_scr[hh, rows, :] = q_fb
        dst_scr[hh, ci] = lax.dot_general(v, k_fb, _TN, preferred_element_type=F32)
        dec_scr[hh, ci] = jnp.broadcast_to(dec, (8, 2 * HEAD_D))
        return carry

    over_heads_and_chunks(local_pass, 16)

    for hh in range(heads):
        if has_s0:
            starts = (s0_ref[0, 0, 0, hh].T, s0_ref[0, 0, 1, hh].T)
        else:
            starts = (jnp.zeros((HEAD_D, HEAD_D), F32),) * 2

        def scan_step(k, states, hh=hh):
            out = []
            for d, ci in ((0, k), (1, n_chunks - 1 - k)):
                lanes = slice(d * HEAD_D, (d + 1) * HEAD_D)
                st_scr[hh, ci, :, lanes] = states[d].astype(BF16)
                out.append(states[d] * dec_scr[hh, ci, 0:1, lanes] + dst_scr[hh, ci, :, lanes])
            return tuple(out)

        st_f, st_b = over_chunks(scan_step, starts, unroll=2)
        if want_final:
            sfin_ref[0, final_own, 0, hh] = st_f.T
            sfin_ref[0, final_own, 1, hh] = st_b.T
            for other in range(sfin_ref.shape[1]):
                if other != final_own:
                    sfin_ref[0, other, :, hh] = jnp.zeros((2, HEAD_D, HEAD_D), F32)

    def output_pass(hh, ci, carry):
        rows = chunk_rows(ci)
        lanes = slice(hh * HEAD_D, (hh + 1) * HEAD_D)
        o = o_scr[hh, rows, :] + lax.dot_general(qfb_scr[hh, rows, :], st_scr[hh, ci], _NT,
                                                 preferred_element_type=F32)
        y_ref[rows, lanes] = (_rms(o, gon_ref[...])
                              * _silu(ga_ref[rows, lanes].astype(F32))).astype(BF16)
        return carry

    over_heads_and_chunks(output_pass, 8)


def _hgrn_mixer(gates, proj, lb, g_onorm, state, layer_j, seq_len, heads, want_final,
                final_prev=None):
    n_tok = proj.shape[0]
    n_seq = n_tok // seq_len
    n_chunks = seq_len // HGRN_CHUNK
    has_s0 = state is not None
    has_prev = final_prev is not None
    width = heads * HEAD_D
    blk = lambda part: pl.BlockSpec(
        (seq_len, width), lambda b, h: (b, part * (N_HEADS // heads) + h))
    in_specs = [blk(AB_PART["q"]), blk(0), blk(1), blk(AB_PART["i"]), blk(AB_PART["gate_a"]),
                pl.BlockSpec((2, width), lambda b, h: (0, h)),
                pl.BlockSpec((1, HEAD_D), lambda b, h: (0, 0)),
                pl.BlockSpec(_HGRN_TRI.shape, lambda b, h: (0, 0)),
                pl.BlockSpec(_HGRN_PSUM.shape, lambda b, h: (0, 0)),
                pl.BlockSpec(_HGRN_LEVEL.shape, lambda b, h: (0, 0))]
    args = [proj, gates, gates, proj, proj, lb, g_onorm.reshape(1, HEAD_D),
            jnp.asarray(_HGRN_TRI, BF16), jnp.asarray(_HGRN_PSUM, BF16), jnp.asarray(_HGRN_LEVEL)]
    state_blk = pl.BlockSpec((1, 1, 2, heads, HEAD_D, HEAD_D), lambda b, h: (b, layer_j, 0, h, 0, 0))
    if has_s0:
        in_specs.append(state_blk)
        args.append(state)
    aliases = {}
    if has_prev:
        aliases[len(args)] = 1
        in_specs.append(pl.BlockSpec(memory_space=pl.ANY))
        args.append(final_prev)
    out_specs = [pl.BlockSpec((seq_len, width), lambda b, h: (b, h))]
    out_shape = [jax.ShapeDtypeStruct((n_tok, BRANCH_W), BF16)]
    if want_final:
        out_specs.append(state_blk if has_prev else pl.BlockSpec(
            (1, N_AB, 2, heads, HEAD_D, HEAD_D), lambda b, h: (b, 0, 0, h, 0, 0)))
        out_shape.append(jax.ShapeDtypeStruct((n_seq, N_AB, 2, N_HEADS, HEAD_D, HEAD_D), F32))
    outs = pl.pallas_call(
        functools.partial(_hgrn_kernel, n_chunks=n_chunks, heads=heads, has_s0=has_s0,
                          has_prev=has_prev, want_final=want_final,
                          final_own=0 if has_prev else layer_j),
        grid=(n_seq, N_HEADS // heads),
        in_specs=in_specs,
        out_specs=out_specs,
        out_shape=out_shape,
        input_output_aliases=aliases,
        scratch_shapes=[
            pltpu.VMEM((heads, seq_len, HEAD_D), F32),
            pltpu.VMEM((heads, seq_len, 2 * HEAD_D), BF16),
            pltpu.VMEM((heads, n_chunks, HEAD_D, 2 * HEAD_D), F32),
            pltpu.VMEM((heads, n_chunks, 8, 2 * HEAD_D), F32),
            pltpu.VMEM((heads, n_chunks, HEAD_D, 2 * HEAD_D), BF16),
        ],
        compiler_params=_cparams(2, 48),
        name="hgrn_mixer",
    )(*args)
    return outs if want_final else (outs[0], None)


def kernel(x_prompt, x_sample, c, state_hgrn, c_ctx, w_ada, b_ada, g_pre, g_post, w_in_ab, w_out_ab,
           lb_logits, g_onorm_a, w_pool, pool_scale, w_in_c, w_out_c, ln_v_g, ln_v_b, w_spatial,
           b_spatial):
    n_batch, seq_p, _ = x_prompt.shape
    dec_batch, seq_s, _ = x_sample.shape
    tm_in, tm_out, tn, tm_c, pool_rows = 1024, 512, BRANCH_W, 512, 2048

    cond = jnp.concatenate(
        [c_ctx[None, :], c, jnp.zeros((N_COND_ROWS - 1 - dec_batch, D_MODEL), F32)], axis=0)
    mods = _modulations(cond, w_ada, b_ada).reshape(DEPTH, N_COND_ROWS, 3, D_MODEL)
    lb_all = _lower_bounds(lb_logits)

    flows = [
        dict(x=x_prompt.reshape(n_batch * seq_p, D_MODEL), seq=seq_p, grid=False, state=None,
             row=lambda tm: (lambda i: 0)),
        dict(x=x_sample.reshape(dec_batch * seq_s, D_MODEL), seq=seq_s, grid=True, state=state_hgrn,
             row=lambda tm: (lambda i: 1 + i // (seq_s // tm))),
    ]
    new_state = None
    for l in range(DEPTH):
        j = l // 2
        if l % 2 == 0:
            w_in, w_out = w_in_ab[j].astype(BF16), w_out_ab[j].astype(BF16)
            reorder = lambda t: jnp.where(t < AB_F32_PARTS, t + 1, jnp.where(t == AB_F32_PARTS, 0, t))
            for fi, fl in enumerate(flows):
                gates, proj = _in_proj(fl["x"], mods[l], g_pre[l], w_in, fl["row"](tm_in), tm_in, tn,
                                       n_f32_tiles=AB_F32_PARTS, w_tile_of=reorder)
                heads = max(2, min(N_HEADS, 8 * HGRN_CHUNK // fl["seq"]))
                if fi == 0:
                    y_a, new_state = _hgrn_mixer(gates, proj, lb_all[j], g_onorm_a[j], None, j,
                                                 fl["seq"], heads, want_final=True,
                                                 final_prev=new_state)
                else:
                    y_a, _ = _hgrn_mixer(gates, proj, lb_all[j], g_onorm_a[j], fl["state"], j,
                                         fl["seq"], heads, want_final=False)
                y_b = _pool_mixer(proj, w_pool[j], pool_scale[j], fl["seq"], fl["grid"], pool_rows)
                fl["x"] = _out_proj([y_a, y_b], w_out, fl["x"], mods[l], g_post[l],
                                    fl["row"](tm_out), tm_out)
        else:
            w_in, w_out = w_in_c[j].astype(BF16), w_out_c[j].astype(BF16)
            for fl in flows:
                fl["x"] = _layer_c(fl["x"], mods[l], g_pre[l], g_post[l], w_in, w_out, ln_v_g[j],
                                   ln_v_b[j], w_spatial[j], b_spatial[j], fl["row"](tm_c), tm_c)
    y_p = flows[0]["x"].reshape(n_batch, seq_p, D_MODEL)
    y_s = flows[1]["x"].reshape(dec_batch, seq_s, D_MODEL)
    return (y_p, y_s, new_state)
```

```python
import functools

import numpy as np
import jax
import jax.numpy as jnp
from jax import lax
from jax.experimental import pallas as pl
from jax.experimental.pallas import tpu as pltpu

F32 = jnp.float32
BF16 = jnp.bfloat16

D_MODEL = 1024
DEPTH = 4
N_AB = (DEPTH + 1) // 2
MIX_WIDTH = 2 * D_MODEL
BRANCH_W = MIX_WIDTH // 2
HEAD_D = 128
N_HEADS = BRANCH_W // HEAD_D
GRID_W = 64
POOL_WINDOWS = (2, 4, 8, 16)
POOL_GROUP_W = BRANCH_W // len(POOL_WINDOWS)
CHUNK_C = 128
N_GROUPS_C = 8
GROUP_W_C = MIX_WIDTH // N_GROUPS_C
AB_IN = 7 * BRANCH_W
C_IN = 3 * MIX_WIDTH
AB_F32_PARTS = 2
AB_PART = {"q": 0, "i": 1, "gate_a": 2, "pool_in": 3, "gate_b": 4}
EPS = 1e-6
N_COND_ROWS = 8

HGRN_CHUNK = 128
HGRN_LEVELS = (64, 32, 16, 8)
HGRN_PSUM_LEVELS = (2, 4)
MIB = 1024 * 1024


def _cparams(n_axes, vmem_mib):
    return pltpu.CompilerParams(
        dimension_semantics=("arbitrary",) * n_axes, vmem_limit_bytes=int(vmem_mib * MIB))


def _resident(shape):
    zeros = (0,) * len(shape)
    return pl.BlockSpec(shape, lambda *_: zeros, pipeline_mode=pl.Buffered(1))


def _silu(x):
    return x * jax.nn.sigmoid(x)


def _rms(x, g):
    return x * lax.rsqrt(jnp.mean(x * x, axis=-1, keepdims=True) + EPS) * g


def _modulated_norm(x, mod, g):
    return _rms(x, g) * (1.0 + mod[1:2]) + mod[0:1]


def _mod_kernel(cond_ref, w_ref, b_ref, o_ref):
    a = _silu(cond_ref[...])
    o_ref[0] = jnp.dot(a, w_ref[0], preferred_element_type=F32,
                       precision=lax.Precision.HIGHEST) + b_ref[0]


def _modulations(cond, w_ada, b_ada):
    tn = 1024
    n3 = 3 * D_MODEL
    return pl.pallas_call(
        _mod_kernel,
        grid=(DEPTH, n3 // tn),
        in_specs=[
            pl.BlockSpec((N_COND_ROWS, D_MODEL), lambda l, j: (0, 0)),
            pl.BlockSpec((1, D_MODEL, tn), lambda l, j: (l, 0, j)),
            pl.BlockSpec((1, 1, tn), lambda l, j: (l, 0, j)),
        ],
        out_specs=pl.BlockSpec((1, N_COND_ROWS, tn), lambda l, j: (l, 0, j)),
        out_shape=jax.ShapeDtypeStruct((DEPTH, N_COND_ROWS, n3), F32),
        compiler_params=_cparams(2, 24),
        name="adaln_modulation",
    )(cond, w_ada, b_ada.reshape(DEPTH, 1, n3))


def _lb_kernel(x_ref, o_ref):
    x = x_ref[...]
    e = jnp.exp(x - jnp.max(x, axis=0, keepdims=True))
    p = e / jnp.sum(e, axis=0, keepdims=True)
    run = p[0]
    o_ref[0] = run - p[0]
    for l in range(1, x.shape[0]):
        run = run + p[l]
        o_ref[l] = run - p[0]


def _lower_bounds(lb_logits):
    return pl.pallas_call(
        _lb_kernel, out_shape=jax.ShapeDtypeStruct(lb_logits.shape, F32), name="hgrn_lower_bounds",
    )(lb_logits)


def _in_kernel(x_ref, mod_ref, g_ref, w_ref, *rest, n_f32_tiles):
    out_refs, h_ref = rest[:-1], rest[-1]
    j = pl.program_id(1)

    @pl.when(j == 0)
    def _():
        h_ref[...] = _modulated_norm(x_ref[...], mod_ref[0], g_ref[...]).astype(BF16)

    def project(o_ref):
        o_ref[...] = jnp.dot(h_ref[...], w_ref[...], preferred_element_type=F32).astype(o_ref.dtype)

    if n_f32_tiles:
        pl.when(j < n_f32_tiles)(lambda: project(out_refs[0]))
        pl.when(j >= n_f32_tiles)(lambda: project(out_refs[1]))
    else:
        project(out_refs[0])


def _in_proj(x, mod, g, w, row_of_tile, tm, tn, n_f32_tiles=0, w_tile_of=lambda j: j):
    n_tok, n_out = x.shape[0], w.shape[1]
    n_tiles = n_out // tn
    out_specs, out_shape = [], []
    if n_f32_tiles:
        out_specs.append(pl.BlockSpec((tm, tn), lambda i, j: (i, jnp.minimum(j, n_f32_tiles - 1))))
        out_shape.append(jax.ShapeDtypeStruct((n_tok, n_f32_tiles * tn), F32))
    out_specs.append(pl.BlockSpec((tm, tn), lambda i, j: (i, jnp.maximum(j - n_f32_tiles, 0))))
    out_shape.append(jax.ShapeDtypeStruct((n_tok, (n_tiles - n_f32_tiles) * tn), BF16))
    return pl.pallas_call(
        functools.partial(_in_kernel, n_f32_tiles=n_f32_tiles),
        grid=(n_tok // tm, n_tiles),
        in_specs=[
            pl.BlockSpec((tm, D_MODEL), lambda i, j: (i, 0)),
            pl.BlockSpec((1, 3, D_MODEL), lambda i, j: (row_of_tile(i), 0, 0)),
            pl.BlockSpec((1, D_MODEL), lambda i, j: (0, 0)),
            pl.BlockSpec((D_MODEL, tn), lambda i, j: (0, w_tile_of(j))),
        ],
        out_specs=out_specs,
        out_shape=out_shape,
        scratch_shapes=[pltpu.VMEM((tm, D_MODEL), BF16)],
        compiler_params=_cparams(2, 52),
        name="in_projection",
    )(x, mod, g.reshape(1, D_MODEL), w)


def _out_kernel(*refs, n_parts):
    y_refs, w_refs = refs[:n_parts], refs[n_parts:2 * n_parts]
    x_ref, mod_ref, g_ref, o_ref = refs[2 * n_parts:]
    acc = jnp.dot(y_refs[0][...], w_refs[0][...], preferred_element_type=F32)
    for y_ref, w_ref in zip(y_refs[1:], w_refs[1:]):
        acc = acc + jnp.dot(y_ref[...], w_ref[...], preferred_element_type=F32)
    o_ref[...] = x_ref[...] + mod_ref[0][2:3] * _rms(acc, g_ref[...])


def _out_proj(ys, w, x, mod, g, row_of_tile, tm):
    n_tok = x.shape[0]
    n_parts = len(ys)
    width = ys[0].shape[1]
    in_specs = [pl.BlockSpec((tm, width), lambda i: (i, 0)) for _ in ys]
    in_specs += [pl.BlockSpec((width, D_MODEL), lambda i, k=k: (k, 0)) for k in range(n_parts)]
    in_specs += [
        pl.BlockSpec((tm, D_MODEL), lambda i: (i, 0)),
        pl.BlockSpec((1, 3, D_MODEL), lambda i: (row_of_tile(i), 0, 0)),
        pl.BlockSpec((1, D_MODEL), lambda i: (0, 0)),
    ]
    return pl.pallas_call(
        functools.partial(_out_kernel, n_parts=n_parts),
        grid=(n_tok // tm,),
        in_specs=in_specs,
        out_specs=pl.BlockSpec((tm, D_MODEL), lambda i: (i, 0)),
        out_shape=jax.ShapeDtypeStruct((n_tok, D_MODEL), F32),
        compiler_params=_cparams(1, 40),
        name="out_projection",
    )(*ys, *([w] * n_parts), x, mod, g.reshape(1, D_MODEL))


def _layer_c_kernel(x_ref, mod_ref, gpre_ref, win_ref, lng_ref, lnb_ref, ws_ref, bs_ref, wout_ref,
                    gpost_ref, o_ref, y_scr, *, tm):
    x = x_ref[...]
    mod = mod_ref[0]
    h = _modulated_norm(x, mod, gpre_ref[...]).astype(BF16)

    v = jnp.dot(h, win_ref[:, MIX_WIDTH:2 * MIX_WIDTH], preferred_element_type=F32)
    mu = jnp.mean(v, axis=-1, keepdims=True)
    vc = v - mu
    var = jnp.mean(vc * vc, axis=-1, keepdims=True)
    vn = (vc * lax.rsqrt(var + EPS) * lng_ref[...] + lnb_ref[...]).astype(BF16)

    for gi in range(N_GROUPS_C):
        cols = slice(gi * GROUP_W_C, (gi + 1) * GROUP_W_C)
        u = jnp.dot(h, win_ref[:, cols], preferred_element_type=F32)
        gate = _silu(jnp.dot(h, win_ref[:, 2 * MIX_WIDTH + gi * GROUP_W_C:
                                        2 * MIX_WIDTH + (gi + 1) * GROUP_W_C],
                             preferred_element_type=F32))
        for n in range(tm // CHUNK_C):
            rows = slice(n * CHUNK_C, (n + 1) * CHUNK_C)
            sp = jnp.dot(ws_ref[gi], vn[rows, cols], preferred_element_type=F32) + bs_ref[:, gi:gi + 1]
            y_scr[rows, cols] = (u[rows] * sp * gate[rows]).astype(BF16)

    acc = jnp.dot(y_scr[...], wout_ref[...], preferred_element_type=F32)
    o_ref[...] = x + mod[2:3] * _rms(acc, gpost_ref[...])


def _layer_c(x, mod, g_pre, g_post, w_in, w_out, ln_g, ln_b, w_s, b_s, row_of_tile, tm):
    n_tok = x.shape[0]
    return pl.pallas_call(
        functools.partial(_layer_c_kernel, tm=tm),
        grid=(n_tok // tm,),
        in_specs=[
            pl.BlockSpec((tm, D_MODEL), lambda i: (i, 0)),
            pl.BlockSpec((1, 3, D_MODEL), lambda i: (row_of_tile(i), 0, 0)),
            _resident((1, D_MODEL)),
            _resident((D_MODEL, C_IN)),
            _resident((1, MIX_WIDTH)),
            _resident((1, MIX_WIDTH)),
            _resident((N_GROUPS_C, CHUNK_C, CHUNK_C)),
            _resident((CHUNK_C, N_GROUPS_C)),
            _resident((MIX_WIDTH, D_MODEL)),
            _resident((1, D_MODEL)),
        ],
        out_specs=pl.BlockSpec((tm, D_MODEL), lambda i: (i, 0)),
        out_shape=jax.ShapeDtypeStruct((n_tok, D_MODEL), F32),
        scratch_shapes=[pltpu.VMEM((tm, MIX_WIDTH), BF16)],
        compiler_params=_cparams(1, 48),
        name="gmlp_layer",
    )(x, mod, g_pre.reshape(1, D_MODEL), w_in, ln_g.reshape(1, MIX_WIDTH), ln_b.reshape(1, MIX_WIDTH),
      w_s.astype(BF16), b_s.T, w_out, g_post.reshape(1, D_MODEL))


def _window_counts(n, w):
    return [min(i - w // 2 + w, n) - max(i - w // 2, 0) for i in range(n)]


def _mean_minor(x3, w):
    g, n, ch = x3.shape
    idx = lax.broadcasted_iota(jnp.int32, (1, n, ch), 1)

    def shifted(a, k):
        return pltpu.roll(a.reshape(g * n, ch), k % (g * n), 0).reshape(g, n, ch)

    trail, lead = x3, x3
    size = 1
    while size < w // 2:
        trail = trail + jnp.where(idx >= size, shifted(trail, size), 0.0)
        lead = lead + jnp.where(idx + size < n, shifted(lead, -size), 0.0)
        size *= 2
    s = jnp.where(idx >= 1, shifted(trail, 1), 0.0) + lead
    half = w // 2
    cnt = jnp.minimum(idx - half + w, n) - jnp.maximum(idx - half, 0)
    return s * (1.0 / cnt.astype(F32))


def _mean_major(x3, w):
    n = x3.shape[0]
    zeros = lambda k: jnp.zeros((k,) + x3.shape[1:], F32)
    back = lambda a, k: jnp.concatenate([zeros(k), a[:n - k]], axis=0)
    ahead = lambda a, k: jnp.concatenate([a[k:], zeros(k)], axis=0)
    trail, lead = x3, x3
    size = 1
    while size < w // 2:
        trail = trail + back(trail, size)
        lead = lead + ahead(lead, size)
        size *= 2
    s = back(trail, 1) + lead
    return jnp.concatenate(
        [s[i:i + 1] * (1.0 / cnt) for i, cnt in enumerate(_window_counts(n, w))], axis=0)


def _pool_kernel(p_ref, gate_ref, wp_ref, ps_ref, y_ref, *, seq_len, grid_mode):
    grp = pl.program_id(1)
    rows, ch = p_ref.shape
    for k, w in enumerate(POOL_WINDOWS):
        @pl.when(grp == k)
        def _(w=w):
            x = p_ref[...].astype(F32)
            if grid_mode:
                m = _mean_major(x.reshape(rows // GRID_W, GRID_W, ch), w)
                m = _mean_minor(m, w).reshape(rows, ch)
            else:
                m = _mean_minor(x.reshape(rows // seq_len, seq_len, ch), w).reshape(rows, ch)
            dlt = (m - x).astype(BF16)
            y = jnp.dot(dlt, wp_ref[0], preferred_element_type=F32) * ps_ref[...]
            y_ref[...] = (y * _silu(gate_ref[...].astype(F32))).astype(BF16)


def _pool_mixer(proj, w_pool, pool_scale, seq_len, grid_mode, rows_per_step):
    n_tok = proj.shape[0]
    n_grp = len(POOL_WINDOWS)
    p_blk0 = AB_PART["pool_in"] * BRANCH_W // POOL_GROUP_W
    g_blk0 = AB_PART["gate_b"] * BRANCH_W // POOL_GROUP_W
    return pl.pallas_call(
        functools.partial(_pool_kernel, seq_len=seq_len, grid_mode=grid_mode),
        grid=(n_tok // rows_per_step, n_grp),
        in_specs=[
            pl.BlockSpec((rows_per_step, POOL_GROUP_W), lambda b, g: (b, p_blk0 + g)),
            pl.BlockSpec((rows_per_step, POOL_GROUP_W), lambda b, g: (b, g_blk0 + g)),
            pl.BlockSpec((1, POOL_GROUP_W, POOL_GROUP_W), lambda b, g: (g, 0, 0)),
            pl.BlockSpec((1, POOL_GROUP_W), lambda b, g: (0, g)),
        ],
        out_specs=pl.BlockSpec((rows_per_step, POOL_GROUP_W), lambda b, g: (b, g)),
        out_shape=jax.ShapeDtypeStruct((n_tok, BRANCH_W), BF16),
        compiler_params=_cparams(2, 56),
        name="pool_mixer",
    )(proj, proj, w_pool.astype(BF16), pool_scale.reshape(1, BRANCH_W))


def _hgrn_triangles():
    ones = np.ones((HGRN_CHUNK, HGRN_CHUNK), np.float32)
    return np.concatenate([np.tril(ones), np.triu(ones)], axis=0)


def _hgrn_partial_sum_matrix():
    c = HGRN_CHUNK
    blocks = []
    for m in HGRN_PSUM_LEVELS:
        mq = np.zeros((c, 2 * c), np.float32)
        mk = np.zeros((c, 2 * c), np.float32)
        for i in range(c):
            start = (i // (2 * m)) * (2 * m)
            r1, r2 = start + m - 1, start + m
            if i >= r2:
                mq[i, r2:i + 1] = 1
                mk[i, c + r2:c + i] = 1
            else:
                mq[i, c + i:c + r1 + 1] = 1
                mk[i, i + 1:r1 + 1] = 1
        blocks += [mq, mk]
    return np.concatenate(blocks, axis=0)


def _hgrn_level_index():
    i = np.arange(HGRN_CHUNK)
    x = i[:, None] ^ i[None, :]
    lv = np.zeros_like(x)
    nz = x > 0
    lv[nz] = np.floor(np.log2(x[nz])).astype(x.dtype) + 1
    return lv.astype(np.int32)


_HGRN_TRI = _hgrn_triangles()
_HGRN_PSUM = _hgrn_partial_sum_matrix()
_HGRN_LEVEL = _hgrn_level_index()

_NT = (((1,), (1,)), ((), ()))
_TN = (((0,), (0,)), ((), ()))


def _hgrn_kernel(*refs, n_chunks, heads, has_s0, has_prev, want_final, final_own):
    q_ref, ff_ref, fb_ref, v_ref, ga_ref, lb_ref, gon_ref, tri_ref, psum_ref, lvl_ref = refs[:10]
    pos = 10
    s0_ref = sfin_ref = None
    if has_s0:
        s0_ref = refs[pos]
        pos += 1
    if has_prev:
        pos += 1
    y_ref = refs[pos]
    pos += 1
    if want_final:
        sfin_ref = refs[pos]
        pos += 1
    o_scr, qfb_scr, dst_scr, dec_scr, st_scr = refs[pos:]

    c = HGRN_CHUNK
    inline = heads * n_chunks <= 8
    level = lvl_ref[...]
    row8 = jnp.concatenate([lax.broadcasted_iota(jnp.int32, (8, HEAD_D), 0)] * (c // 8), axis=0)

    def chunk_rows(ci):
        if isinstance(ci, int):
            return pl.ds(ci * c, c)
        return pl.ds(pl.multiple_of(ci * c, c), c)

    def over_chunks(body, init, unroll):
        if inline:
            carry = init
            for ci in range(n_chunks):
                carry = body(ci, carry)
            return carry
        return lax.fori_loop(0, n_chunks, body, init, unroll=unroll)

    def over_heads_and_chunks(fn, in_flight):
        def body(ci, carry):
            for hh in range(heads):
                fn(hh, ci, carry)
            return carry
        over_chunks(body, 0, unroll=max(1, min(n_chunks, in_flight // heads)))

    def hi_mid(x):
        hi = x.astype(BF16)
        return hi, (x - hi.astype(F32)).astype(BF16)

    def summed(zero_one, hi, mid):
        dd = jnp.dot(zero_one, jnp.concatenate([hi, mid], axis=1), preferred_element_type=F32)
        return dd[:, :HEAD_D] + dd[:, HEAD_D:]

    def local_pass(hh, ci, carry):
        rows = chunk_rows(ci)
        lanes = slice(hh * HEAD_D, (hh + 1) * HEAD_D)
        lb_f, lb_b = lb_ref[0:1, lanes], lb_ref[1:2, lanes]
        f_f = lb_f + (1.0 - lb_f) * jax.nn.sigmoid(ff_ref[rows, lanes])
        f_b = lb_b + (1.0 - lb_b) * jax.nn.sigmoid(fb_ref[rows, lanes])
        k_f, k_b = 1.0 - f_f, 1.0 - f_b
        hi_f, mid_f = hi_mid(jnp.log2(f_f))
        hi_b, mid_b = hi_mid(jnp.log2(f_b))
        b_f = summed(tri_ref[0:c, :], hi_f, mid_f)
        b_b = summed(tri_ref[c:2 * c, :], hi_b, mid_b)
        small = summed(psum_ref[...], jnp.concatenate([hi_f, hi_b], axis=0),
                       jnp.concatenate([mid_f, mid_b], axis=0))

        q = _silu(q_ref[rows, lanes].astype(F32))
        v = v_ref[rows, lanes]

        def level_scores(d_q, d_k, k_sel):
            qt = (q * jnp.exp2(d_q)).astype(BF16)
            kt = (k_sel * jnp.exp2(d_k)).astype(BF16)
            return lax.dot_general(qt, kt, _NT, preferred_element_type=F32)

        s = lax.dot_general(q.astype(BF16), (k_f + k_b).astype(BF16), _NT,
                            preferred_element_type=F32)
        s = jnp.where(level == 0, s, 0.0)
        odd = (row8 & 1) == 1
        q1 = (q * jnp.where(odd, f_f, f_b)).astype(BF16)
        k1 = jnp.where(odd, k_b, k_f).astype(BF16)
        s = jnp.where(level == 1, lax.dot_general(q1, k1, _NT, preferred_element_type=F32), s)
        for li, m in enumerate(HGRN_PSUM_LEVELS):
            second = (row8 & m) == m
            s_l = level_scores(small[2 * li * c:(2 * li + 1) * c], small[(2 * li + 1) * c:(2 * li + 2) * c],
                               jnp.where(second, k_b, k_f))
            s = jnp.where(level == m.bit_length(), s_l, s)
        for m in HGRN_LEVELS:
            dq_parts, dk_parts, k_parts = [], [], []
            for lo in range(0, c, 2 * m):
                mid_row, hi_row = lo + m, lo + 2 * m
                edge_f = b_f[mid_row - 1:mid_row]
                edge_b = b_b[mid_row:mid_row + 1]
                dq_parts += [b_b[lo:mid_row] - edge_b, b_f[mid_row:hi_row] - edge_f]
                dk_parts += [edge_f - b_f[lo:mid_row], edge_b - b_b[mid_row:hi_row]]
                k_parts += [k_f[lo:mid_row], k_b[mid_row:hi_row]]
            s_l = level_scores(jnp.concatenate(dq_parts, axis=0), jnp.concatenate(dk_parts, axis=0),
                               jnp.concatenate(k_parts, axis=0))
            s = jnp.where(level == m.bit_length(), s_l, s)

        q_fb = jnp.concatenate([q * jnp.exp2(b_f), q * jnp.exp2(b_b)], axis=1).astype(BF16)
        k_fb = jnp.concatenate([k_f * jnp.exp2(b_f[c - 1:c] - b_f),
                                k_b * jnp.exp2(b_b[0:1] - b_b)], axis=1).astype(BF16)
        dec = jnp.concatenate([jnp.exp2(b_f[c - 1:c]), jnp.exp2(b_b[0:1])], axis=1)

        o_scr[hh, rows, :] = jnp.dot(s.astype(BF16), v, preferred_element_type=F32)
        qfb_scr[hh, rows, :] = q_fb
        dst_scr[hh, ci] = lax.dot_general(v, k_fb, _TN, preferred_element_type=F32)
        dec_scr[hh, ci] = jnp.broadcast_to(dec, (8, 2 * HEAD_D))
        return carry

    over_heads_and_chunks(local_pass, 16)

    for hh in range(heads):
        if has_s0:
            starts = (s0_ref[0, 0, 0, hh].T, s0_ref[0, 0, 1, hh].T)
        else:
            starts = (jnp.zeros((HEAD_D, HEAD_D), F32),) * 2

        def scan_step(k, states, hh=hh):
            out = []
            for d, ci in ((0, k), (1, n_chunks - 1 - k)):
                lanes = slice(d * HEAD_D, (d + 1) * HEAD_D)
                st_scr[hh, ci, :, lanes] = states[d].astype(BF16)
                out.append(states[d] * dec_scr[hh, ci, 0:1, lanes] + dst_scr[hh, ci, :, lanes])
            return tuple(out)

        st_f, st_b = over_chunks(scan_step, starts, unroll=2)
        if want_final:
            sfin_ref[0, final_own, 0, hh] = st_f.T
            sfin_ref[0, final_own, 1, hh] = st_b.T
            for other in range(sfin_ref.shape[1]):
                if other != final_own:
                    sfin_ref[0, other, :, hh] = jnp.zeros((2, HEAD_D, HEAD_D), F32)

    def output_pass(hh, ci, carry):
        rows = chunk_rows(ci)
        lanes = slice(hh * HEAD_D, (hh + 1) * HEAD_D)
        o = o_scr[hh, rows, :] + lax.dot_general(qfb_scr[hh, rows, :], st_scr[hh, ci], _NT,
                                                 preferred_element_type=F32)
        y_ref[rows, lanes] = (_rms(o, gon_ref[...])
                              * _silu(ga_ref[rows, lanes].astype(F32))).astype(BF16)
        return carry

    over_heads_and_chunks(output_pass, 8)


def _hgrn_mixer(gates, proj, lb, g_onorm, state, layer_j, seq_len, heads, want_final,
                final_prev=None):
    n_tok = proj.shape[0]
    n_seq = n_tok // seq_len
    n_chunks = seq_len // HGRN_CHUNK
    has_s0 = state is not None
    has_prev = final_prev is not None
    width = heads * HEAD_D
    blk = lambda part: pl.BlockSpec(
        (seq_len, width), lambda b, h: (b, part * (N_HEADS // heads) + h))
    in_specs = [blk(AB_PART["q"]), blk(0), blk(1), blk(AB_PART["i"]), blk(AB_PART["gate_a"]),
                pl.BlockSpec((2, width), lambda b, h: (0, h)),
                pl.BlockSpec((1, HEAD_D), lambda b, h: (0, 0)),
                pl.BlockSpec(_HGRN_TRI.shape, lambda b, h: (0, 0)),
                pl.BlockSpec(_HGRN_PSUM.shape, lambda b, h: (0, 0)),
                pl.BlockSpec(_HGRN_LEVEL.shape, lambda b, h: (0, 0))]
    args = [proj, gates, gates, proj, proj, lb, g_onorm.reshape(1, HEAD_D),
            jnp.asarray(_HGRN_TRI, BF16), jnp.asarray(_HGRN_PSUM, BF16), jnp.asarray(_HGRN_LEVEL)]
    state_blk = pl.BlockSpec((1, 1, 2, heads, HEAD_D, HEAD_D), lambda b, h: (b, layer_j, 0, h, 0, 0))
    if has_s0:
        in_specs.append(state_blk)
        args.append(state)
    aliases = {}
    if has_prev:
        aliases[len(args)] = 1
        in_specs.append(pl.BlockSpec(memory_space=pl.ANY))
        args.append(final_prev)
    out_specs = [pl.BlockSpec((seq_len, width), lambda b, h: (b, h))]
    out_shape = [jax.ShapeDtypeStruct((n_tok, BRANCH_W), BF16)]
    if want_final:
        out_specs.append(state_blk if has_prev else pl.BlockSpec(
            (1, N_AB, 2, heads, HEAD_D, HEAD_D), lambda b, h: (b, 0, 0, h, 0, 0)))
        out_shape.append(jax.ShapeDtypeStruct((n_seq, N_AB, 2, N_HEADS, HEAD_D, HEAD_D), F32))
    outs = pl.pallas_call(
        functools.partial(_hgrn_kernel, n_chunks=n_chunks, heads=heads, has_s0=has_s0,
                          has_prev=has_prev, want_final=want_final,
                          final_own=0 if has_prev else layer_j),
        grid=(n_seq, N_HEADS // heads),
        in_specs=in_specs,
        out_specs=out_specs,
        out_shape=out_shape,
        input_output_aliases=aliases,
        scratch_shapes=[
            pltpu.VMEM((heads, seq_len, HEAD_D), F32),
            pltpu.VMEM((heads, seq_len, 2 * HEAD_D), BF16),
            pltpu.VMEM((heads, n_chunks, HEAD_D, 2 * HEAD_D), F32),
            pltpu.VMEM((heads, n_chunks, 8, 2 * HEAD_D), F32),
            pltpu.VMEM((heads, n_chunks, HEAD_D, 2 * HEAD_D), BF16),
        ],
        compiler_params=_cparams(2, 48),
        name="hgrn_mixer",
    )(*args)
    return outs if want_final else (outs[0], None)


def kernel(x_prompt, x_sample, c, state_hgrn, c_ctx, w_ada, b_ada, g_pre, g_post, w_in_ab, w_out_ab,
           lb_logits, g_onorm_a, w_pool, pool_scale, w_in_c, w_out_c, ln_v_g, ln_v_b, w_spatial,
           b_spatial):
    n_batch, seq_p, _ = x_prompt.shape
    dec_batch, seq_s, _ = x_sample.shape
    tm_in, tm_out, tn, tm_c, pool_rows = 2048, 512, 512, 512, 2048

    cond = jnp.concatenate(
        [c_ctx[None, :], c, jnp.zeros((N_COND_ROWS - 1 - dec_batch, D_MODEL), F32)], axis=0)
    mods = _modulations(cond, w_ada, b_ada).reshape(DEPTH, N_COND_ROWS, 3, D_MODEL)
    lb_all = _lower_bounds(lb_logits)

    flows = [
        dict(x=x_prompt.reshape(n_batch * seq_p, D_MODEL), seq=seq_p, grid=False, state=None,
             row=lambda tm: (lambda i: 0)),
        dict(x=x_sample.reshape(dec_batch * seq_s, D_MODEL), seq=seq_s, grid=True, state=state_hgrn,
             row=lambda tm: (lambda i: 1 + i // (seq_s // tm))),
    ]
    new_state = None
    for l in range(DEPTH):
        j = l // 2
        if l % 2 == 0:
            w_in, w_out = w_in_ab[j].astype(BF16), w_out_ab[j].astype(BF16)
            per_part = BRANCH_W // tn
            n_f32 = AB_F32_PARTS * per_part
            reorder = lambda t: jnp.where(t < n_f32, t + per_part,
                                          jnp.where(t < n_f32 + per_part, t - n_f32, t))
            for fi, fl in enumerate(flows):
                gates, proj = _in_proj(fl["x"], mods[l], g_pre[l], w_in, fl["row"](tm_in), tm_in, tn,
                                       n_f32_tiles=n_f32, w_tile_of=reorder)
                heads = max(2, min(N_HEADS, 8 * HGRN_CHUNK // fl["seq"]))
                if fi == 0:
                    y_a, new_state = _hgrn_mixer(gates, proj, lb_all[j], g_onorm_a[j], None, j,
                                                 fl["seq"], heads, want_final=True,
                                                 final_prev=new_state)
                else:
                    y_a, _ = _hgrn_mixer(gates, proj, lb_all[j], g_onorm_a[j], fl["state"], j,
                                         fl["seq"], heads, want_final=False)
                y_b = _pool_mixer(proj, w_pool[j], pool_scale[j], fl["seq"], fl["grid"], pool_rows)
                fl["x"] = _out_proj([y_a, y_b], w_out, fl["x"], mods[l], g_post[l],
                                    fl["row"](tm_out), tm_out)
        else:
            w_in, w_out = w_in_c[j].astype(BF16), w_out_c[j].astype(BF16)
            for fl in flows:
                fl["x"] = _layer_c(fl["x"], mods[l], g_pre[l], g_post[l], w_in, w_out, ln_v_g[j],
                                   ln_v_b[j], w_spatial[j], b_spatial[j], fl["row"](tm_c), tm_c)
    y_p = flows[0]["x"].reshape(n_batch, seq_p, D_MODEL)
    y_s = flows[1]["x"].reshape(dec_batch, seq_s, D_MODEL)
    return (y_p, y_s, new_state)
```

```python
import functools

import numpy as np
import jax
import jax.numpy as jnp
from jax import lax
from jax.experimental import pallas as pl
from jax.experimental.pallas import tpu as pltpu

F32 = jnp.float32
BF16 = jnp.bfloat16

D_MODEL = 1024
DEPTH = 4
N_AB = (DEPTH + 1) // 2
MIX_WIDTH = 2 * D_MODEL
BRANCH_W = MIX_WIDTH // 2
HEAD_D = 128
N_HEADS = BRANCH_W // HEAD_D
GRID_W = 64
POOL_WINDOWS = (2, 4, 8, 16)
POOL_GROUP_W = BRANCH_W // len(POOL_WINDOWS)
CHUNK_C = 128
N_GROUPS_C = 8
GROUP_W_C = MIX_WIDTH // N_GROUPS_C
AB_IN = 7 * BRANCH_W
C_IN = 3 * MIX_WIDTH
AB_F32_PARTS = 2
AB_PART = {"q": 0, "i": 1, "gate_a": 2, "pool_in": 3, "gate_b": 4}
EPS = 1e-6
N_COND_ROWS = 8

HGRN_CHUNK = 128
HGRN_LEVELS = (64, 32, 16, 8)
HGRN_PSUM_LEVELS = (2, 4)
MIB = 1024 * 1024


def _cparams(n_axes, vmem_mib):
    return pltpu.CompilerParams(
        dimension_semantics=("arbitrary",) * n_axes, vmem_limit_bytes=int(vmem_mib * MIB))


def _resident(shape, layer=None):
    zeros = (0,) * len(shape)
    if layer is None:
        return pl.BlockSpec(shape, lambda *_: zeros, pipeline_mode=pl.Buffered(1))
    return pl.BlockSpec((None,) + tuple(shape), lambda *_: (layer,) + zeros,
                        pipeline_mode=pl.Buffered(1))


def _silu(x):
    return x * jax.nn.sigmoid(x)


def _rms(x, g):
    return x * lax.rsqrt(jnp.mean(x * x, axis=-1, keepdims=True) + EPS) * g


def _modulated_norm(x, mod, g):
    return _rms(x, g) * (1.0 + mod[1:2]) + mod[0:1]


def _mod_kernel(cond_ref, w_ref, b_ref, o_ref):
    a = _silu(cond_ref[...])
    o_ref[0] = jnp.dot(a, w_ref[0], preferred_element_type=F32,
                       precision=lax.Precision.HIGHEST) + b_ref[0]


def _modulations(cond, w_ada, b_ada):
    tn = 1024
    n3 = 3 * D_MODEL
    return pl.pallas_call(
        _mod_kernel,
        grid=(DEPTH, n3 // tn),
        in_specs=[
            pl.BlockSpec((N_COND_ROWS, D_MODEL), lambda l, j: (0, 0)),
            pl.BlockSpec((1, D_MODEL, tn), lambda l, j: (l, 0, j)),
            pl.BlockSpec((1, 1, tn), lambda l, j: (l, 0, j)),
        ],
        out_specs=pl.BlockSpec((1, N_COND_ROWS, tn), lambda l, j: (l, 0, j)),
        out_shape=jax.ShapeDtypeStruct((DEPTH, N_COND_ROWS, n3), F32),
        compiler_params=_cparams(2, 24),
        name="adaln_modulation",
    )(cond, w_ada, b_ada.reshape(DEPTH, 1, n3))


def _lb_kernel(x_ref, o_ref):
    x = x_ref[...]
    e = jnp.exp(x - jnp.max(x, axis=0, keepdims=True))
    p = e / jnp.sum(e, axis=0, keepdims=True)
    run = p[0]
    o_ref[0] = run - p[0]
    for l in range(1, x.shape[0]):
        run = run + p[l]
        o_ref[l] = run - p[0]


def _lower_bounds(lb_logits):
    return pl.pallas_call(
        _lb_kernel, out_shape=jax.ShapeDtypeStruct(lb_logits.shape, F32), name="hgrn_lower_bounds",
    )(lb_logits)


def _in_kernel(x_ref, mod_ref, g_ref, w_ref, *rest, n_f32_tiles):
    out_refs, h_ref = rest[:-1], rest[-1]
    j = pl.program_id(1)

    @pl.when(j == 0)
    def _():
        h_ref[...] = _modulated_norm(x_ref[...], mod_ref[0], g_ref[...]).astype(BF16)

    def project(o_ref):
        o_ref[...] = jnp.dot(h_ref[...], w_ref[...], preferred_element_type=F32).astype(o_ref.dtype)

    if n_f32_tiles:
        pl.when(j < n_f32_tiles)(lambda: project(out_refs[0]))
        pl.when(j >= n_f32_tiles)(lambda: project(out_refs[1]))
    else:
        project(out_refs[0])


def _in_proj(x, mod, g, w, layer, row_of_tile, tm, tn, n_f32_tiles=0, w_tile_of=lambda j: j):
    n_tok, n_out = x.shape[0], w.shape[2]
    n_tiles = n_out // tn
    out_specs, out_shape = [], []
    if n_f32_tiles:
        out_specs.append(pl.BlockSpec((tm, tn), lambda i, j: (i, jnp.minimum(j, n_f32_tiles - 1))))
        out_shape.append(jax.ShapeDtypeStruct((n_tok, n_f32_tiles * tn), F32))
    out_specs.append(pl.BlockSpec((tm, tn), lambda i, j: (i, jnp.maximum(j - n_f32_tiles, 0))))
    out_shape.append(jax.ShapeDtypeStruct((n_tok, (n_tiles - n_f32_tiles) * tn), BF16))
    return pl.pallas_call(
        functools.partial(_in_kernel, n_f32_tiles=n_f32_tiles),
        grid=(n_tok // tm, n_tiles),
        in_specs=[
            pl.BlockSpec((tm, D_MODEL), lambda i, j: (i, 0)),
            pl.BlockSpec((1, 3, D_MODEL), lambda i, j: (row_of_tile(i), 0, 0)),
            pl.BlockSpec((1, D_MODEL), lambda i, j: (0, 0)),
            pl.BlockSpec((None, D_MODEL, tn), lambda i, j: (layer, 0, w_tile_of(j))),
        ],
        out_specs=out_specs,
        out_shape=out_shape,
        scratch_shapes=[pltpu.VMEM((tm, D_MODEL), BF16)],
        compiler_params=_cparams(2, 40),
        name="in_projection",
    )(x, mod, g.reshape(1, D_MODEL), w)


def _out_kernel(*refs, n_parts):
    y_refs, w_refs = refs[:n_parts], refs[n_parts:2 * n_parts]
    x_ref, mod_ref, g_ref, o_ref = refs[2 * n_parts:]
    acc = jnp.dot(y_refs[0][...], w_refs[0][...], preferred_element_type=F32)
    for y_ref, w_ref in zip(y_refs[1:], w_refs[1:]):
        acc = acc + jnp.dot(y_ref[...], w_ref[...], preferred_element_type=F32)
    o_ref[...] = x_ref[...] + mod_ref[0][2:3] * _rms(acc, g_ref[...])


def _out_proj(ys, w, layer, x, mod, g, row_of_tile, tm):
    n_tok = x.shape[0]
    n_parts = len(ys)
    width = ys[0].shape[1]
    in_specs = [pl.BlockSpec((tm, width), lambda i: (i, 0)) for _ in ys]
    in_specs += [pl.BlockSpec((None, width, D_MODEL), lambda i, k=k: (layer, k, 0))
                 for k in range(n_parts)]
    in_specs += [
        pl.BlockSpec((tm, D_MODEL), lambda i: (i, 0)),
        pl.BlockSpec((1, 3, D_MODEL), lambda i: (row_of_tile(i), 0, 0)),
        pl.BlockSpec((1, D_MODEL), lambda i: (0, 0)),
    ]
    return pl.pallas_call(
        functools.partial(_out_kernel, n_parts=n_parts),
        grid=(n_tok // tm,),
        in_specs=in_specs,
        out_specs=pl.BlockSpec((tm, D_MODEL), lambda i: (i, 0)),
        out_shape=jax.ShapeDtypeStruct((n_tok, D_MODEL), F32),
        compiler_params=_cparams(1, 40),
        name="out_projection",
    )(*ys, *([w] * n_parts), x, mod, g.reshape(1, D_MODEL))


def _layer_c_kernel(x_ref, mod_ref, gpre_ref, win_ref, lng_ref, lnb_ref, ws_ref, bs_ref, wout_ref,
                    gpost_ref, o_ref, y_scr, *, tm):
    x = x_ref[...]
    mod = mod_ref[0]
    h = _modulated_norm(x, mod, gpre_ref[...]).astype(BF16)

    v = jnp.dot(h, win_ref[:, MIX_WIDTH:2 * MIX_WIDTH], preferred_element_type=F32)
    mu = jnp.mean(v, axis=-1, keepdims=True)
    vc = v - mu
    var = jnp.mean(vc * vc, axis=-1, keepdims=True)
    vn = (vc * lax.rsqrt(var + EPS) * lng_ref[...] + lnb_ref[...]).astype(BF16)

    for gi in range(N_GROUPS_C):
        cols = slice(gi * GROUP_W_C, (gi + 1) * GROUP_W_C)
        u = jnp.dot(h, win_ref[:, cols], preferred_element_type=F32)
        gate = _silu(jnp.dot(h, win_ref[:, 2 * MIX_WIDTH + gi * GROUP_W_C:
                                        2 * MIX_WIDTH + (gi + 1) * GROUP_W_C],
                             preferred_element_type=F32))
        for n in range(tm // CHUNK_C):
            rows = slice(n * CHUNK_C, (n + 1) * CHUNK_C)
            sp = jnp.dot(ws_ref[gi], vn[rows, cols], preferred_element_type=F32) + bs_ref[:, gi:gi + 1]
            y_scr[rows, cols] = (u[rows] * sp * gate[rows]).astype(BF16)

    acc = jnp.dot(y_scr[...], wout_ref[...], preferred_element_type=F32)
    o_ref[...] = x + mod[2:3] * _rms(acc, gpost_ref[...])


def _layer_c(x, mod, g_pre, g_post, w_in, w_out, layer, ln_g, ln_b, w_s, b_s, row_of_tile, tm):
    n_tok = x.shape[0]
    return pl.pallas_call(
        functools.partial(_layer_c_kernel, tm=tm),
        grid=(n_tok // tm,),
        in_specs=[
            pl.BlockSpec((tm, D_MODEL), lambda i: (i, 0)),
            pl.BlockSpec((1, 3, D_MODEL), lambda i: (row_of_tile(i), 0, 0)),
            _resident((1, D_MODEL)),
            _resident((D_MODEL, C_IN), layer),
            _resident((1, MIX_WIDTH)),
            _resident((1, MIX_WIDTH)),
            _resident((N_GROUPS_C, CHUNK_C, CHUNK_C)),
            _resident((CHUNK_C, N_GROUPS_C)),
            _resident((MIX_WIDTH, D_MODEL), layer),
            _resident((1, D_MODEL)),
        ],
        out_specs=pl.BlockSpec((tm, D_MODEL), lambda i: (i, 0)),
        out_shape=jax.ShapeDtypeStruct((n_tok, D_MODEL), F32),
        scratch_shapes=[pltpu.VMEM((tm, MIX_WIDTH), BF16)],
        compiler_params=_cparams(1, 48),
        name="gmlp_layer",
    )(x, mod, g_pre.reshape(1, D_MODEL), w_in, ln_g.reshape(1, MIX_WIDTH), ln_b.reshape(1, MIX_WIDTH),
      w_s.astype(BF16), b_s.T, w_out, g_post.reshape(1, D_MODEL))


def _window_counts(n, w):
    return [min(i - w // 2 + w, n) - max(i - w // 2, 0) for i in range(n)]


def _mean_minor(x3, w):
    g, n, ch = x3.shape
    idx = lax.broadcasted_iota(jnp.int32, (1, n, ch), 1)

    def shifted(a, k):
        return pltpu.roll(a.reshape(g * n, ch), k % (g * n), 0).reshape(g, n, ch)

    trail, lead = x3, x3
    size = 1
    while size < w // 2:
        trail = trail + jnp.where(idx >= size, shifted(trail, size), 0.0)
        lead = lead + jnp.where(idx + size < n, shifted(lead, -size), 0.0)
        size *= 2
    s = jnp.where(idx >= 1, shifted(trail, 1), 0.0) + lead
    half = w // 2
    cnt = jnp.minimum(idx - half + w, n) - jnp.maximum(idx - half, 0)
    return s * (1.0 / cnt.astype(F32))


def _mean_major(x3, w):
    n = x3.shape[0]
    zeros = lambda k: jnp.zeros((k,) + x3.shape[1:], F32)
    back = lambda a, k: jnp.concatenate([zeros(k), a[:n - k]], axis=0)
    ahead = lambda a, k: jnp.concatenate([a[k:], zeros(k)], axis=0)
    trail, lead = x3, x3
    size = 1
    while size < w // 2:
        trail = trail + back(trail, size)
        lead = lead + ahead(lead, size)
        size *= 2
    s = back(trail, 1) + lead
    return jnp.concatenate(
        [s[i:i + 1] * (1.0 / cnt) for i, cnt in enumerate(_window_counts(n, w))], axis=0)


def _pool_kernel(p_ref, gate_ref, wp_ref, ps_ref, y_ref, *, seq_len, grid_mode):
    grp = pl.program_id(1)
    rows, ch = p_ref.shape
    for k, w in enumerate(POOL_WINDOWS):
        @pl.when(grp == k)
        def _(w=w):
            x = p_ref[...].astype(F32)
            if grid_mode:
                m = _mean_major(x.reshape(rows // GRID_W, GRID_W, ch), w)
                m = _mean_minor(m, w).reshape(rows, ch)
            else:
                m = _mean_minor(x.reshape(rows // seq_len, seq_len, ch), w).reshape(rows, ch)
            dlt = (m - x).astype(BF16)
            y = jnp.dot(dlt, wp_ref[0], preferred_element_type=F32) * ps_ref[...]
            y_ref[...] = (y * _silu(gate_ref[...].astype(F32))).astype(BF16)


def _pool_mixer(proj, w_pool, pool_scale, seq_len, grid_mode, rows_per_step):
    n_tok = proj.shape[0]
    n_grp = len(POOL_WINDOWS)
    p_blk0 = AB_PART["pool_in"] * BRANCH_W // POOL_GROUP_W
    g_blk0 = AB_PART["gate_b"] * BRANCH_W // POOL_GROUP_W
    return pl.pallas_call(
        functools.partial(_pool_kernel, seq_len=seq_len, grid_mode=grid_mode),
        grid=(n_tok // rows_per_step, n_grp),
        in_specs=[
            pl.BlockSpec((rows_per_step, POOL_GROUP_W), lambda b, g: (b, p_blk0 + g)),
            pl.BlockSpec((rows_per_step, POOL_GROUP_W), lambda b, g: (b, g_blk0 + g)),
            pl.BlockSpec((1, POOL_GROUP_W, POOL_GROUP_W), lambda b, g: (g, 0, 0)),
            pl.BlockSpec((1, POOL_GROUP_W), lambda b, g: (0, g)),
        ],
        out_specs=pl.BlockSpec((rows_per_step, POOL_GROUP_W), lambda b, g: (b, g)),
        out_shape=jax.ShapeDtypeStruct((n_tok, BRANCH_W), BF16),
        compiler_params=_cparams(2, 56),
        name="pool_mixer",
    )(proj, proj, w_pool.astype(BF16), pool_scale.reshape(1, BRANCH_W))


def _hgrn_triangles():
    ones = np.ones((HGRN_CHUNK, HGRN_CHUNK), np.float32)
    return np.concatenate([np.tril(ones), np.triu(ones)], axis=0)


def _hgrn_partial_sum_matrix():
    c = HGRN_CHUNK
    blocks = []
    for m in HGRN_PSUM_LEVELS:
        mq = np.zeros((c, 2 * c), np.float32)
        mk = np.zeros((c, 2 * c), np.float32)
        for i in range(c):
            start = (i // (2 * m)) * (2 * m)
            r1, r2 = start + m - 1, start + m
            if i >= r2:
                mq[i, r2:i + 1] = 1
                mk[i, c + r2:c + i] = 1
            else:
                mq[i, c + i:c + r1 + 1] = 1
                mk[i, i + 1:r1 + 1] = 1
        blocks += [mq, mk]
    return np.concatenate(blocks, axis=0)


def _hgrn_level_index():
    i = np.arange(HGRN_CHUNK)
    x = i[:, None] ^ i[None, :]
    lv = np.zeros_like(x)
    nz = x > 0
    lv[nz] = np.floor(np.log2(x[nz])).astype(x.dtype) + 1
    return lv.astype(np.int32)


_HGRN_TRI = _hgrn_triangles()
_HGRN_PSUM = _hgrn_partial_sum_matrix()
_HGRN_LEVEL = _hgrn_level_index()

_NT = (((1,), (1,)), ((), ()))
_TN = (((0,), (0,)), ((), ()))


def _hgrn_kernel(*refs, n_chunks, heads, has_s0, has_prev, want_final, final_own):
    q_ref, ff_ref, fb_ref, v_ref, ga_ref, lb_ref, gon_ref, tri_ref, psum_ref, lvl_ref = refs[:10]
    pos = 10
    s0_ref = sfin_ref = None
    if has_s0:
        s0_ref = refs[pos]
        pos += 1
    if has_prev:
        pos += 1
    y_ref = refs[pos]
    pos += 1
    if want_final:
        sfin_ref = refs[pos]
        pos += 1
    o_scr, qfb_scr, dst_scr, dec_scr, st_scr = refs[pos:]

    c = HGRN_CHUNK
    inline = heads * n_chunks <= 8
    level = lvl_ref[...]
    row8 = jnp.concatenate([lax.broadcasted_iota(jnp.int32, (8, HEAD_D), 0)] * (c // 8), axis=0)

    def chunk_rows(ci):
        if isinstance(ci, int):
            return pl.ds(ci * c, c)
        return pl.ds(pl.multiple_of(ci * c, c), c)

    def over_chunks(body, init, unroll):
        if inline:
            carry = init
            for ci in range(n_chunks):
                carry = body(ci, carry)
            return carry
        return lax.fori_loop(0, n_chunks, body, init, unroll=unroll)

    def over_heads_and_chunks(fn, in_flight):
        def body(ci, carry):
            for hh in range(heads):
                fn(hh, ci, carry)
            return carry
        over_chunks(body, 0, unroll=max(1, min(n_chunks, in_flight // heads)))

    def hi_mid(x):
        hi = x.astype(BF16)
        return hi, (x - hi.astype(F32)).astype(BF16)

    def summed(zero_one, hi, mid):
        dd = jnp.dot(zero_one, jnp.concatenate([hi, mid], axis=1), preferred_element_type=F32)
        return dd[:, :HEAD_D] + dd[:, HEAD_D:]

    def local_pass(hh, ci, carry):
        rows = chunk_rows(ci)
        lanes = slice(hh * HEAD_D, (hh + 1) * HEAD_D)
        lb_f, lb_b = lb_ref[0:1, lanes], lb_ref[1:2, lanes]
        f_f = lb_f + (1.0 - lb_f) * jax.nn.sigmoid(ff_ref[rows, lanes])
        f_b = lb_b + (1.0 - lb_b) * jax.nn.sigmoid(fb_ref[rows, lanes])
        k_f, k_b = 1.0 - f_f, 1.0 - f_b
        hi_f, mid_f = hi_mid(jnp.log2(f_f))
        hi_b, mid_b = hi_mid(jnp.log2(f_b))
        b_f = summed(tri_ref[0:c, :], hi_f, mid_f)
        b_b = summed(tri_ref[c:2 * c, :], hi_b, mid_b)
        small = summed(psum_ref[...], jnp.concatenate([hi_f, hi_b], axis=0),
                       jnp.concatenate([mid_f, mid_b], axis=0))

        q = _silu(q_ref[rows, lanes].astype(F32))
        v = v_ref[rows, lanes]

        def level_scores(d_q, d_k, k_sel):
            qt = (q * jnp.exp2(d_q)).astype(BF16)
            kt = (k_sel * jnp.exp2(d_k)).astype(BF16)
            return lax.dot_general(qt, kt, _NT, preferred_element_type=F32)

        s = lax.dot_general(q.astype(BF16), (k_f + k_b).astype(BF16), _NT,
                            preferred_element_type=F32)
        s = jnp.where(level == 0, s, 0.0)
        odd = (row8 & 1) == 1
        q1 = (q * jnp.where(odd, f_f, f_b)).astype(BF16)
        k1 = jnp.where(odd, k_b, k_f).astype(BF16)
        s = jnp.where(level == 1, lax.dot_general(q1, k1, _NT, preferred_element_type=F32), s)
        for li, m in enumerate(HGRN_PSUM_LEVELS):
            second = (row8 & m) == m
            s_l = level_scores(small[2 * li * c:(2 * li + 1) * c], small[(2 * li + 1) * c:(2 * li + 2) * c],
                               jnp.where(second, k_b, k_f))
            s = jnp.where(level == m.bit_length(), s_l, s)
        for m in HGRN_LEVELS:
            dq_parts, dk_parts, k_parts = [], [], []
            for lo in range(0, c, 2 * m):
                mid_row, hi_row = lo + m, lo + 2 * m
                edge_f = b_f[mid_row - 1:mid_row]
                edge_b = b_b[mid_row:mid_row + 1]
                dq_parts += [b_b[lo:mid_row] - edge_b, b_f[mid_row:hi_row] - edge_f]
                dk_parts += [edge_f - b_f[lo:mid_row], edge_b - b_b[mid_row:hi_row]]
                k_parts += [k_f[lo:mid_row], k_b[mid_row:hi_row]]
            s_l = level_scores(jnp.concatenate(dq_parts, axis=0), jnp.concatenate(dk_parts, axis=0),
                               jnp.concatenate(k_parts, axis=0))
            s = jnp.where(level == m.bit_length(), s_l, s)

        q_fb = jnp.concatenate([q * jnp.exp2(b_f), q * jnp.exp2(b_b)], axis=1).astype(BF16)
        k_fb = jnp.concatenate([k_f * jnp.exp2(b_f[c - 1:c] - b_f),
                                k_b * jnp.exp2(b_b[0:1] - b_b)], axis=1).astype(BF16)
        dec = jnp.concatenate([jnp.exp2(b_f[c - 1:c]), jnp.exp2(b_b[0:1])], axis=1)

        o_scr[hh, rows, :] = jnp.dot(s.astype(BF16), v, preferred_element_type=F32)
        qfb_scr[hh, rows, :] = q_fb
        dst_scr[hh, ci] = lax.dot_general(v, k_fb, _TN, preferred_element_type=F32)
        dec_scr[hh, ci] = jnp.broadcast_to(dec, (8, 2 * HEAD_D))
        return carry

    over_heads_and_chunks(local_pass, 16)

    for hh in range(heads):
        if has_s0:
            starts = (s0_ref[0, 0, 0, hh].T, s0_ref[0, 0, 1, hh].T)
        else:
            starts = (jnp.zeros((HEAD_D, HEAD_D), F32),) * 2

        def scan_step(k, states, hh=hh):
            out = []
            for d, ci in ((0, k), (1, n_chunks - 1 - k)):
                lanes = slice(d * HEAD_D, (d + 1) * HEAD_D)
                st_scr[hh, ci, :, lanes] = states[d].astype(BF16)
                out.append(states[d] * dec_scr[hh, ci, 0:1, lanes] + dst_scr[hh, ci, :, lanes])
            return tuple(out)

        st_f, st_b = over_chunks(scan_step, starts, unroll=2)
        if want_final:
            sfin_ref[0, final_own, 0, hh] = st_f.T
            sfin_ref[0, final_own, 1, hh] = st_b.T
            for other in range(sfin_ref.shape[1]):
                if other != final_own:
                    sfin_ref[0, other, :, hh] = jnp.zeros((2, HEAD_D, HEAD_D), F32)

    def output_pass(hh, ci, carry):
        rows = chunk_rows(ci)
        lanes = slice(hh * HEAD_D, (hh + 1) * HEAD_D)
        o = o_scr[hh, rows, :] + lax.dot_general(qfb_scr[hh, rows, :], st_scr[hh, ci], _NT,
                                                 preferred_element_type=F32)
        y_ref[rows, lanes] = (_rms(o, gon_ref[...])
                              * _silu(ga_ref[rows, lanes].astype(F32))).astype(BF16)
        return carry

    over_heads_and_chunks(output_pass, 8)


def _hgrn_mixer(gates, proj, lb, g_onorm, state, layer_j, seq_len, heads, want_final,
                final_prev=None):
    n_tok = proj.shape[0]
    n_seq = n_tok // seq_len
    n_chunks = seq_len // HGRN_CHUNK
    has_s0 = state is not None
    has_prev = final_prev is not None
    width = heads * HEAD_D
    blk = lambda part: pl.BlockSpec(
        (seq_len, width), lambda b, h: (b, part * (N_HEADS // heads) + h))
    in_specs = [blk(AB_PART["q"]), blk(0), blk(1), blk(AB_PART["i"]), blk(AB_PART["gate_a"]),
                pl.BlockSpec((2, width), lambda b, h: (0, h)),
                pl.BlockSpec((1, HEAD_D), lambda b, h: (0, 0)),
                pl.BlockSpec(_HGRN_TRI.shape, lambda b, h: (0, 0)),
                pl.BlockSpec(_HGRN_PSUM.shape, lambda b, h: (0, 0)),
                pl.BlockSpec(_HGRN_LEVEL.shape, lambda b, h: (0, 0))]
    args = [proj, gates, gates, proj, proj, lb, g_onorm.reshape(1, HEAD_D),
            jnp.asarray(_HGRN_TRI, BF16), jnp.asarray(_HGRN_PSUM, BF16), jnp.asarray(_HGRN_LEVEL)]
    state_blk = pl.BlockSpec((1, 1, 2, heads, HEAD_D, HEAD_D), lambda b, h: (b, layer_j, 0, h, 0, 0))
    if has_s0:
        in_specs.append(state_blk)
        args.append(state)
    aliases = {}
    if has_prev:
        aliases[len(args)] = 1
        in_specs.append(pl.BlockSpec(memory_space=pl.ANY))
        args.append(final_prev)
    out_specs = [pl.BlockSpec((seq_len, width), lambda b, h: (b, h))]
    out_shape = [jax.ShapeDtypeStruct((n_tok, BRANCH_W), BF16)]
    if want_final:
        out_specs.append(state_blk if has_prev else pl.BlockSpec(
            (1, N_AB, 2, heads, HEAD_D, HEAD_D), lambda b, h: (b, 0, 0, h, 0, 0)))
        out_shape.append(jax.ShapeDtypeStruct((n_seq, N_AB, 2, N_HEADS, HEAD_D, HEAD_D), F32))
    outs = pl.pallas_call(
        functools.partial(_hgrn_kernel, n_chunks=n_chunks, heads=heads, has_s0=has_s0,
                          has_prev=has_prev, want_final=want_final,
                          final_own=0 if has_prev else layer_j),
        grid=(n_seq, N_HEADS // heads),
        in_specs=in_specs,
        out_specs=out_specs,
        out_shape=out_shape,
        input_output_aliases=aliases,
        scratch_shapes=[
            pltpu.VMEM((heads, seq_len, HEAD_D), F32),
            pltpu.VMEM((heads, seq_len, 2 * HEAD_D), BF16),
            pltpu.VMEM((heads, n_chunks, HEAD_D, 2 * HEAD_D), F32),
            pltpu.VMEM((heads, n_chunks, 8, 2 * HEAD_D), F32),
            pltpu.VMEM((heads, n_chunks, HEAD_D, 2 * HEAD_D), BF16),
        ],
        compiler_params=_cparams(2, 48),
        name="hgrn_mixer",
    )(*args)
    return outs if want_final else (outs[0], None)


def kernel(x_prompt, x_sample, c, state_hgrn, c_ctx, w_ada, b_ada, g_pre, g_post, w_in_ab, w_out_ab,
           lb_logits, g_onorm_a, w_pool, pool_scale, w_in_c, w_out_c, ln_v_g, ln_v_b, w_spatial,
           b_spatial):
    n_batch, seq_p, _ = x_prompt.shape
    dec_batch, seq_s, _ = x_sample.shape
    tm_in, tm_out, tn, tm_c, pool_rows = 1024, 512, BRANCH_W, 512, 2048

    cond = jnp.concatenate(
        [c_ctx[None, :], c, jnp.zeros((N_COND_ROWS - 1 - dec_batch, D_MODEL), F32)], axis=0)
    mods = _modulations(cond, w_ada, b_ada).reshape(DEPTH, N_COND_ROWS, 3, D_MODEL)
    lb_all = _lower_bounds(lb_logits)

    flows = [
        dict(x=x_prompt.reshape(n_batch * seq_p, D_MODEL), seq=seq_p, grid=False, state=None,
             row=lambda tm: (lambda i: 0)),
        dict(x=x_sample.reshape(dec_batch * seq_s, D_MODEL), seq=seq_s, grid=True, state=state_hgrn,
             row=lambda tm: (lambda i: 1 + i // (seq_s // tm))),
    ]
    w_in_ab, w_out_ab, w_in_c, w_out_c = (
        w.astype(BF16) for w in (w_in_ab, w_out_ab, w_in_c, w_out_c))
    new_state = None
    for l in range(DEPTH):
        j = l // 2
        if l % 2 == 0:
            per_part = BRANCH_W // tn
            n_f32 = AB_F32_PARTS * per_part
            reorder = lambda t: jnp.where(t < n_f32, t + per_part,
                                          jnp.where(t < n_f32 + per_part, t - n_f32, t))
            for fi, fl in enumerate(flows):
                gates, proj = _in_proj(fl["x"], mods[l], g_pre[l], w_in_ab, j, fl["row"](tm_in), tm_in,
                                       tn, n_f32_tiles=n_f32, w_tile_of=reorder)
                heads = max(2, min(N_HEADS, 8 * HGRN_CHUNK // fl["seq"]))
                if fi == 0:
                    y_a, new_state = _hgrn_mixer(gates, proj, lb_all[j], g_onorm_a[j], None, j,
                                                 fl["seq"], heads, want_final=True,
                                                 final_prev=new_state)
                else:
                    y_a, _ = _hgrn_mixer(gates, proj, lb_all[j], g_onorm_a[j], fl["state"], j,
                                         fl["seq"], heads, want_final=False)
                y_b = _pool_mixer(proj, w_pool[j], pool_scale[j], fl["seq"], fl["grid"], pool_rows)
                fl["x"] = _out_proj([y_a, y_b], w_out_ab, j, fl["x"], mods[l], g_post[l],
                                    fl["row"](tm_out), tm_out)
        else:
            for fl in flows:
                fl["x"] = _layer_c(fl["x"], mods[l], g_pre[l], g_post[l], w_in_c, w_out_c, j,
                                   ln_v_g[j], ln_v_b[j], w_spatial[j], b_spatial[j],
                                   fl["row"](tm_c), tm_c)
    y_p = flows[0]["x"].reshape(n_batch, seq_p, D_MODEL)
    y_s = flows[1]["x"].reshape(dec_batch, seq_s, D_MODEL)
    return (y_p, y_s, new_state)
```

```python
import functools

import numpy as np
import jax
import jax.numpy as jnp
from jax import lax
from jax.experimental import pallas as pl
from jax.experimental.pallas import tpu as pltpu

F32 = jnp.float32
BF16 = jnp.bfloat16

D_MODEL = 1024
DEPTH = 4
N_AB = (DEPTH + 1) // 2
MIX_WIDTH = 2 * D_MODEL
BRANCH_W = MIX_WIDTH // 2
HEAD_D = 128
N_HEADS = BRANCH_W // HEAD_D
GRID_W = 64
POOL_WINDOWS = (2, 4, 8, 16)
POOL_GROUP_W = BRANCH_W // len(POOL_WINDOWS)
CHUNK_C = 128
N_GROUPS_C = 8
GROUP_W_C = MIX_WIDTH // N_GROUPS_C
AB_IN = 7 * BRANCH_W
C_IN = 3 * MIX_WIDTH
AB_F32_PARTS = 2
AB_PART = {"q": 0, "i": 1, "gate_a": 2, "pool_in": 3, "gate_b": 4}
EPS = 1e-6
N_COND_ROWS = 8

HGRN_CHUNK = 128
HGRN_LEVELS = (64, 32, 16, 8)
HGRN_PSUM_LEVELS = (2, 4)
MIB = 1024 * 1024


def _cparams(n_axes, vmem_mib):
    return pltpu.CompilerParams(
        dimension_semantics=("arbitrary",) * n_axes, vmem_limit_bytes=int(vmem_mib * MIB))


def _resident(shape, layer=None):
    zeros = (0,) * len(shape)
    if layer is None:
        return pl.BlockSpec(shape, lambda *_: zeros, pipeline_mode=pl.Buffered(1))
    return pl.BlockSpec((None,) + tuple(shape), lambda *_: (layer,) + zeros,
                        pipeline_mode=pl.Buffered(1))


def _silu(x):
    return x * jax.nn.sigmoid(x)


def _rms(x, g):
    return x * lax.rsqrt(jnp.mean(x * x, axis=-1, keepdims=True) + EPS) * g


def _modulated_norm(x, mod, g):
    return _rms(x, g) * (1.0 + mod[1:2]) + mod[0:1]


def _mod_kernel(cond_ref, w_ref, b_ref, o_ref):
    a = _silu(cond_ref[...])
    o_ref[0] = jnp.dot(a.astype(BF16), w_ref[0].astype(BF16), preferred_element_type=F32) + b_ref[0]


def _modulations(cond, w_ada, b_ada):
    tn = 1024
    n3 = 3 * D_MODEL
    return pl.pallas_call(
        _mod_kernel,
        grid=(DEPTH, n3 // tn),
        in_specs=[
            pl.BlockSpec((N_COND_ROWS, D_MODEL), lambda l, j: (0, 0)),
            pl.BlockSpec((1, D_MODEL, tn), lambda l, j: (l, 0, j)),
            pl.BlockSpec((1, 1, tn), lambda l, j: (l, 0, j)),
        ],
        out_specs=pl.BlockSpec((1, N_COND_ROWS, tn), lambda l, j: (l, 0, j)),
        out_shape=jax.ShapeDtypeStruct((DEPTH, N_COND_ROWS, n3), F32),
        compiler_params=_cparams(2, 24),
        name="adaln_modulation",
    )(cond, w_ada, b_ada.reshape(DEPTH, 1, n3))


def _lb_kernel(x_ref, o_ref):
    x = x_ref[...]
    e = jnp.exp(x - jnp.max(x, axis=0, keepdims=True))
    p = e / jnp.sum(e, axis=0, keepdims=True)
    run = p[0]
    o_ref[0] = run - p[0]
    for l in range(1, x.shape[0]):
        run = run + p[l]
        o_ref[l] = run - p[0]


def _lower_bounds(lb_logits):
    return pl.pallas_call(
        _lb_kernel, out_shape=jax.ShapeDtypeStruct(lb_logits.shape, F32), name="hgrn_lower_bounds",
    )(lb_logits)


def _in_kernel(x_ref, mod_ref, g_ref, w_ref, *rest, n_f32_tiles):
    out_refs, h_ref = rest[:-1], rest[-1]
    j = pl.program_id(1)

    @pl.when(j == 0)
    def _():
        h_ref[...] = _modulated_norm(x_ref[...], mod_ref[0], g_ref[...]).astype(BF16)

    def project(o_ref):
        o_ref[...] = jnp.dot(h_ref[...], w_ref[...], preferred_element_type=F32).astype(o_ref.dtype)

    if n_f32_tiles:
        pl.when(j < n_f32_tiles)(lambda: project(out_refs[0]))
        pl.when(j >= n_f32_tiles)(lambda: project(out_refs[1]))
    else:
        project(out_refs[0])


def _in_proj(x, mod, g, w, layer, row_of_tile, tm, tn, n_f32_tiles=0, w_tile_of=lambda j: j):
    n_tok, n_out = x.shape[0], w.shape[2]
    n_tiles = n_out // tn
    out_specs, out_shape = [], []
    if n_f32_tiles:
        out_specs.append(pl.BlockSpec((tm, tn), lambda i, j: (i, jnp.minimum(j, n_f32_tiles - 1))))
        out_shape.append(jax.ShapeDtypeStruct((n_tok, n_f32_tiles * tn), F32))
    out_specs.append(pl.BlockSpec((tm, tn), lambda i, j: (i, jnp.maximum(j - n_f32_tiles, 0))))
    out_shape.append(jax.ShapeDtypeStruct((n_tok, (n_tiles - n_f32_tiles) * tn), BF16))
    return pl.pallas_call(
        functools.partial(_in_kernel, n_f32_tiles=n_f32_tiles),
        grid=(n_tok // tm, n_tiles),
        in_specs=[
            pl.BlockSpec((tm, D_MODEL), lambda i, j: (i, 0)),
            pl.BlockSpec((1, 3, D_MODEL), lambda i, j: (row_of_tile(i), 0, 0)),
            pl.BlockSpec((1, D_MODEL), lambda i, j: (0, 0)),
            pl.BlockSpec((None, D_MODEL, tn), lambda i, j: (layer, 0, w_tile_of(j))),
        ],
        out_specs=out_specs,
        out_shape=out_shape,
        scratch_shapes=[pltpu.VMEM((tm, D_MODEL), BF16)],
        compiler_params=_cparams(2, 40),
        name="in_projection",
    )(x, mod, g.reshape(1, D_MODEL), w)


def _out_kernel(*refs, n_parts):
    y_refs, w_refs = refs[:n_parts], refs[n_parts:2 * n_parts]
    x_ref, mod_ref, g_ref, o_ref = refs[2 * n_parts:]
    acc = jnp.dot(y_refs[0][...], w_refs[0][...], preferred_element_type=F32)
    for y_ref, w_ref in zip(y_refs[1:], w_refs[1:]):
        acc = acc + jnp.dot(y_ref[...], w_ref[...], preferred_element_type=F32)
    o_ref[...] = x_ref[...] + mod_ref[0][2:3] * _rms(acc, g_ref[...])


def _out_proj(ys, w, layer, x, mod, g, row_of_tile, tm):
    n_tok = x.shape[0]
    n_parts = len(ys)
    width = ys[0].shape[1]
    in_specs = [pl.BlockSpec((tm, width), lambda i: (i, 0)) for _ in ys]
    in_specs += [pl.BlockSpec((None, width, D_MODEL), lambda i, k=k: (layer, k, 0))
                 for k in range(n_parts)]
    in_specs += [
        pl.BlockSpec((tm, D_MODEL), lambda i: (i, 0)),
        pl.BlockSpec((1, 3, D_MODEL), lambda i: (row_of_tile(i), 0, 0)),
        pl.BlockSpec((1, D_MODEL), lambda i: (0, 0)),
    ]
    return pl.pallas_call(
        functools.partial(_out_kernel, n_parts=n_parts),
        grid=(n_tok // tm,),
        in_specs=in_specs,
        out_specs=pl.BlockSpec((tm, D_MODEL), lambda i: (i, 0)),
        out_shape=jax.ShapeDtypeStruct((n_tok, D_MODEL), F32),
        compiler_params=_cparams(1, 40),
        name="out_projection",
    )(*ys, *([w] * n_parts), x, mod, g.reshape(1, D_MODEL))


def _layer_c_kernel(x_ref, mod_ref, gpre_ref, win_ref, lng_ref, lnb_ref, ws_ref, bs_ref, wout_ref,
                    gpost_ref, o_ref, y_scr, *, tm):
    x = x_ref[...]
    mod = mod_ref[0]
    h = _modulated_norm(x, mod, gpre_ref[...]).astype(BF16)

    v = jnp.dot(h, win_ref[:, MIX_WIDTH:2 * MIX_WIDTH], preferred_element_type=F32)
    mu = jnp.mean(v, axis=-1, keepdims=True)
    vc = v - mu
    var = jnp.mean(vc * vc, axis=-1, keepdims=True)
    vn = (vc * lax.rsqrt(var + EPS) * lng_ref[...] + lnb_ref[...]).astype(BF16)

    for gi in range(N_GROUPS_C):
        cols = slice(gi * GROUP_W_C, (gi + 1) * GROUP_W_C)
        u = jnp.dot(h, win_ref[:, cols], preferred_element_type=F32)
        gate = _silu(jnp.dot(h, win_ref[:, 2 * MIX_WIDTH + gi * GROUP_W_C:
                                        2 * MIX_WIDTH + (gi + 1) * GROUP_W_C],
                             preferred_element_type=F32))
        for n in range(tm // CHUNK_C):
            rows = slice(n * CHUNK_C, (n + 1) * CHUNK_C)
            sp = jnp.dot(ws_ref[gi], vn[rows, cols], preferred_element_type=F32) + bs_ref[:, gi:gi + 1]
            y_scr[rows, cols] = (u[rows] * sp * gate[rows]).astype(BF16)

    acc = jnp.dot(y_scr[...], wout_ref[...], preferred_element_type=F32)
    o_ref[...] = x + mod[2:3] * _rms(acc, gpost_ref[...])


def _layer_c(x, mod, g_pre, g_post, w_in, w_out, layer, ln_g, ln_b, w_s, b_s, row_of_tile, tm):
    n_tok = x.shape[0]
    return pl.pallas_call(
        functools.partial(_layer_c_kernel, tm=tm),
        grid=(n_tok // tm,),
        in_specs=[
            pl.BlockSpec((tm, D_MODEL), lambda i: (i, 0)),
            pl.BlockSpec((1, 3, D_MODEL), lambda i: (row_of_tile(i), 0, 0)),
            _resident((1, D_MODEL)),
            _resident((D_MODEL, C_IN), layer),
            _resident((1, MIX_WIDTH)),
            _resident((1, MIX_WIDTH)),
            _resident((N_GROUPS_C, CHUNK_C, CHUNK_C)),
            _resident((CHUNK_C, N_GROUPS_C)),
            _resident((MIX_WIDTH, D_MODEL), layer),
            _resident((1, D_MODEL)),
        ],
        out_specs=pl.BlockSpec((tm, D_MODEL), lambda i: (i, 0)),
        out_shape=jax.ShapeDtypeStruct((n_tok, D_MODEL), F32),
        scratch_shapes=[pltpu.VMEM((tm, MIX_WIDTH), BF16)],
        compiler_params=_cparams(1, 48),
        name="gmlp_layer",
    )(x, mod, g_pre.reshape(1, D_MODEL), w_in, ln_g.reshape(1, MIX_WIDTH), ln_b.reshape(1, MIX_WIDTH),
      w_s.astype(BF16), b_s.T, w_out, g_post.reshape(1, D_MODEL))


def _window_counts(n, w):
    return [min(i - w // 2 + w, n) - max(i - w // 2, 0) for i in range(n)]


def _window_bands(n):
    bands = np.zeros((len(POOL_WINDOWS), n, n), np.float32)
    for k, w in enumerate(POOL_WINDOWS):
        for i in range(n):
            bands[k, i, max(i - w // 2, 0):min(i - w // 2 + w, n)] = 1
    return bands


def _mean_minor(x, band, inv_count):
    n = band.shape[0]
    ch = x.shape[1]
    groups = x.shape[0] // n
    wide = jnp.concatenate([x[i * n:(i + 1) * n, :] for i in range(groups)], axis=1)
    sums = jnp.dot(band, wide, preferred_element_type=F32)
    return [sums[:, i * ch:(i + 1) * ch] * inv_count for i in range(groups)]


def _mean_major(x3, w):
    n = x3.shape[0]
    zeros = lambda k: jnp.zeros((k,) + x3.shape[1:], F32)
    back = lambda a, k: jnp.concatenate([zeros(k), a[:n - k]], axis=0)
    ahead = lambda a, k: jnp.concatenate([a[k:], zeros(k)], axis=0)
    trail, lead = x3, x3
    size = 1
    while size < w // 2:
        trail = trail + back(trail, size)
        lead = lead + ahead(lead, size)
        size *= 2
    s = back(trail, 1) + lead
    return jnp.concatenate(
        [s[i:i + 1] * (1.0 / cnt) for i, cnt in enumerate(_window_counts(n, w))], axis=0)


def _pool_kernel(p_ref, gate_ref, band_ref, inv_ref, wp_ref, ps_ref, y_ref, *, grid_mode):
    grp = pl.program_id(1)
    rows, ch = p_ref.shape
    means = _mean_minor(p_ref[...], band_ref[0], inv_ref[0])

    def finish(m):
        dlt = (m - p_ref[...].astype(F32)).astype(BF16)
        y = jnp.dot(dlt, wp_ref[0], preferred_element_type=F32) * ps_ref[...]
        y_ref[...] = (y * _silu(gate_ref[...].astype(F32))).astype(BF16)

    if grid_mode:
        m3 = jnp.stack(means, axis=0)
        for k, w in enumerate(POOL_WINDOWS):
            pl.when(grp == k)(lambda w=w: finish(_mean_major(m3, w).reshape(rows, ch)))
    else:
        finish(jnp.concatenate(means, axis=0))


def _pool_mixer(proj, w_pool, pool_scale, seq_len, grid_mode, rows_per_step):
    n_tok = proj.shape[0]
    n_grp = len(POOL_WINDOWS)
    p_blk0 = AB_PART["pool_in"] * BRANCH_W // POOL_GROUP_W
    g_blk0 = AB_PART["gate_b"] * BRANCH_W // POOL_GROUP_W
    n = GRID_W if grid_mode else seq_len
    assert not grid_mode or rows_per_step == seq_len
    bands = _window_bands(n)
    inv_count = np.broadcast_to(1.0 / bands.sum(axis=2, keepdims=True), (n_grp, n, POOL_GROUP_W))
    return pl.pallas_call(
        functools.partial(_pool_kernel, grid_mode=grid_mode),
        grid=(n_tok // rows_per_step, n_grp),
        in_specs=[
            pl.BlockSpec((rows_per_step, POOL_GROUP_W), lambda b, g: (b, p_blk0 + g)),
            pl.BlockSpec((rows_per_step, POOL_GROUP_W), lambda b, g: (b, g_blk0 + g)),
            pl.BlockSpec((1, n, n), lambda b, g: (g, 0, 0)),
            pl.BlockSpec((1, n, POOL_GROUP_W), lambda b, g: (g, 0, 0)),
            pl.BlockSpec((1, POOL_GROUP_W, POOL_GROUP_W), lambda b, g: (g, 0, 0)),
            pl.BlockSpec((1, POOL_GROUP_W), lambda b, g: (0, g)),
        ],
        out_specs=pl.BlockSpec((rows_per_step, POOL_GROUP_W), lambda b, g: (b, g)),
        out_shape=jax.ShapeDtypeStruct((n_tok, BRANCH_W), BF16),
        compiler_params=_cparams(2, 56),
        name="pool_mixer",
    )(proj, proj, jnp.asarray(bands, BF16), jnp.asarray(inv_count, F32), w_pool.astype(BF16),
      pool_scale.reshape(1, BRANCH_W))


def _hgrn_triangles():
    ones = np.ones((HGRN_CHUNK, HGRN_CHUNK), np.float32)
    return np.concatenate([np.tril(ones), np.triu(ones)], axis=0)


def _hgrn_partial_sum_matrix():
    c = HGRN_CHUNK
    blocks = []
    for m in HGRN_PSUM_LEVELS:
        mq = np.zeros((c, 2 * c), np.float32)
        mk = np.zeros((c, 2 * c), np.float32)
        for i in range(c):
            start = (i // (2 * m)) * (2 * m)
            r1, r2 = start + m - 1, start + m
            if i >= r2:
                mq[i, r2:i + 1] = 1
                mk[i, c + r2:c + i] = 1
            else:
                mq[i, c + i:c + r1 + 1] = 1
                mk[i, i + 1:r1 + 1] = 1
        blocks += [mq, mk]
    return np.concatenate(blocks, axis=0)


def _hgrn_level_index():
    i = np.arange(HGRN_CHUNK)
    x = i[:, None] ^ i[None, :]
    lv = np.zeros_like(x)
    nz = x > 0
    lv[nz] = np.floor(np.log2(x[nz])).astype(x.dtype) + 1
    return lv.astype(np.int32)


_HGRN_TRI = _hgrn_triangles()
_HGRN_PSUM = _hgrn_partial_sum_matrix()
_HGRN_LEVEL = _hgrn_level_index()

_NT = (((1,), (1,)), ((), ()))
_TN = (((0,), (0,)), ((), ()))


def _hgrn_kernel(*refs, n_chunks, heads, has_s0, has_prev, want_final, final_own):
    q_ref, ff_ref, fb_ref, v_ref, ga_ref, lb_ref, gon_ref, tri_ref, psum_ref, lvl_ref = refs[:10]
    pos = 10
    s0_ref = sfin_ref = None
    if has_s0:
        s0_ref = refs[pos]
        pos += 1
    if has_prev:
        pos += 1
    y_ref = refs[pos]
    pos += 1
    if want_final:
        sfin_ref = refs[pos]
        pos += 1
    o_scr, qfb_scr, dst_scr, dec_scr, st_scr = refs[pos:]

    c = HGRN_CHUNK
    inline = heads * n_chunks <= 8
    level = lvl_ref[...]
    row8 = jnp.concatenate([lax.broadcasted_iota(jnp.int32, (8, HEAD_D), 0)] * (c // 8), axis=0)

    def chunk_rows(ci):
        if isinstance(ci, int):
            return pl.ds(ci * c, c)
        return pl.ds(pl.multiple_of(ci * c, c), c)

    def over_chunks(body, init, unroll):
        if inline:
            carry = init
            for ci in range(n_chunks):
                carry = body(ci, carry)
            return carry
        return lax.fori_loop(0, n_chunks, body, init, unroll=unroll)

    def over_heads_and_chunks(fn, in_flight):
        def body(ci, carry):
            for hh in range(heads):
                fn(hh, ci, carry)
            return carry
        over_chunks(body, 0, unroll=max(1, min(n_chunks, in_flight // heads)))

    def hi_mid(x):
        hi = x.astype(BF16)
        return hi, (x - hi.astype(F32)).astype(BF16)

    def summed(zero_one, hi, mid):
        dd = jnp.dot(zero_one, jnp.concatenate([hi, mid], axis=1), preferred_element_type=F32)
        return dd[:, :HEAD_D] + dd[:, HEAD_D:]

    def local_pass(hh, ci, carry):
        rows = chunk_rows(ci)
        lanes = slice(hh * HEAD_D, (hh + 1) * HEAD_D)
        lb_f, lb_b = lb_ref[0:1, lanes], lb_ref[1:2, lanes]
        f_f = lb_f + (1.0 - lb_f) * jax.nn.sigmoid(ff_ref[rows, lanes])
        f_b = lb_b + (1.0 - lb_b) * jax.nn.sigmoid(fb_ref[rows, lanes])
        k_f, k_b = 1.0 - f_f, 1.0 - f_b
        hi_f, mid_f = hi_mid(jnp.log2(f_f))
        hi_b, mid_b = hi_mid(jnp.log2(f_b))
        b_f = summed(tri_ref[0:c, :], hi_f, mid_f)
        b_b = summed(tri_ref[c:2 * c, :], hi_b, mid_b)
        small = summed(psum_ref[...], jnp.concatenate([hi_f, hi_b], axis=0),
                       jnp.concatenate([mid_f, mid_b], axis=0))

        q = _silu(q_ref[rows, lanes].astype(F32))
        v = v_ref[rows, lanes]

        def level_scores(d_q, d_k, k_sel):
            qt = (q * jnp.exp2(d_q)).astype(BF16)
            kt = (k_sel * jnp.exp2(d_k)).astype(BF16)
            return lax.dot_general(qt, kt, _NT, preferred_element_type=F32)

        s = lax.dot_general(q.astype(BF16), (k_f + k_b).astype(BF16), _NT,
                            preferred_element_type=F32)
        s = jnp.where(level == 0, s, 0.0)
        odd = (row8 & 1) == 1
        q1 = (q * jnp.where(odd, f_f, f_b)).astype(BF16)
        k1 = jnp.where(odd, k_b, k_f).astype(BF16)
        s = jnp.where(level == 1, lax.dot_general(q1, k1, _NT, preferred_element_type=F32), s)
        for li, m in enumerate(HGRN_PSUM_LEVELS):
            second = (row8 & m) == m
            s_l = level_scores(small[2 * li * c:(2 * li + 1) * c], small[(2 * li + 1) * c:(2 * li + 2) * c],
                               jnp.where(second, k_b, k_f))
            s = jnp.where(level == m.bit_length(), s_l, s)
        for m in HGRN_LEVELS:
            dq_parts, dk_parts, k_parts = [], [], []
            for lo in range(0, c, 2 * m):
                mid_row, hi_row = lo + m, lo + 2 * m
                edge_f = b_f[mid_row - 1:mid_row]
                edge_b = b_b[mid_row:mid_row + 1]
                dq_parts += [b_b[lo:mid_row] - edge_b, b_f[mid_row:hi_row] - edge_f]
                dk_parts += [edge_f - b_f[lo:mid_row], edge_b - b_b[mid_row:hi_row]]
                k_parts += [k_f[lo:mid_row], k_b[mid_row:hi_row]]
            s_l = level_scores(jnp.concatenate(dq_parts, axis=0), jnp.concatenate(dk_parts, axis=0),
                               jnp.concatenate(k_parts, axis=0))
            s = jnp.where(level == m.bit_length(), s_l, s)

        q_fb = jnp.concatenate([q * jnp.exp2(b_f), q * jnp.exp2(b_b)], axis=1).astype(BF16)
        k_fb = jnp.concatenate([k_f * jnp.exp2(b_f[c - 1:c] - b_f),
                                k_b * jnp.exp2(b_b[0:1] - b_b)], axis=1).astype(BF16)
        dec = jnp.concatenate([jnp.exp2(b_f[c - 1:c]), jnp.exp2(b_b[0:1])], axis=1)

        o_scr[hh, rows, :] = jnp.dot(s.astype(BF16), v, preferred_element_type=F32)
        qfb_scr[hh, rows, :] = q_fb
        dst_scr[hh, ci] = lax.dot_general(v, k_fb, _TN, preferred_element_type=F32)
        dec_scr[hh, ci] = jnp.broadcast_to(dec, (8, 2 * HEAD_D))
        return carry

    over_heads_and_chunks(local_pass, 16)

    for hh in range(heads):
        if has_s0:
            starts = (s0_ref[0, 0, 0, hh].T, s0_ref[0, 0, 1, hh].T)
        else:
            starts = (jnp.zeros((HEAD_D, HEAD_D), F32),) * 2

        def scan_step(k, states, hh=hh):
            out = []
            for d, ci in ((0, k), (1, n_chunks - 1 - k)):
                lanes = slice(d * HEAD_D, (d + 1) * HEAD_D)
                st_scr[hh, ci, :, lanes] = states[d].astype(BF16)
                out.append(states[d] * dec_scr[hh, ci, 0:1, lanes] + dst_scr[hh, ci, :, lanes])
            return tuple(out)

        st_f, st_b = over_chunks(scan_step, starts, unroll=2)
        if want_final:
            sfin_ref[0, final_own, 0, hh] = st_f.T
            sfin_ref[0, final_own, 1, hh] = st_b.T
            for other in range(sfin_ref.shape[1]):
                if other != final_own:
                    sfin_ref[0, other, :, hh] = jnp.zeros((2, HEAD_D, HEAD_D), F32)

    def output_pass(hh, ci, carry):
        rows = chunk_rows(ci)
        lanes = slice(hh * HEAD_D, (hh + 1) * HEAD_D)
        o = o_scr[hh, rows, :] + lax.dot_general(qfb_scr[hh, rows, :], st_scr[hh, ci], _NT,
                                                 preferred_element_type=F32)
        y_ref[rows, lanes] = (_rms(o, gon_ref[...])
                              * _silu(ga_ref[rows, lanes].astype(F32))).astype(BF16)
        return carry

    over_heads_and_chunks(output_pass, 8)


def _hgrn_mixer(gates, proj, lb, g_onorm, state, layer_j, seq_len, heads, want_final,
                final_prev=None):
    n_tok = proj.shape[0]
    n_seq = n_tok // seq_len
    n_chunks = seq_len // HGRN_CHUNK
    has_s0 = state is not None
    has_prev = final_prev is not None
    width = heads * HEAD_D
    blk = lambda part: pl.BlockSpec(
        (seq_len, width), lambda b, h: (b, part * (N_HEADS // heads) + h))
    in_specs = [blk(AB_PART["q"]), blk(0), blk(1), blk(AB_PART["i"]), blk(AB_PART["gate_a"]),
                pl.BlockSpec((2, width), lambda b, h: (0, h)),
                pl.BlockSpec((1, HEAD_D), lambda b, h: (0, 0)),
                pl.BlockSpec(_HGRN_TRI.shape, lambda b, h: (0, 0)),
                pl.BlockSpec(_HGRN_PSUM.shape, lambda b, h: (0, 0)),
                pl.BlockSpec(_HGRN_LEVEL.shape, lambda b, h: (0, 0))]
    args = [proj, gates, gates, proj, proj, lb, g_onorm.reshape(1, HEAD_D),
            jnp.asarray(_HGRN_TRI, BF16), jnp.asarray(_HGRN_PSUM, BF16), jnp.asarray(_HGRN_LEVEL)]
    state_blk = pl.BlockSpec((1, 1, 2, heads, HEAD_D, HEAD_D), lambda b, h: (b, layer_j, 0, h, 0, 0))
    if has_s0:
        in_specs.append(state_blk)
        args.append(state)
    aliases = {}
    if has_prev:
        aliases[len(args)] = 1
        in_specs.append(pl.BlockSpec(memory_space=pl.ANY))
        args.append(final_prev)
    out_specs = [pl.BlockSpec((seq_len, width), lambda b, h: (b, h))]
    out_shape = [jax.ShapeDtypeStruct((n_tok, BRANCH_W), BF16)]
    if want_final:
        out_specs.append(state_blk if has_prev else pl.BlockSpec(
            (1, N_AB, 2, heads, HEAD_D, HEAD_D), lambda b, h: (b, 0, 0, h, 0, 0)))
        out_shape.append(jax.ShapeDtypeStruct((n_seq, N_AB, 2, N_HEADS, HEAD_D, HEAD_D), F32))
    outs = pl.pallas_call(
        functools.partial(_hgrn_kernel, n_chunks=n_chunks, heads=heads, has_s0=has_s0,
                          has_prev=has_prev, want_final=want_final,
                          final_own=0 if has_prev else layer_j),
        grid=(n_seq, N_HEADS // heads),
        in_specs=in_specs,
        out_specs=out_specs,
        out_shape=out_shape,
        input_output_aliases=aliases,
        scratch_shapes=[
            pltpu.VMEM((heads, seq_len, HEAD_D), F32),
            pltpu.VMEM((heads, seq_len, 2 * HEAD_D), BF16),
            pltpu.VMEM((heads, n_chunks, HEAD_D, 2 * HEAD_D), F32),
            pltpu.VMEM((heads, n_chunks, 8, 2 * HEAD_D), F32),
            pltpu.VMEM((heads, n_chunks, HEAD_D, 2 * HEAD_D), BF16),
        ],
        compiler_params=_cparams(2, 48),
        name="hgrn_mixer",
    )(*args)
    return outs if want_final else (outs[0], None)


def kernel(x_prompt, x_sample, c, state_hgrn, c_ctx, w_ada, b_ada, g_pre, g_post, w_in_ab, w_out_ab,
           lb_logits, g_onorm_a, w_pool, pool_scale, w_in_c, w_out_c, ln_v_g, ln_v_b, w_spatial,
           b_spatial):
    n_batch, seq_p, _ = x_prompt.shape
    dec_batch, seq_s, _ = x_sample.shape
    tm_in, tm_out, tn, tm_c, pool_rows = 1024, 512, BRANCH_W, 512, 2048

    cond = jnp.concatenate(
        [c_ctx[None, :], c, jnp.zeros((N_COND_ROWS - 1 - dec_batch, D_MODEL), F32)], axis=0)
    mods = _modulations(cond, w_ada, b_ada).reshape(DEPTH, N_COND_ROWS, 3, D_MODEL)
    lb_all = _lower_bounds(lb_logits)

    flows = [
        dict(x=x_prompt.reshape(n_batch * seq_p, D_MODEL), seq=seq_p, grid=False, state=None,
             row=lambda tm: (lambda i: 0)),
        dict(x=x_sample.reshape(dec_batch * seq_s, D_MODEL), seq=seq_s, grid=True, state=state_hgrn,
             row=lambda tm: (lambda i: 1 + i // (seq_s // tm))),
    ]
    w_in_ab, w_out_ab, w_in_c, w_out_c = (
        w.astype(BF16) for w in (w_in_ab, w_out_ab, w_in_c, w_out_c))
    new_state = None
    for l in range(DEPTH):
        j = l // 2
        if l % 2 == 0:
            per_part = BRANCH_W // tn
            n_f32 = AB_F32_PARTS * per_part
            reorder = lambda t: jnp.where(t < n_f32, t + per_part,
                                          jnp.where(t < n_f32 + per_part, t - n_f32, t))
            for fi, fl in enumerate(flows):
                gates, proj = _in_proj(fl["x"], mods[l], g_pre[l], w_in_ab, j, fl["row"](tm_in), tm_in,
                                       tn, n_f32_tiles=n_f32, w_tile_of=reorder)
                heads = max(2, min(N_HEADS, 8 * HGRN_CHUNK // fl["seq"]))
                if fi == 0:
                    y_a, new_state = _hgrn_mixer(gates, proj, lb_all[j], g_onorm_a[j], None, j,
                                                 fl["seq"], heads, want_final=True,
                                                 final_prev=new_state)
                else:
                    y_a, _ = _hgrn_mixer(gates, proj, lb_all[j], g_onorm_a[j], fl["state"], j,
                                         fl["seq"], heads, want_final=False)
                y_b = _pool_mixer(proj, w_pool[j], pool_scale[j], fl["seq"], fl["grid"], pool_rows)
                fl["x"] = _out_proj([y_a, y_b], w_out_ab, j, fl["x"], mods[l], g_post[l],
                                    fl["row"](tm_out), tm_out)
        else:
            for fl in flows:
                fl["x"] = _layer_c(fl["x"], mods[l], g_pre[l], g_post[l], w_in_c, w_out_c, j,
                                   ln_v_g[j], ln_v_b[j], w_spatial[j], b_spatial[j],
                                   fl["row"](tm_c), tm_c)
    y_p = flows[0]["x"].reshape(n_batch, seq_p, D_MODEL)
    y_s = flows[1]["x"].reshape(dec_batch, seq_s, D_MODEL)
    return (y_p, y_s, new_state)
```

```python
import functools

import numpy as np
import jax
import jax.numpy as jnp
from jax import lax
from jax.experimental import pallas as pl
from jax.experimental.pallas import tpu as pltpu

F32 = jnp.float32
BF16 = jnp.bfloat16

D_MODEL = 1024
DEPTH = 4
N_AB = (DEPTH + 1) // 2
MIX_WIDTH = 2 * D_MODEL
BRANCH_W = MIX_WIDTH // 2
HEAD_D = 128
N_HEADS = BRANCH_W // HEAD_D
GRID_W = 64
POOL_WINDOWS = (2, 4, 8, 16)
POOL_GROUP_W = BRANCH_W // len(POOL_WINDOWS)
CHUNK_C = 128
N_GROUPS_C = 8
GROUP_W_C = MIX_WIDTH // N_GROUPS_C
AB_IN = 7 * BRANCH_W
C_IN = 3 * MIX_WIDTH
AB_F32_PARTS = 2
AB_PART = {"q": 0, "i": 1, "gate_a": 2, "pool_in": 3, "gate_b": 4}
EPS = 1e-6
N_COND_ROWS = 8

HGRN_CHUNK = 128
HGRN_LEVELS = (64, 32, 16, 8)
HGRN_PSUM_LEVELS = (2, 4)
MIB = 1024 * 1024


def _cparams(n_axes, vmem_mib):
    return pltpu.CompilerParams(
        dimension_semantics=("arbitrary",) * n_axes, vmem_limit_bytes=int(vmem_mib * MIB))


def _resident(shape, layer=None):
    zeros = (0,) * len(shape)
    if layer is None:
        return pl.BlockSpec(shape, lambda *_: zeros, pipeline_mode=pl.Buffered(1))
    return pl.BlockSpec((None,) + tuple(shape), lambda *_: (layer,) + zeros,
                        pipeline_mode=pl.Buffered(1))


def _silu(x):
    return x * jax.nn.sigmoid(x)


def _rms(x, g):
    return x * lax.rsqrt(jnp.mean(x * x, axis=-1, keepdims=True) + EPS) * g


def _modulated_norm(x, mod, g):
    return _rms(x, g) * (1.0 + mod[1:2]) + mod[0:1]


def _mod_kernel(cond_ref, w_ref, b_ref, o_ref):
    a = _silu(cond_ref[...])
    o_ref[0] = jnp.dot(a.astype(BF16), w_ref[0].astype(BF16), preferred_element_type=F32) + b_ref[0]


def _modulations(cond, w_ada, b_ada):
    tn = 1024
    n3 = 3 * D_MODEL
    return pl.pallas_call(
        _mod_kernel,
        grid=(DEPTH, n3 // tn),
        in_specs=[
            pl.BlockSpec((N_COND_ROWS, D_MODEL), lambda l, j: (0, 0)),
            pl.BlockSpec((1, D_MODEL, tn), lambda l, j: (l, 0, j)),
            pl.BlockSpec((1, 1, tn), lambda l, j: (l, 0, j)),
        ],
        out_specs=pl.BlockSpec((1, N_COND_ROWS, tn), lambda l, j: (l, 0, j)),
        out_shape=jax.ShapeDtypeStruct((DEPTH, N_COND_ROWS, n3), F32),
        compiler_params=_cparams(2, 24),
        name="adaln_modulation",
    )(cond, w_ada, b_ada.reshape(DEPTH, 1, n3))


def _lb_kernel(x_ref, o_ref):
    x = x_ref[...]
    e = jnp.exp(x - jnp.max(x, axis=0, keepdims=True))
    p = e / jnp.sum(e, axis=0, keepdims=True)
    run = p[0]
    o_ref[0] = run - p[0]
    for l in range(1, x.shape[0]):
        run = run + p[l]
        o_ref[l] = run - p[0]


def _lower_bounds(lb_logits):
    return pl.pallas_call(
        _lb_kernel, out_shape=jax.ShapeDtypeStruct(lb_logits.shape, F32), name="hgrn_lower_bounds",
    )(lb_logits)


def _in_kernel(x_ref, mod_ref, g_ref, w_ref, gates_ref, proj_ref):
    h = _modulated_norm(x_ref[...], mod_ref[0], g_ref[...]).astype(BF16)

    def part(k):
        return jnp.dot(h, w_ref[:, k * BRANCH_W:(k + 1) * BRANCH_W], preferred_element_type=F32)

    for slot, k in enumerate((1, 2)):
        gates_ref[:, slot * BRANCH_W:(slot + 1) * BRANCH_W] = part(k)
    for slot, k in enumerate((0, 3, 4, 5, 6)):
        proj_ref[:, slot * BRANCH_W:(slot + 1) * BRANCH_W] = part(k).astype(BF16)


def _in_proj(x, mod, g, w, layer, row_of_tile, tm):
    n_tok = x.shape[0]
    n_gate, n_proj = AB_F32_PARTS * BRANCH_W, AB_IN - AB_F32_PARTS * BRANCH_W
    return pl.pallas_call(
        _in_kernel,
        grid=(n_tok // tm,),
        in_specs=[
            pl.BlockSpec((tm, D_MODEL), lambda i: (i, 0)),
            pl.BlockSpec((1, 3, D_MODEL), lambda i: (row_of_tile(i), 0, 0)),
            _resident((1, D_MODEL)),
            _resident((D_MODEL, AB_IN), layer),
        ],
        out_specs=[pl.BlockSpec((tm, n_gate), lambda i: (i, 0)),
                   pl.BlockSpec((tm, n_proj), lambda i: (i, 0))],
        out_shape=[jax.ShapeDtypeStruct((n_tok, n_gate), F32),
                   jax.ShapeDtypeStruct((n_tok, n_proj), BF16)],
        compiler_params=_cparams(1, 56),
        name="in_projection",
    )(x, mod, g.reshape(1, D_MODEL), w)


def _out_kernel(*refs, n_parts):
    y_refs, w_refs = refs[:n_parts], refs[n_parts:2 * n_parts]
    x_ref, mod_ref, g_ref, o_ref = refs[2 * n_parts:]
    acc = jnp.dot(y_refs[0][...], w_refs[0][...], preferred_element_type=F32)
    for y_ref, w_ref in zip(y_refs[1:], w_refs[1:]):
        acc = acc + jnp.dot(y_ref[...], w_ref[...], preferred_element_type=F32)
    o_ref[...] = x_ref[...] + mod_ref[0][2:3] * _rms(acc, g_ref[...])


def _out_proj(ys, w, layer, x, mod, g, row_of_tile, tm):
    n_tok = x.shape[0]
    n_parts = len(ys)
    width = ys[0].shape[1]
    in_specs = [pl.BlockSpec((tm, width), lambda i: (i, 0)) for _ in ys]
    in_specs += [pl.BlockSpec((None, width, D_MODEL), lambda i, k=k: (layer, k, 0))
                 for k in range(n_parts)]
    in_specs += [
        pl.BlockSpec((tm, D_MODEL), lambda i: (i, 0)),
        pl.BlockSpec((1, 3, D_MODEL), lambda i: (row_of_tile(i), 0, 0)),
        pl.BlockSpec((1, D_MODEL), lambda i: (0, 0)),
    ]
    return pl.pallas_call(
        functools.partial(_out_kernel, n_parts=n_parts),
        grid=(n_tok // tm,),
        in_specs=in_specs,
        out_specs=pl.BlockSpec((tm, D_MODEL), lambda i: (i, 0)),
        out_shape=jax.ShapeDtypeStruct((n_tok, D_MODEL), F32),
        compiler_params=_cparams(1, 40),
        name="out_projection",
    )(*ys, *([w] * n_parts), x, mod, g.reshape(1, D_MODEL))


def _layer_c_kernel(x_ref, mod_ref, gpre_ref, win_ref, lng_ref, lnb_ref, ws_ref, bs_ref, wout_ref,
                    gpost_ref, o_ref, y_scr, *, tm):
    x = x_ref[...]
    mod = mod_ref[0]
    h = _modulated_norm(x, mod, gpre_ref[...]).astype(BF16)

    v = jnp.dot(h, win_ref[:, MIX_WIDTH:2 * MIX_WIDTH], preferred_element_type=F32)
    mu = jnp.mean(v, axis=-1, keepdims=True)
    vc = v - mu
    var = jnp.mean(vc * vc, axis=-1, keepdims=True)
    vn = (vc * lax.rsqrt(var + EPS) * lng_ref[...] + lnb_ref[...]).astype(BF16)

    for gi in range(N_GROUPS_C):
        cols = slice(gi * GROUP_W_C, (gi + 1) * GROUP_W_C)
        u = jnp.dot(h, win_ref[:, cols], preferred_element_type=F32)
        gate = _silu(jnp.dot(h, win_ref[:, 2 * MIX_WIDTH + gi * GROUP_W_C:
                                        2 * MIX_WIDTH + (gi + 1) * GROUP_W_C],
                             preferred_element_type=F32))
        for n in range(tm // CHUNK_C):
            rows = slice(n * CHUNK_C, (n + 1) * CHUNK_C)
            sp = jnp.dot(ws_ref[gi], vn[rows, cols], preferred_element_type=F32) + bs_ref[:, gi:gi + 1]
            y_scr[rows, cols] = (u[rows] * sp * gate[rows]).astype(BF16)

    acc = jnp.dot(y_scr[...], wout_ref[...], preferred_element_type=F32)
    o_ref[...] = x + mod[2:3] * _rms(acc, gpost_ref[...])


def _layer_c(x, mod, g_pre, g_post, w_in, w_out, layer, ln_g, ln_b, w_s, b_s, row_of_tile, tm):
    n_tok = x.shape[0]
    return pl.pallas_call(
        functools.partial(_layer_c_kernel, tm=tm),
        grid=(n_tok // tm,),
        in_specs=[
            pl.BlockSpec((tm, D_MODEL), lambda i: (i, 0)),
            pl.BlockSpec((1, 3, D_MODEL), lambda i: (row_of_tile(i), 0, 0)),
            _resident((1, D_MODEL)),
            _resident((D_MODEL, C_IN), layer),
            _resident((1, MIX_WIDTH)),
            _resident((1, MIX_WIDTH)),
            _resident((N_GROUPS_C, CHUNK_C, CHUNK_C)),
            _resident((CHUNK_C, N_GROUPS_C)),
            _resident((MIX_WIDTH, D_MODEL), layer),
            _resident((1, D_MODEL)),
        ],
        out_specs=pl.BlockSpec((tm, D_MODEL), lambda i: (i, 0)),
        out_shape=jax.ShapeDtypeStruct((n_tok, D_MODEL), F32),
        scratch_shapes=[pltpu.VMEM((tm, MIX_WIDTH), BF16)],
        compiler_params=_cparams(1, 48),
        name="gmlp_layer",
    )(x, mod, g_pre.reshape(1, D_MODEL), w_in, ln_g.reshape(1, MIX_WIDTH), ln_b.reshape(1, MIX_WIDTH),
      w_s.astype(BF16), b_s.T, w_out, g_post.reshape(1, D_MODEL))


def _window_counts(n, w):
    return [min(i - w // 2 + w, n) - max(i - w // 2, 0) for i in range(n)]


def _window_bands(n):
    bands = np.zeros((len(POOL_WINDOWS), n, n), np.float32)
    for k, w in enumerate(POOL_WINDOWS):
        for i in range(n):
            bands[k, i, max(i - w // 2, 0):min(i - w // 2 + w, n)] = 1
    return bands


def _mean_minor(x, band, inv_count):
    n = band.shape[0]
    ch = x.shape[1]
    groups = x.shape[0] // n
    wide = jnp.concatenate([x[i * n:(i + 1) * n, :] for i in range(groups)], axis=1)
    sums = jnp.dot(band, wide, preferred_element_type=F32)
    return [sums[:, i * ch:(i + 1) * ch] * inv_count for i in range(groups)]


def _mean_major(x3, w):
    n = x3.shape[0]
    zeros = lambda k: jnp.zeros((k,) + x3.shape[1:], F32)
    back = lambda a, k: jnp.concatenate([zeros(k), a[:n - k]], axis=0)
    ahead = lambda a, k: jnp.concatenate([a[k:], zeros(k)], axis=0)
    trail, lead = x3, x3
    size = 1
    while size < w // 2:
        trail = trail + back(trail, size)
        lead = lead + ahead(lead, size)
        size *= 2
    s = back(trail, 1) + lead
    return jnp.concatenate(
        [s[i:i + 1] * (1.0 / cnt) for i, cnt in enumerate(_window_counts(n, w))], axis=0)


def _pool_kernel(p_ref, gate_ref, band_ref, inv_ref, wp_ref, ps_ref, y_ref, *, grid_mode):
    grp = pl.program_id(1)
    rows, ch = p_ref.shape
    means = _mean_minor(p_ref[...], band_ref[0], inv_ref[0])

    def finish(m):
        dlt = (m - p_ref[...].astype(F32)).astype(BF16)
        y = jnp.dot(dlt, wp_ref[0], preferred_element_type=F32) * ps_ref[...]
        y_ref[...] = (y * _silu(gate_ref[...].astype(F32))).astype(BF16)

    if grid_mode:
        m3 = jnp.stack(means, axis=0)
        for k, w in enumerate(POOL_WINDOWS):
            pl.when(grp == k)(lambda w=w: finish(_mean_major(m3, w).reshape(rows, ch)))
    else:
        finish(jnp.concatenate(means, axis=0))


def _pool_mixer(proj, w_pool, pool_scale, seq_len, grid_mode, rows_per_step):
    n_tok = proj.shape[0]
    n_grp = len(POOL_WINDOWS)
    p_blk0 = AB_PART["pool_in"] * BRANCH_W // POOL_GROUP_W
    g_blk0 = AB_PART["gate_b"] * BRANCH_W // POOL_GROUP_W
    n = GRID_W if grid_mode else seq_len
    assert not grid_mode or rows_per_step == seq_len
    bands = _window_bands(n)
    inv_count = np.broadcast_to(1.0 / bands.sum(axis=2, keepdims=True), (n_grp, n, POOL_GROUP_W))
    return pl.pallas_call(
        functools.partial(_pool_kernel, grid_mode=grid_mode),
        grid=(n_tok // rows_per_step, n_grp),
        in_specs=[
            pl.BlockSpec((rows_per_step, POOL_GROUP_W), lambda b, g: (b, p_blk0 + g)),
            pl.BlockSpec((rows_per_step, POOL_GROUP_W), lambda b, g: (b, g_blk0 + g)),
            pl.BlockSpec((1, n, n), lambda b, g: (g, 0, 0)),
            pl.BlockSpec((1, n, POOL_GROUP_W), lambda b, g: (g, 0, 0)),
            pl.BlockSpec((1, POOL_GROUP_W, POOL_GROUP_W), lambda b, g: (g, 0, 0)),
            pl.BlockSpec((1, POOL_GROUP_W), lambda b, g: (0, g)),
        ],
        out_specs=pl.BlockSpec((rows_per_step, POOL_GROUP_W), lambda b, g: (b, g)),
        out_shape=jax.ShapeDtypeStruct((n_tok, BRANCH_W), BF16),
        compiler_params=_cparams(2, 56),
        name="pool_mixer",
    )(proj, proj, jnp.asarray(bands, BF16), jnp.asarray(inv_count, F32), w_pool.astype(BF16),
      pool_scale.reshape(1, BRANCH_W))


def _hgrn_triangles():
    ones = np.ones((HGRN_CHUNK, HGRN_CHUNK), np.float32)
    return np.concatenate([np.tril(ones), np.triu(ones)], axis=0)


def _hgrn_partial_sum_matrix():
    c = HGRN_CHUNK
    blocks = []
    for m in HGRN_PSUM_LEVELS:
        mq = np.zeros((c, 2 * c), np.float32)
        mk = np.zeros((c, 2 * c), np.float32)
        for i in range(c):
            start = (i // (2 * m)) * (2 * m)
            r1, r2 = start + m - 1, start + m
            if i >= r2:
                mq[i, r2:i + 1] = 1
                mk[i, c + r2:c + i] = 1
            else:
                mq[i, c + i:c + r1 + 1] = 1
                mk[i, i + 1:r1 + 1] = 1
        blocks += [mq, mk]
    return np.concatenate(blocks, axis=0)


def _hgrn_level_index():
    i = np.arange(HGRN_CHUNK)
    x = i[:, None] ^ i[None, :]
    lv = np.zeros_like(x)
    nz = x > 0
    lv[nz] = np.floor(np.log2(x[nz])).astype(x.dtype) + 1
    return lv.astype(np.int32)


_HGRN_TRI = _hgrn_triangles()
_HGRN_PSUM = _hgrn_partial_sum_matrix()
_HGRN_LEVEL = _hgrn_level_index()

_NT = (((1,), (1,)), ((), ()))
_TN = (((0,), (0,)), ((), ()))


def _hgrn_kernel(*refs, n_chunks, heads, has_s0, has_prev, want_final, final_own):
    q_ref, ff_ref, fb_ref, v_ref, ga_ref, lb_ref, gon_ref, tri_ref, psum_ref, lvl_ref = refs[:10]
    pos = 10
    s0_ref = sfin_ref = None
    if has_s0:
        s0_ref = refs[pos]
        pos += 1
    if has_prev:
        pos += 1
    y_ref = refs[pos]
    pos += 1
    if want_final:
        sfin_ref = refs[pos]
        pos += 1
    o_scr, qfb_scr, dst_scr, dec_scr, st_scr = refs[pos:]

    c = HGRN_CHUNK
    inline = heads * n_chunks <= 8
    level = lvl_ref[...]
    row8 = jnp.concatenate([lax.broadcasted_iota(jnp.int32, (8, HEAD_D), 0)] * (c // 8), axis=0)

    def chunk_rows(ci):
        if isinstance(ci, int):
            return pl.ds(ci * c, c)
        return pl.ds(pl.multiple_of(ci * c, c), c)

    def over_chunks(body, init, unroll):
        if inline:
            carry = init
            for ci in range(n_chunks):
                carry = body(ci, carry)
            return carry
        return lax.fori_loop(0, n_chunks, body, init, unroll=unroll)

    def over_heads_and_chunks(fn, in_flight):
        def body(ci, carry):
            for hh in range(heads):
                fn(hh, ci, carry)
            return carry
        over_chunks(body, 0, unroll=max(1, min(n_chunks, in_flight // heads)))

    def hi_mid(x):
        hi = x.astype(BF16)
        return hi, (x - hi.astype(F32)).astype(BF16)

    def summed(zero_one, hi, mid):
        dd = jnp.dot(zero_one, jnp.concatenate([hi, mid], axis=1), preferred_element_type=F32)
        return dd[:, :HEAD_D] + dd[:, HEAD_D:]

    def local_pass(hh, ci, carry):
        rows = chunk_rows(ci)
        lanes = slice(hh * HEAD_D, (hh + 1) * HEAD_D)
        lb_f, lb_b = lb_ref[0:1, lanes], lb_ref[1:2, lanes]
        f_f = lb_f + (1.0 - lb_f) * jax.nn.sigmoid(ff_ref[rows, lanes])
        f_b = lb_b + (1.0 - lb_b) * jax.nn.sigmoid(fb_ref[rows, lanes])
        k_f, k_b = 1.0 - f_f, 1.0 - f_b
        hi_f, mid_f = hi_mid(jnp.log2(f_f))
        hi_b, mid_b = hi_mid(jnp.log2(f_b))
        b_f = summed(tri_ref[0:c, :], hi_f, mid_f)
        b_b = summed(tri_ref[c:2 * c, :], hi_b, mid_b)
        small = summed(psum_ref[...], jnp.concatenate([hi_f, hi_b], axis=0),
                       jnp.concatenate([mid_f, mid_b], axis=0))

        q = _silu(q_ref[rows, lanes].astype(F32))
        v = v_ref[rows, lanes]

        def level_scores(d_q, d_k, k_sel):
            qt = (q * jnp.exp2(d_q)).astype(BF16)
            kt = (k_sel * jnp.exp2(d_k)).astype(BF16)
            return lax.dot_general(qt, kt, _NT, preferred_element_type=F32)

        s = lax.dot_general(q.astype(BF16), (k_f + k_b).astype(BF16), _NT,
                            preferred_element_type=F32)
        s = jnp.where(level == 0, s, 0.0)
        odd = (row8 & 1) == 1
        q1 = (q * jnp.where(odd, f_f, f_b)).astype(BF16)
        k1 = jnp.where(odd, k_b, k_f).astype(BF16)
        s = jnp.where(level == 1, lax.dot_general(q1, k1, _NT, preferred_element_type=F32), s)
        for li, m in enumerate(HGRN_PSUM_LEVELS):
            second = (row8 & m) == m
            s_l = level_scores(small[2 * li * c:(2 * li + 1) * c], small[(2 * li + 1) * c:(2 * li + 2) * c],
                               jnp.where(second, k_b, k_f))
            s = jnp.where(level == m.bit_length(), s_l, s)
        for m in HGRN_LEVELS:
            dq_parts, dk_parts, k_parts = [], [], []
            for lo in range(0, c, 2 * m):
                mid_row, hi_row = lo + m, lo + 2 * m
                edge_f = b_f[mid_row - 1:mid_row]
                edge_b = b_b[mid_row:mid_row + 1]
                dq_parts += [b_b[lo:mid_row] - edge_b, b_f[mid_row:hi_row] - edge_f]
                dk_parts += [edge_f - b_f[lo:mid_row], edge_b - b_b[mid_row:hi_row]]
                k_parts += [k_f[lo:mid_row], k_b[mid_row:hi_row]]
            s_l = level_scores(jnp.concatenate(dq_parts, axis=0), jnp.concatenate(dk_parts, axis=0),
                               jnp.concatenate(k_parts, axis=0))
            s = jnp.where(level == m.bit_length(), s_l, s)

        q_fb = jnp.concatenate([q * jnp.exp2(b_f), q * jnp.exp2(b_b)], axis=1).astype(BF16)
        k_fb = jnp.concatenate([k_f * jnp.exp2(b_f[c - 1:c] - b_f),
                                k_b * jnp.exp2(b_b[0:1] - b_b)], axis=1).astype(BF16)
        dec = jnp.concatenate([jnp.exp2(b_f[c - 1:c]), jnp.exp2(b_b[0:1])], axis=1)

        o_scr[hh, rows, :] = jnp.dot(s.astype(BF16), v, preferred_element_type=F32)
        qfb_scr[hh, rows, :] = q_fb
        dst_scr[hh, ci] = lax.dot_general(v, k_fb, _TN, preferred_element_type=F32)
        dec_scr[hh, ci] = jnp.broadcast_to(dec, (8, 2 * HEAD_D))
        return carry

    over_heads_and_chunks(local_pass, 16)

    for hh in range(heads):
        if has_s0:
            starts = (s0_ref[0, 0, 0, hh].T, s0_ref[0, 0, 1, hh].T)
        else:
            starts = (jnp.zeros((HEAD_D, HEAD_D), F32),) * 2

        def scan_step(k, states, hh=hh):
            out = []
            for d, ci in ((0, k), (1, n_chunks - 1 - k)):
                lanes = slice(d * HEAD_D, (d + 1) * HEAD_D)
                st_scr[hh, ci, :, lanes] = states[d].astype(BF16)
                out.append(states[d] * dec_scr[hh, ci, 0:1, lanes] + dst_scr[hh, ci, :, lanes])
            return tuple(out)

        st_f, st_b = over_chunks(scan_step, starts, unroll=2)
        if want_final:
            sfin_ref[0, final_own, 0, hh] = st_f.T
            sfin_ref[0, final_own, 1, hh] = st_b.T
            for other in range(sfin_ref.shape[1]):
                if other != final_own:
                    sfin_ref[0, other, :, hh] = jnp.zeros((2, HEAD_D, HEAD_D), F32)

    def output_pass(hh, ci, carry):
        rows = chunk_rows(ci)
        lanes = slice(hh * HEAD_D, (hh + 1) * HEAD_D)
        o = o_scr[hh, rows, :] + lax.dot_general(qfb_scr[hh, rows, :], st_scr[hh, ci], _NT,
                                                 preferred_element_type=F32)
        y_ref[rows, lanes] = (_rms(o, gon_ref[...])
                              * _silu(ga_ref[rows, lanes].astype(F32))).astype(BF16)
        return carry

    over_heads_and_chunks(output_pass, 8)


def _hgrn_mixer(gates, proj, lb, g_onorm, state, layer_j, seq_len, heads, want_final,
                final_prev=None):
    n_tok = proj.shape[0]
    n_seq = n_tok // seq_len
    n_chunks = seq_len // HGRN_CHUNK
    has_s0 = state is not None
    has_prev = final_prev is not None
    width = heads * HEAD_D
    blk = lambda part: pl.BlockSpec(
        (seq_len, width), lambda b, h: (b, part * (N_HEADS // heads) + h))
    in_specs = [blk(AB_PART["q"]), blk(0), blk(1), blk(AB_PART["i"]), blk(AB_PART["gate_a"]),
                pl.BlockSpec((2, width), lambda b, h: (0, h)),
                pl.BlockSpec((1, HEAD_D), lambda b, h: (0, 0)),
                pl.BlockSpec(_HGRN_TRI.shape, lambda b, h: (0, 0)),
                pl.BlockSpec(_HGRN_PSUM.shape, lambda b, h: (0, 0)),
                pl.BlockSpec(_HGRN_LEVEL.shape, lambda b, h: (0, 0))]
    args = [proj, gates, gates, proj, proj, lb, g_onorm.reshape(1, HEAD_D),
            jnp.asarray(_HGRN_TRI, BF16), jnp.asarray(_HGRN_PSUM, BF16), jnp.asarray(_HGRN_LEVEL)]
    state_blk = pl.BlockSpec((1, 1, 2, heads, HEAD_D, HEAD_D), lambda b, h: (b, layer_j, 0, h, 0, 0))
    if has_s0:
        in_specs.append(state_blk)
        args.append(state)
    aliases = {}
    if has_prev:
        aliases[len(args)] = 1
        in_specs.append(pl.BlockSpec(memory_space=pl.ANY))
        args.append(final_prev)
    out_specs = [pl.BlockSpec((seq_len, width), lambda b, h: (b, h))]
    out_shape = [jax.ShapeDtypeStruct((n_tok, BRANCH_W), BF16)]
    if want_final:
        out_specs.append(state_blk if has_prev else pl.BlockSpec(
            (1, N_AB, 2, heads, HEAD_D, HEAD_D), lambda b, h: (b, 0, 0, h, 0, 0)))
        out_shape.append(jax.ShapeDtypeStruct((n_seq, N_AB, 2, N_HEADS, HEAD_D, HEAD_D), F32))
    outs = pl.pallas_call(
        functools.partial(_hgrn_kernel, n_chunks=n_chunks, heads=heads, has_s0=has_s0,
                          has_prev=has_prev, want_final=want_final,
                          final_own=0 if has_prev else layer_j),
        grid=(n_seq, N_HEADS // heads),
        in_specs=in_specs,
        out_specs=out_specs,
        out_shape=out_shape,
        input_output_aliases=aliases,
        scratch_shapes=[
            pltpu.VMEM((heads, seq_len, HEAD_D), F32),
            pltpu.VMEM((heads, seq_len, 2 * HEAD_D), BF16),
            pltpu.VMEM((heads, n_chunks, HEAD_D, 2 * HEAD_D), F32),
            pltpu.VMEM((heads, n_chunks, 8, 2 * HEAD_D), F32),
            pltpu.VMEM((heads, n_chunks, HEAD_D, 2 * HEAD_D), BF16),
        ],
        compiler_params=_cparams(2, 48),
        name="hgrn_mixer",
    )(*args)
    return outs if want_final else (outs[0], None)


def kernel(x_prompt, x_sample, c, state_hgrn, c_ctx, w_ada, b_ada, g_pre, g_post, w_in_ab, w_out_ab,
           lb_logits, g_onorm_a, w_pool, pool_scale, w_in_c, w_out_c, ln_v_g, ln_v_b, w_spatial,
           b_spatial):
    n_batch, seq_p, _ = x_prompt.shape
    dec_batch, seq_s, _ = x_sample.shape
    tm_in, tm_out, tm_c, pool_rows = 512, 512, 512, 2048

    cond = jnp.concatenate(
        [c_ctx[None, :], c, jnp.zeros((N_COND_ROWS - 1 - dec_batch, D_MODEL), F32)], axis=0)
    mods = _modulations(cond, w_ada, b_ada).reshape(DEPTH, N_COND_ROWS, 3, D_MODEL)
    lb_all = _lower_bounds(lb_logits)

    flows = [
        dict(x=x_prompt.reshape(n_batch * seq_p, D_MODEL), seq=seq_p, grid=False, state=None,
             row=lambda tm: (lambda i: 0)),
        dict(x=x_sample.reshape(dec_batch * seq_s, D_MODEL), seq=seq_s, grid=True, state=state_hgrn,
             row=lambda tm: (lambda i: 1 + i // (seq_s // tm))),
    ]
    w_in_ab, w_out_ab, w_in_c, w_out_c = (
        w.astype(BF16) for w in (w_in_ab, w_out_ab, w_in_c, w_out_c))
    new_state = None
    for l in range(DEPTH):
        j = l // 2
        if l % 2 == 0:
            for fi, fl in enumerate(flows):
                gates, proj = _in_proj(fl["x"], mods[l], g_pre[l], w_in_ab, j, fl["row"](tm_in), tm_in)
                heads = max(2, min(N_HEADS, 8 * HGRN_CHUNK // fl["seq"]))
                if fi == 0:
                    y_a, new_state = _hgrn_mixer(gates, proj, lb_all[j], g_onorm_a[j], None, j,
                                                 fl["seq"], heads, want_final=True,
                                                 final_prev=new_state)
                else:
                    y_a, _ = _hgrn_mixer(gates, proj, lb_all[j], g_onorm_a[j], fl["state"], j,
                                         fl["seq"], heads, want_final=False)
                y_b = _pool_mixer(proj, w_pool[j], pool_scale[j], fl["seq"], fl["grid"], pool_rows)
                fl["x"] = _out_proj([y_a, y_b], w_out_ab, j, fl["x"], mods[l], g_post[l],
                                    fl["row"](tm_out), tm_out)
        else:
            for fl in flows:
                fl["x"] = _layer_c(fl["x"], mods[l], g_pre[l], g_post[l], w_in_c, w_out_c, j,
                                   ln_v_g[j], ln_v_b[j], w_spatial[j], b_spatial[j],
                                   fl["row"](tm_c), tm_c)
    y_p = flows[0]["x"].reshape(n_batch, seq_p, D_MODEL)
    y_s = flows[1]["x"].reshape(dec_batch, seq_s, D_MODEL)
    return (y_p, y_s, new_state)
```

```python
import functools

import numpy as np
import jax
import jax.numpy as jnp
from jax import lax
from jax.experimental import pallas as pl
from jax.experimental.pallas import tpu as pltpu

F32 = jnp.float32
BF16 = jnp.bfloat16

D_MODEL = 1024
DEPTH = 4
N_AB = (DEPTH + 1) // 2
MIX_WIDTH = 2 * D_MODEL
BRANCH_W = MIX_WIDTH // 2
HEAD_D = 128
N_HEADS = BRANCH_W // HEAD_D
GRID_W = 64
POOL_WINDOWS = (2, 4, 8, 16)
POOL_GROUP_W = BRANCH_W // len(POOL_WINDOWS)
CHUNK_C = 128
N_GROUPS_C = 8
GROUP_W_C = MIX_WIDTH // N_GROUPS_C
AB_IN = 7 * BRANCH_W
C_IN = 3 * MIX_WIDTH
AB_F32_PARTS = 2
AB_PART = {"q": 0, "i": 1, "gate_a": 2, "pool_in": 3, "gate_b": 4}
EPS = 1e-6
N_COND_ROWS = 8

HGRN_CHUNK = 128
HGRN_LEVELS = (64, 32, 16, 8)
HGRN_PSUM_LEVELS = (2, 4)
HGRN_IN_FLIGHT = 16

TOKEN_TILE = 512
POOL_ROWS = 2048
VMEM_MIB = {"adaln_modulation": 24, "in_projection": 56, "out_projection": 40, "gmlp_layer": 48,
            "pool_mixer": 56, "hgrn_mixer": 48}
MIB = 1024 * 1024


def _cparams(n_axes, name):
    return pltpu.CompilerParams(
        dimension_semantics=("arbitrary",) * n_axes, vmem_limit_bytes=VMEM_MIB[name] * MIB)


def _resident(shape, layer=None):
    zeros = (0,) * len(shape)
    if layer is None:
        return pl.BlockSpec(shape, lambda *_: zeros, pipeline_mode=pl.Buffered(1))
    return pl.BlockSpec((None,) + tuple(shape), lambda *_: (layer,) + zeros,
                        pipeline_mode=pl.Buffered(1))


def _silu(x):
    return x * jax.nn.sigmoid(x)


def _rms(x, g):
    return x * lax.rsqrt(jnp.mean(x * x, axis=-1, keepdims=True) + EPS) * g


def _modulated_norm(x, mod, g):
    return _rms(x, g) * (1.0 + mod[1:2]) + mod[0:1]


def _mod_kernel(cond_ref, w_ref, b_ref, o_ref):
    a = _silu(cond_ref[...])
    o_ref[0] = jnp.dot(a.astype(BF16), w_ref[0].astype(BF16), preferred_element_type=F32) + b_ref[0]


def _modulations(cond, w_ada, b_ada):
    tn = 1024
    n3 = 3 * D_MODEL
    return pl.pallas_call(
        _mod_kernel,
        grid=(DEPTH, n3 // tn),
        in_specs=[
            pl.BlockSpec((N_COND_ROWS, D_MODEL), lambda l, j: (0, 0)),
            pl.BlockSpec((1, D_MODEL, tn), lambda l, j: (l, 0, j)),
            pl.BlockSpec((1, 1, tn), lambda l, j: (l, 0, j)),
        ],
        out_specs=pl.BlockSpec((1, N_COND_ROWS, tn), lambda l, j: (l, 0, j)),
        out_shape=jax.ShapeDtypeStruct((DEPTH, N_COND_ROWS, n3), F32),
        compiler_params=_cparams(2, "adaln_modulation"),
        name="adaln_modulation",
    )(cond, w_ada, b_ada.reshape(DEPTH, 1, n3))


def _lb_kernel(x_ref, o_ref):
    x = x_ref[...]
    e = jnp.exp(x - jnp.max(x, axis=0, keepdims=True))
    p = e / jnp.sum(e, axis=0, keepdims=True)
    run = p[0]
    o_ref[0] = run - p[0]
    for l in range(1, x.shape[0]):
        run = run + p[l]
        o_ref[l] = run - p[0]


def _lower_bounds(lb_logits):
    return pl.pallas_call(
        _lb_kernel, out_shape=jax.ShapeDtypeStruct(lb_logits.shape, F32), name="hgrn_lower_bounds",
    )(lb_logits)


def _in_kernel(x_ref, mod_ref, g_ref, w_ref, gates_ref, proj_ref):
    h = _modulated_norm(x_ref[...], mod_ref[0], g_ref[...]).astype(BF16)

    def part(k):
        return jnp.dot(h, w_ref[:, k * BRANCH_W:(k + 1) * BRANCH_W], preferred_element_type=F32)

    for slot, k in enumerate((1, 2)):
        gates_ref[:, slot * BRANCH_W:(slot + 1) * BRANCH_W] = part(k)
    for slot, k in enumerate((0, 3, 4, 5, 6)):
        proj_ref[:, slot * BRANCH_W:(slot + 1) * BRANCH_W] = part(k).astype(BF16)


def _in_proj(x, mod, g, w, layer, row_of_tile, tm):
    n_tok = x.shape[0]
    n_gate, n_proj = AB_F32_PARTS * BRANCH_W, AB_IN - AB_F32_PARTS * BRANCH_W
    return pl.pallas_call(
        _in_kernel,
        grid=(n_tok // tm,),
        in_specs=[
            pl.BlockSpec((tm, D_MODEL), lambda i: (i, 0)),
            pl.BlockSpec((1, 3, D_MODEL), lambda i: (row_of_tile(i), 0, 0)),
            _resident((1, D_MODEL)),
            _resident((D_MODEL, AB_IN), layer),
        ],
        out_specs=[pl.BlockSpec((tm, n_gate), lambda i: (i, 0)),
                   pl.BlockSpec((tm, n_proj), lambda i: (i, 0))],
        out_shape=[jax.ShapeDtypeStruct((n_tok, n_gate), F32),
                   jax.ShapeDtypeStruct((n_tok, n_proj), BF16)],
        compiler_params=_cparams(1, "in_projection"),
        name="in_projection",
    )(x, mod, g.reshape(1, D_MODEL), w)


def _out_kernel(*refs, n_parts):
    y_refs, w_refs = refs[:n_parts], refs[n_parts:2 * n_parts]
    x_ref, mod_ref, g_ref, o_ref = refs[2 * n_parts:]
    acc = jnp.dot(y_refs[0][...], w_refs[0][...], preferred_element_type=F32)
    for y_ref, w_ref in zip(y_refs[1:], w_refs[1:]):
        acc = acc + jnp.dot(y_ref[...], w_ref[...], preferred_element_type=F32)
    o_ref[...] = x_ref[...] + mod_ref[0][2:3] * _rms(acc, g_ref[...])


def _out_proj(ys, w, layer, x, mod, g, row_of_tile, tm):
    n_tok = x.shape[0]
    n_parts = len(ys)
    width = ys[0].shape[1]
    in_specs = [pl.BlockSpec((tm, width), lambda i: (i, 0)) for _ in ys]
    in_specs += [pl.BlockSpec((None, width, D_MODEL), lambda i, k=k: (layer, k, 0))
                 for k in range(n_parts)]
    in_specs += [
        pl.BlockSpec((tm, D_MODEL), lambda i: (i, 0)),
        pl.BlockSpec((1, 3, D_MODEL), lambda i: (row_of_tile(i), 0, 0)),
        pl.BlockSpec((1, D_MODEL), lambda i: (0, 0)),
    ]
    return pl.pallas_call(
        functools.partial(_out_kernel, n_parts=n_parts),
        grid=(n_tok // tm,),
        in_specs=in_specs,
        out_specs=pl.BlockSpec((tm, D_MODEL), lambda i: (i, 0)),
        out_shape=jax.ShapeDtypeStruct((n_tok, D_MODEL), F32),
        compiler_params=_cparams(1, "out_projection"),
        name="out_projection",
    )(*ys, *([w] * n_parts), x, mod, g.reshape(1, D_MODEL))


def _layer_c_kernel(x_ref, mod_ref, gpre_ref, win_ref, lng_ref, lnb_ref, ws_ref, bs_ref, wout_ref,
                    gpost_ref, o_ref, y_scr, *, tm):
    x = x_ref[...]
    mod = mod_ref[0]
    h = _modulated_norm(x, mod, gpre_ref[...]).astype(BF16)

    v = jnp.dot(h, win_ref[:, MIX_WIDTH:2 * MIX_WIDTH], preferred_element_type=F32)
    mu = jnp.mean(v, axis=-1, keepdims=True)
    vc = v - mu
    var = jnp.mean(vc * vc, axis=-1, keepdims=True)
    vn = (vc * lax.rsqrt(var + EPS) * lng_ref[...] + lnb_ref[...]).astype(BF16)

    for gi in range(N_GROUPS_C):
        cols = slice(gi * GROUP_W_C, (gi + 1) * GROUP_W_C)
        u = jnp.dot(h, win_ref[:, cols], preferred_element_type=F32)
        gate = _silu(jnp.dot(h, win_ref[:, 2 * MIX_WIDTH + gi * GROUP_W_C:
                                        2 * MIX_WIDTH + (gi + 1) * GROUP_W_C],
                             preferred_element_type=F32))
        for n in range(tm // CHUNK_C):
            rows = slice(n * CHUNK_C, (n + 1) * CHUNK_C)
            sp = jnp.dot(ws_ref[gi], vn[rows, cols], preferred_element_type=F32) + bs_ref[:, gi:gi + 1]
            y_scr[rows, cols] = (u[rows] * sp * gate[rows]).astype(BF16)

    acc = jnp.dot(y_scr[...], wout_ref[...], preferred_element_type=F32)
    o_ref[...] = x + mod[2:3] * _rms(acc, gpost_ref[...])


def _layer_c(x, mod, g_pre, g_post, w_in, w_out, layer, ln_g, ln_b, w_s, b_s, row_of_tile, tm):
    n_tok = x.shape[0]
    return pl.pallas_call(
        functools.partial(_layer_c_kernel, tm=tm),
        grid=(n_tok // tm,),
        in_specs=[
            pl.BlockSpec((tm, D_MODEL), lambda i: (i, 0)),
            pl.BlockSpec((1, 3, D_MODEL), lambda i: (row_of_tile(i), 0, 0)),
            _resident((1, D_MODEL)),
            _resident((D_MODEL, C_IN), layer),
            _resident((1, MIX_WIDTH)),
            _resident((1, MIX_WIDTH)),
            _resident((N_GROUPS_C, CHUNK_C, CHUNK_C)),
            _resident((CHUNK_C, N_GROUPS_C)),
            _resident((MIX_WIDTH, D_MODEL), layer),
            _resident((1, D_MODEL)),
        ],
        out_specs=pl.BlockSpec((tm, D_MODEL), lambda i: (i, 0)),
        out_shape=jax.ShapeDtypeStruct((n_tok, D_MODEL), F32),
        scratch_shapes=[pltpu.VMEM((tm, MIX_WIDTH), BF16)],
        compiler_params=_cparams(1, "gmlp_layer"),
        name="gmlp_layer",
    )(x, mod, g_pre.reshape(1, D_MODEL), w_in, ln_g.reshape(1, MIX_WIDTH), ln_b.reshape(1, MIX_WIDTH),
      w_s.astype(BF16), b_s.T, w_out, g_post.reshape(1, D_MODEL))


def _window_counts(n, w):
    return [min(i - w // 2 + w, n) - max(i - w // 2, 0) for i in range(n)]


def _window_bands(n):
    bands = np.zeros((len(POOL_WINDOWS), n, n), np.float32)
    for k, w in enumerate(POOL_WINDOWS):
        for i in range(n):
            bands[k, i, max(i - w // 2, 0):min(i - w // 2 + w, n)] = 1
    return bands


def _mean_minor(x, band, inv_count):
    n = band.shape[0]
    ch = x.shape[1]
    groups = x.shape[0] // n
    wide = jnp.concatenate([x[i * n:(i + 1) * n, :] for i in range(groups)], axis=1)
    sums = jnp.dot(band, wide, preferred_element_type=F32)
    return [sums[:, i * ch:(i + 1) * ch] * inv_count for i in range(groups)]


def _mean_major(x3, w):
    n = x3.shape[0]
    zeros = lambda k: jnp.zeros((k,) + x3.shape[1:], F32)
    back = lambda a, k: jnp.concatenate([zeros(k), a[:n - k]], axis=0)
    ahead = lambda a, k: jnp.concatenate([a[k:], zeros(k)], axis=0)
    trail, lead = x3, x3
    size = 1
    while size < w // 2:
        trail = trail + back(trail, size)
        lead = lead + ahead(lead, size)
        size *= 2
    s = back(trail, 1) + lead
    return jnp.concatenate(
        [s[i:i + 1] * (1.0 / cnt) for i, cnt in enumerate(_window_counts(n, w))], axis=0)


def _pool_kernel(p_ref, gate_ref, band_ref, inv_ref, wp_ref, ps_ref, y_ref, *, grid_mode):
    grp = pl.program_id(1)
    rows, ch = p_ref.shape
    means = _mean_minor(p_ref[...], band_ref[0], inv_ref[0])

    def finish(m):
        dlt = (m - p_ref[...].astype(F32)).astype(BF16)
        y = jnp.dot(dlt, wp_ref[0], preferred_element_type=F32) * ps_ref[...]
        y_ref[...] = (y * _silu(gate_ref[...].astype(F32))).astype(BF16)

    if grid_mode:
        m3 = jnp.stack(means, axis=0)
        for k, w in enumerate(POOL_WINDOWS):
            pl.when(grp == k)(lambda w=w: finish(_mean_major(m3, w).reshape(rows, ch)))
    else:
        finish(jnp.concatenate(means, axis=0))


def _pool_mixer(proj, w_pool, pool_scale, seq_len, grid_mode, rows_per_step):
    n_tok = proj.shape[0]
    n_grp = len(POOL_WINDOWS)
    p_blk0 = AB_PART["pool_in"] * BRANCH_W // POOL_GROUP_W
    g_blk0 = AB_PART["gate_b"] * BRANCH_W // POOL_GROUP_W
    n = GRID_W if grid_mode else seq_len
    assert not grid_mode or rows_per_step == seq_len
    bands = _window_bands(n)
    inv_count = np.broadcast_to(1.0 / bands.sum(axis=2, keepdims=True), (n_grp, n, POOL_GROUP_W))
    return pl.pallas_call(
        functools.partial(_pool_kernel, grid_mode=grid_mode),
        grid=(n_tok // rows_per_step, n_grp),
        in_specs=[
            pl.BlockSpec((rows_per_step, POOL_GROUP_W), lambda b, g: (b, p_blk0 + g)),
            pl.BlockSpec((rows_per_step, POOL_GROUP_W), lambda b, g: (b, g_blk0 + g)),
            pl.BlockSpec((1, n, n), lambda b, g: (g, 0, 0)),
            pl.BlockSpec((1, n, POOL_GROUP_W), lambda b, g: (g, 0, 0)),
            pl.BlockSpec((1, POOL_GROUP_W, POOL_GROUP_W), lambda b, g: (g, 0, 0)),
            pl.BlockSpec((1, POOL_GROUP_W), lambda b, g: (0, g)),
        ],
        out_specs=pl.BlockSpec((rows_per_step, POOL_GROUP_W), lambda b, g: (b, g)),
        out_shape=jax.ShapeDtypeStruct((n_tok, BRANCH_W), BF16),
        compiler_params=_cparams(2, "pool_mixer"),
        name="pool_mixer",
    )(proj, proj, jnp.asarray(bands, BF16), jnp.asarray(inv_count, F32), w_pool.astype(BF16),
      pool_scale.reshape(1, BRANCH_W))


def _hgrn_triangles():
    ones = np.ones((HGRN_CHUNK, HGRN_CHUNK), np.float32)
    return np.concatenate([np.tril(ones), np.triu(ones)], axis=0)


def _hgrn_partial_sum_matrix():
    c = HGRN_CHUNK
    blocks = []
    for m in HGRN_PSUM_LEVELS:
        mq = np.zeros((c, 2 * c), np.float32)
        mk = np.zeros((c, 2 * c), np.float32)
        for i in range(c):
            start = (i // (2 * m)) * (2 * m)
            r1, r2 = start + m - 1, start + m
            if i >= r2:
                mq[i, r2:i + 1] = 1
                mk[i, c + r2:c + i] = 1
            else:
                mq[i, c + i:c + r1 + 1] = 1
                mk[i, i + 1:r1 + 1] = 1
        blocks += [mq, mk]
    return np.concatenate(blocks, axis=0)


def _hgrn_level_index():
    i = np.arange(HGRN_CHUNK)
    x = i[:, None] ^ i[None, :]
    lv = np.zeros_like(x)
    nz = x > 0
    lv[nz] = np.floor(np.log2(x[nz])).astype(x.dtype) + 1
    return lv.astype(np.int32)


_HGRN_TRI = _hgrn_triangles()
_HGRN_PSUM = _hgrn_partial_sum_matrix()
_HGRN_LEVEL = _hgrn_level_index()

_NT = (((1,), (1,)), ((), ()))
_TN = (((0,), (0,)), ((), ()))


def _hgrn_kernel(*refs, n_chunks, heads, has_s0, has_prev, want_final, final_own):
    q_ref, ff_ref, fb_ref, v_ref, ga_ref, lb_ref, gon_ref, tri_ref, psum_ref, lvl_ref = refs[:10]
    pos = 10
    s0_ref = sfin_ref = None
    if has_s0:
        s0_ref = refs[pos]
        pos += 1
    if has_prev:
        pos += 1
    y_ref = refs[pos]
    pos += 1
    if want_final:
        sfin_ref = refs[pos]
        pos += 1
    o_scr, qfb_scr, dst_scr, dec_scr, st_scr = refs[pos:]

    c = HGRN_CHUNK
    inline = heads * n_chunks <= HGRN_IN_FLIGHT
    level = lvl_ref[...]
    row8 = jnp.concatenate([lax.broadcasted_iota(jnp.int32, (8, HEAD_D), 0)] * (c // 8), axis=0)

    def chunk_rows(ci):
        if isinstance(ci, int):
            return pl.ds(ci * c, c)
        return pl.ds(pl.multiple_of(ci * c, c), c)

    def over_chunks(body, init, unroll):
        if inline:
            carry = init
            for ci in range(n_chunks):
                carry = body(ci, carry)
            return carry
        return lax.fori_loop(0, n_chunks, body, init, unroll=unroll)

    def over_heads_and_chunks(fn, in_flight):
        def body(ci, carry):
            for hh in range(heads):
                fn(hh, ci, carry)
            return carry
        over_chunks(body, 0, unroll=max(1, min(n_chunks, in_flight // heads)))

    def hi_mid(x):
        hi = x.astype(BF16)
        return hi, (x - hi.astype(F32)).astype(BF16)

    def summed(zero_one, hi, mid):
        dd = jnp.dot(zero_one, jnp.concatenate([hi, mid], axis=1), preferred_element_type=F32)
        return dd[:, :HEAD_D] + dd[:, HEAD_D:]

    def local_pass(hh, ci, carry):
        rows = chunk_rows(ci)
        lanes = slice(hh * HEAD_D, (hh + 1) * HEAD_D)
        lb_f, lb_b = lb_ref[0:1, lanes], lb_ref[1:2, lanes]
        f_f = lb_f + (1.0 - lb_f) * jax.nn.sigmoid(ff_ref[rows, lanes])
        f_b = lb_b + (1.0 - lb_b) * jax.nn.sigmoid(fb_ref[rows, lanes])
        k_f, k_b = 1.0 - f_f, 1.0 - f_b
        hi_f, mid_f = hi_mid(jnp.log2(f_f))
        hi_b, mid_b = hi_mid(jnp.log2(f_b))
        b_f = summed(tri_ref[0:c, :], hi_f, mid_f)
        b_b = summed(tri_ref[c:2 * c, :], hi_b, mid_b)
        small = summed(psum_ref[...], jnp.concatenate([hi_f, hi_b], axis=0),
                       jnp.concatenate([mid_f, mid_b], axis=0))

        q = _silu(q_ref[rows, lanes].astype(F32))
        v = v_ref[rows, lanes]

        def level_scores(d_q, d_k, k_sel):
            qt = (q * jnp.exp2(d_q)).astype(BF16)
            kt = (k_sel * jnp.exp2(d_k)).astype(BF16)
            return lax.dot_general(qt, kt, _NT, preferred_element_type=F32)

        s = lax.dot_general(q.astype(BF16), (k_f + k_b).astype(BF16), _NT,
                            preferred_element_type=F32)
        s = jnp.where(level == 0, s, 0.0)
        odd = (row8 & 1) == 1
        q1 = (q * jnp.where(odd, f_f, f_b)).astype(BF16)
        k1 = jnp.where(odd, k_b, k_f).astype(BF16)
        s = jnp.where(level == 1, lax.dot_general(q1, k1, _NT, preferred_element_type=F32), s)
        for li, m in enumerate(HGRN_PSUM_LEVELS):
            second = (row8 & m) == m
            s_l = level_scores(small[2 * li * c:(2 * li + 1) * c], small[(2 * li + 1) * c:(2 * li + 2) * c],
                               jnp.where(second, k_b, k_f))
            s = jnp.where(level == m.bit_length(), s_l, s)
        for m in HGRN_LEVELS:
            dq_parts, dk_parts, k_parts = [], [], []
            for lo in range(0, c, 2 * m):
                mid_row, hi_row = lo + m, lo + 2 * m
                edge_f = b_f[mid_row - 1:mid_row]
                edge_b = b_b[mid_row:mid_row + 1]
                dq_parts += [b_b[lo:mid_row] - edge_b, b_f[mid_row:hi_row] - edge_f]
                dk_parts += [edge_f - b_f[lo:mid_row], edge_b - b_b[mid_row:hi_row]]
                k_parts += [k_f[lo:mid_row], k_b[mid_row:hi_row]]
            s_l = level_scores(jnp.concatenate(dq_parts, axis=0), jnp.concatenate(dk_parts, axis=0),
                               jnp.concatenate(k_parts, axis=0))
            s = jnp.where(level == m.bit_length(), s_l, s)

        q_fb = jnp.concatenate([q * jnp.exp2(b_f), q * jnp.exp2(b_b)], axis=1).astype(BF16)
        k_fb = jnp.concatenate([k_f * jnp.exp2(b_f[c - 1:c] - b_f),
                                k_b * jnp.exp2(b_b[0:1] - b_b)], axis=1).astype(BF16)
        dec = jnp.concatenate([jnp.exp2(b_f[c - 1:c]), jnp.exp2(b_b[0:1])], axis=1)

        o_scr[hh, rows, :] = jnp.dot(s.astype(BF16), v, preferred_element_type=F32)
        qfb_scr[hh, rows, :] = q_fb
        dst_scr[hh, ci] = lax.dot_general(v, k_fb, _TN, preferred_element_type=F32)
        dec_scr[hh, ci] = jnp.broadcast_to(dec, (8, 2 * HEAD_D))
        return carry

    over_heads_and_chunks(local_pass, HGRN_IN_FLIGHT)

    for hh in range(heads):
        if has_s0:
            starts = (s0_ref[0, 0, 0, hh].T, s0_ref[0, 0, 1, hh].T)
        else:
            starts = (jnp.zeros((HEAD_D, HEAD_D), F32),) * 2

        def scan_step(k, states, hh=hh):
            out = []
            for d, ci in ((0, k), (1, n_chunks - 1 - k)):
                lanes = slice(d * HEAD_D, (d + 1) * HEAD_D)
                st_scr[hh, ci, :, lanes] = states[d].astype(BF16)
                out.append(states[d] * dec_scr[hh, ci, 0:1, lanes] + dst_scr[hh, ci, :, lanes])
            return tuple(out)

        st_f, st_b = over_chunks(scan_step, starts, unroll=2)
        if want_final:
            sfin_ref[0, final_own, 0, hh] = st_f.T
            sfin_ref[0, final_own, 1, hh] = st_b.T
            for other in range(sfin_ref.shape[1]):
                if other != final_own:
                    sfin_ref[0, other, :, hh] = jnp.zeros((2, HEAD_D, HEAD_D), F32)

    def output_pass(hh, ci, carry):
        rows = chunk_rows(ci)
        lanes = slice(hh * HEAD_D, (hh + 1) * HEAD_D)
        o = o_scr[hh, rows, :] + lax.dot_general(qfb_scr[hh, rows, :], st_scr[hh, ci], _NT,
                                                 preferred_element_type=F32)
        y_ref[rows, lanes] = (_rms(o, gon_ref[...])
                              * _silu(ga_ref[rows, lanes].astype(F32))).astype(BF16)
        return carry

    over_heads_and_chunks(output_pass, HGRN_IN_FLIGHT // 2)


def _hgrn_mixer(gates, proj, lb, g_onorm, state, layer_j, seq_len, heads, want_final,
                final_prev=None):
    n_tok = proj.shape[0]
    n_seq = n_tok // seq_len
    n_chunks = seq_len // HGRN_CHUNK
    has_s0 = state is not None
    has_prev = final_prev is not None
    width = heads * HEAD_D
    blk = lambda part: pl.BlockSpec(
        (seq_len, width), lambda b, h: (b, part * (N_HEADS // heads) + h))
    in_specs = [blk(AB_PART["q"]), blk(0), blk(1), blk(AB_PART["i"]), blk(AB_PART["gate_a"]),
                pl.BlockSpec((2, width), lambda b, h: (0, h)),
                pl.BlockSpec((1, HEAD_D), lambda b, h: (0, 0)),
                pl.BlockSpec(_HGRN_TRI.shape, lambda b, h: (0, 0)),
                pl.BlockSpec(_HGRN_PSUM.shape, lambda b, h: (0, 0)),
                pl.BlockSpec(_HGRN_LEVEL.shape, lambda b, h: (0, 0))]
    args = [proj, gates, gates, proj, proj, lb, g_onorm.reshape(1, HEAD_D),
            jnp.asarray(_HGRN_TRI, BF16), jnp.asarray(_HGRN_PSUM, BF16), jnp.asarray(_HGRN_LEVEL)]
    state_blk = pl.BlockSpec((1, 1, 2, heads, HEAD_D, HEAD_D), lambda b, h: (b, layer_j, 0, h, 0, 0))
    if has_s0:
        in_specs.append(state_blk)
        args.append(state)
    aliases = {}
    if has_prev:
        aliases[len(args)] = 1
        in_specs.append(pl.BlockSpec(memory_space=pl.ANY))
        args.append(final_prev)
    out_specs = [pl.BlockSpec((seq_len, width), lambda b, h: (b, h))]
    out_shape = [jax.ShapeDtypeStruct((n_tok, BRANCH_W), BF16)]
    if want_final:
        out_specs.append(state_blk if has_prev else pl.BlockSpec(
            (1, N_AB, 2, heads, HEAD_D, HEAD_D), lambda b, h: (b, 0, 0, h, 0, 0)))
        out_shape.append(jax.ShapeDtypeStruct((n_seq, N_AB, 2, N_HEADS, HEAD_D, HEAD_D), F32))
    outs = pl.pallas_call(
        functools.partial(_hgrn_kernel, n_chunks=n_chunks, heads=heads, has_s0=has_s0,
                          has_prev=has_prev, want_final=want_final,
                          final_own=0 if has_prev else layer_j),
        grid=(n_seq, N_HEADS // heads),
        in_specs=in_specs,
        out_specs=out_specs,
        out_shape=out_shape,
        input_output_aliases=aliases,
        scratch_shapes=[
            pltpu.VMEM((heads, seq_len, HEAD_D), F32),
            pltpu.VMEM((heads, seq_len, 2 * HEAD_D), BF16),
            pltpu.VMEM((heads, n_chunks, HEAD_D, 2 * HEAD_D), F32),
            pltpu.VMEM((heads, n_chunks, 8, 2 * HEAD_D), F32),
            pltpu.VMEM((heads, n_chunks, HEAD_D, 2 * HEAD_D), BF16),
        ],
        compiler_params=_cparams(2, "hgrn_mixer"),
        name="hgrn_mixer",
    )(*args)
    return outs if want_final else (outs[0], None)


def kernel(x_prompt, x_sample, c, state_hgrn, c_ctx, w_ada, b_ada, g_pre, g_post, w_in_ab, w_out_ab,
           lb_logits, g_onorm_a, w_pool, pool_scale, w_in_c, w_out_c, ln_v_g, ln_v_b, w_spatial,
           b_spatial):
    n_batch, seq_p, _ = x_prompt.shape
    dec_batch, seq_s, _ = x_sample.shape
    assert dec_batch + 1 <= N_COND_ROWS and seq_s % TOKEN_TILE == 0 and POOL_ROWS % seq_p == 0

    cond = jnp.concatenate(
        [c_ctx[None, :], c, jnp.zeros((N_COND_ROWS - 1 - dec_batch, D_MODEL), F32)], axis=0)
    mods = _modulations(cond, w_ada, b_ada).reshape(DEPTH, N_COND_ROWS, 3, D_MODEL)
    lb_all = _lower_bounds(lb_logits)

    flows = [
        dict(x=x_prompt.reshape(n_batch * seq_p, D_MODEL), seq=seq_p, grid=False, state=None,
             row=lambda i: 0),
        dict(x=x_sample.reshape(dec_batch * seq_s, D_MODEL), seq=seq_s, grid=True, state=state_hgrn,
             row=lambda i: 1 + i // (seq_s // TOKEN_TILE)),
    ]
    w_in_ab, w_out_ab, w_in_c, w_out_c = (
        w.astype(BF16) for w in (w_in_ab, w_out_ab, w_in_c, w_out_c))
    new_state = None
    for l in range(DEPTH):
        j = l // 2
        if l % 2 == 0:
            for fi, fl in enumerate(flows):
                gates, proj = _in_proj(fl["x"], mods[l], g_pre[l], w_in_ab, j, fl["row"], TOKEN_TILE)
                heads = max(2, min(N_HEADS, HGRN_IN_FLIGHT * HGRN_CHUNK // fl["seq"]))
                if fi == 0:
                    y_a, new_state = _hgrn_mixer(gates, proj, lb_all[j], g_onorm_a[j], None, j,
                                                 fl["seq"], heads, want_final=True,
                                                 final_prev=new_state)
                else:
                    y_a, _ = _hgrn_mixer(gates, proj, lb_all[j], g_onorm_a[j], fl["state"], j,
                                         fl["seq"], heads, want_final=False)
                y_b = _pool_mixer(proj, w_pool[j], pool_scale[j], fl["seq"], fl["grid"], POOL_ROWS)
                fl["x"] = _out_proj([y_a, y_b], w_out_ab, j, fl["x"], mods[l], g_post[l],
                                    fl["row"], TOKEN_TILE)
        else:
            for fl in flows:
                fl["x"] = _layer_c(fl["x"], mods[l], g_pre[l], g_post[l], w_in_c, w_out_c, j,
                                   ln_v_g[j], ln_v_b[j], w_spatial[j], b_spatial[j],
                                   fl["row"], TOKEN_TILE)
    y_p = flows[0]["x"].reshape(n_batch, seq_p, D_MODEL)
    y_s = flows[1]["x"].reshape(dec_batch, seq_s, D_MODEL)
    return (y_p, y_s, new_state)
```

```python
import functools

import numpy as np
import jax
import jax.numpy as jnp
from jax import lax
from jax.experimental import pallas as pl
from jax.experimental.pallas import tpu as pltpu

F32 = jnp.float32
BF16 = jnp.bfloat16

D_MODEL = 1024
DEPTH = 4
N_AB = (DEPTH + 1) // 2
MIX_WIDTH = 2 * D_MODEL
BRANCH_W = MIX_WIDTH // 2
HEAD_D = 128
N_HEADS = BRANCH_W // HEAD_D
GRID_W = 64
POOL_WINDOWS = (2, 4, 8, 16)
POOL_GROUP_W = BRANCH_W // len(POOL_WINDOWS)
CHUNK_C = 128
N_GROUPS_C = 8
GROUP_W_C = MIX_WIDTH // N_GROUPS_C
AB_IN = 7 * BRANCH_W
C_IN = 3 * MIX_WIDTH
AB_F32_PARTS = 2
AB_PART = {"q": 0, "i": 1, "gate_a": 2, "pool_in": 3, "gate_b": 4}
EPS = 1e-6
N_COND_ROWS = 8

HGRN_CHUNK = 128
HGRN_LEVELS = (64, 32, 16, 8)
HGRN_PSUM_LEVELS = (2, 4)
HGRN_IN_FLIGHT = 16

TOKEN_TILE = 512
POOL_ROWS = 2048
VMEM_MIB = {"adaln_modulation": 24, "in_projection": 56, "out_projection": 40, "gmlp_layer": 48,
            "pool_mixer": 56, "hgrn_mixer": 48}
MIB = 1024 * 1024


def _cparams(n_axes, name):
    return pltpu.CompilerParams(
        dimension_semantics=("arbitrary",) * n_axes, vmem_limit_bytes=VMEM_MIB[name] * MIB)


def _resident(shape, layer=None):
    zeros = (0,) * len(shape)
    if layer is None:
        return pl.BlockSpec(shape, lambda *_: zeros, pipeline_mode=pl.Buffered(1))
    return pl.BlockSpec((None,) + tuple(shape), lambda *_: (layer,) + zeros,
                        pipeline_mode=pl.Buffered(1))


def _silu(x):
    return x * jax.nn.sigmoid(x)


def _rms(x, g):
    return x * lax.rsqrt(jnp.mean(x * x, axis=-1, keepdims=True) + EPS) * g


def _modulated_norm(x, mod, g):
    return _rms(x, g) * (1.0 + mod[1:2]) + mod[0:1]


def _mod_kernel(cond_ref, w_ref, b_ref, o_ref):
    a = _silu(cond_ref[...])
    o_ref[0] = jnp.dot(a.astype(BF16), w_ref[0].astype(BF16), preferred_element_type=F32) + b_ref[0]


def _modulations(cond, w_ada, b_ada):
    tn = 1024
    n3 = 3 * D_MODEL
    return pl.pallas_call(
        _mod_kernel,
        grid=(DEPTH, n3 // tn),
        in_specs=[
            pl.BlockSpec((N_COND_ROWS, D_MODEL), lambda l, j: (0, 0)),
            pl.BlockSpec((1, D_MODEL, tn), lambda l, j: (l, 0, j)),
            pl.BlockSpec((1, 1, tn), lambda l, j: (l, 0, j)),
        ],
        out_specs=pl.BlockSpec((1, N_COND_ROWS, tn), lambda l, j: (l, 0, j)),
        out_shape=jax.ShapeDtypeStruct((DEPTH, N_COND_ROWS, n3), F32),
        compiler_params=_cparams(2, "adaln_modulation"),
        name="adaln_modulation",
    )(cond, w_ada, b_ada.reshape(DEPTH, 1, n3))


def _lb_kernel(x_ref, o_ref):
    x = x_ref[...]
    e = jnp.exp(x - jnp.max(x, axis=0, keepdims=True))
    p = e / jnp.sum(e, axis=0, keepdims=True)
    run = p[0]
    o_ref[0] = run - p[0]
    for l in range(1, x.shape[0]):
        run = run + p[l]
        o_ref[l] = run - p[0]


def _lower_bounds(lb_logits):
    return pl.pallas_call(
        _lb_kernel, out_shape=jax.ShapeDtypeStruct(lb_logits.shape, F32), name="hgrn_lower_bounds",
    )(lb_logits)


def _in_kernel(x_ref, mod_ref, g_ref, w_ref, gates_ref, proj_ref):
    h = _modulated_norm(x_ref[...], mod_ref[0], g_ref[...]).astype(BF16)

    def part(k):
        return jnp.dot(h, w_ref[:, k * BRANCH_W:(k + 1) * BRANCH_W], preferred_element_type=F32)

    for slot, k in enumerate((1, 2)):
        gates_ref[:, slot * BRANCH_W:(slot + 1) * BRANCH_W] = part(k)
    for slot, (k, gate) in enumerate(((0, False), (3, False), (4, True), (5, False), (6, True))):
        y = part(k)
        proj_ref[:, slot * BRANCH_W:(slot + 1) * BRANCH_W] = (_silu(y) if gate else y).astype(BF16)


def _in_proj(x, mod, g, w, layer, row_of_tile, tm):
    n_tok = x.shape[0]
    n_gate, n_proj = AB_F32_PARTS * BRANCH_W, AB_IN - AB_F32_PARTS * BRANCH_W
    return pl.pallas_call(
        _in_kernel,
        grid=(n_tok // tm,),
        in_specs=[
            pl.BlockSpec((tm, D_MODEL), lambda i: (i, 0)),
            pl.BlockSpec((1, 3, D_MODEL), lambda i: (row_of_tile(i), 0, 0)),
            _resident((1, D_MODEL)),
            _resident((D_MODEL, AB_IN), layer),
        ],
        out_specs=[pl.BlockSpec((tm, n_gate), lambda i: (i, 0)),
                   pl.BlockSpec((tm, n_proj), lambda i: (i, 0))],
        out_shape=[jax.ShapeDtypeStruct((n_tok, n_gate), F32),
                   jax.ShapeDtypeStruct((n_tok, n_proj), BF16)],
        compiler_params=_cparams(1, "in_projection"),
        name="in_projection",
    )(x, mod, g.reshape(1, D_MODEL), w)


def _out_kernel(*refs, n_parts):
    y_refs, w_refs = refs[:n_parts], refs[n_parts:2 * n_parts]
    x_ref, mod_ref, g_ref, o_ref = refs[2 * n_parts:]
    acc = jnp.dot(y_refs[0][...], w_refs[0][...], preferred_element_type=F32)
    for y_ref, w_ref in zip(y_refs[1:], w_refs[1:]):
        acc = acc + jnp.dot(y_ref[...], w_ref[...], preferred_element_type=F32)
    o_ref[...] = x_ref[...] + mod_ref[0][2:3] * _rms(acc, g_ref[...])


def _out_proj(ys, w, layer, x, mod, g, row_of_tile, tm):
    n_tok = x.shape[0]
    n_parts = len(ys)
    width = ys[0].shape[1]
    in_specs = [pl.BlockSpec((tm, width), lambda i: (i, 0)) for _ in ys]
    in_specs += [pl.BlockSpec((None, width, D_MODEL), lambda i, k=k: (layer, k, 0))
                 for k in range(n_parts)]
    in_specs += [
        pl.BlockSpec((tm, D_MODEL), lambda i: (i, 0)),
        pl.BlockSpec((1, 3, D_MODEL), lambda i: (row_of_tile(i), 0, 0)),
        pl.BlockSpec((1, D_MODEL), lambda i: (0, 0)),
    ]
    return pl.pallas_call(
        functools.partial(_out_kernel, n_parts=n_parts),
        grid=(n_tok // tm,),
        in_specs=in_specs,
        out_specs=pl.BlockSpec((tm, D_MODEL), lambda i: (i, 0)),
        out_shape=jax.ShapeDtypeStruct((n_tok, D_MODEL), F32),
        compiler_params=_cparams(1, "out_projection"),
        name="out_projection",
    )(*ys, *([w] * n_parts), x, mod, g.reshape(1, D_MODEL))


def _layer_c_kernel(x_ref, mod_ref, gpre_ref, win_ref, lng_ref, lnb_ref, ws_ref, bs_ref, wout_ref,
                    gpost_ref, o_ref, y_scr, *, tm):
    x = x_ref[...]
    mod = mod_ref[0]
    h = _modulated_norm(x, mod, gpre_ref[...]).astype(BF16)

    v = jnp.dot(h, win_ref[:, MIX_WIDTH:2 * MIX_WIDTH], preferred_element_type=F32)
    mu = jnp.mean(v, axis=-1, keepdims=True)
    vc = v - mu
    var = jnp.mean(vc * vc, axis=-1, keepdims=True)
    vn = (vc * lax.rsqrt(var + EPS) * lng_ref[...] + lnb_ref[...]).astype(BF16)

    for gi in range(N_GROUPS_C):
        cols = slice(gi * GROUP_W_C, (gi + 1) * GROUP_W_C)
        u = jnp.dot(h, win_ref[:, cols], preferred_element_type=F32)
        gate = _silu(jnp.dot(h, win_ref[:, 2 * MIX_WIDTH + gi * GROUP_W_C:
                                        2 * MIX_WIDTH + (gi + 1) * GROUP_W_C],
                             preferred_element_type=F32))
        n_chunks = tm // CHUNK_C
        wide = jnp.concatenate(
            [vn[n * CHUNK_C:(n + 1) * CHUNK_C, cols] for n in range(n_chunks)], axis=1)
        mixed = jnp.dot(ws_ref[gi], wide, preferred_element_type=F32)
        for n in range(n_chunks):
            rows = slice(n * CHUNK_C, (n + 1) * CHUNK_C)
            sp = mixed[:, n * GROUP_W_C:(n + 1) * GROUP_W_C] + bs_ref[:, gi:gi + 1]
            y_scr[rows, cols] = (u[rows] * sp * gate[rows]).astype(BF16)

    acc = jnp.dot(y_scr[...], wout_ref[...], preferred_element_type=F32)
    o_ref[...] = x + mod[2:3] * _rms(acc, gpost_ref[...])


def _layer_c(x, mod, g_pre, g_post, w_in, w_out, layer, ln_g, ln_b, w_s, b_s, row_of_tile, tm):
    n_tok = x.shape[0]
    return pl.pallas_call(
        functools.partial(_layer_c_kernel, tm=tm),
        grid=(n_tok // tm,),
        in_specs=[
            pl.BlockSpec((tm, D_MODEL), lambda i: (i, 0)),
            pl.BlockSpec((1, 3, D_MODEL), lambda i: (row_of_tile(i), 0, 0)),
            _resident((1, D_MODEL)),
            _resident((D_MODEL, C_IN), layer),
            _resident((1, MIX_WIDTH)),
            _resident((1, MIX_WIDTH)),
            _resident((N_GROUPS_C, CHUNK_C, CHUNK_C)),
            _resident((CHUNK_C, N_GROUPS_C)),
            _resident((MIX_WIDTH, D_MODEL), layer),
            _resident((1, D_MODEL)),
        ],
        out_specs=pl.BlockSpec((tm, D_MODEL), lambda i: (i, 0)),
        out_shape=jax.ShapeDtypeStruct((n_tok, D_MODEL), F32),
        scratch_shapes=[pltpu.VMEM((tm, MIX_WIDTH), BF16)],
        compiler_params=_cparams(1, "gmlp_layer"),
        name="gmlp_layer",
    )(x, mod, g_pre.reshape(1, D_MODEL), w_in, ln_g.reshape(1, MIX_WIDTH), ln_b.reshape(1, MIX_WIDTH),
      w_s.astype(BF16), b_s.T, w_out, g_post.reshape(1, D_MODEL))


def _window_counts(n, w):
    return [min(i - w // 2 + w, n) - max(i - w // 2, 0) for i in range(n)]


def _window_bands(n):
    bands = np.zeros((len(POOL_WINDOWS), n, n), np.float32)
    for k, w in enumerate(POOL_WINDOWS):
        for i in range(n):
            bands[k, i, max(i - w // 2, 0):min(i - w // 2 + w, n)] = 1
    return bands


def _mean_minor(x, band, inv_count):
    n = band.shape[0]
    ch = x.shape[1]
    groups = x.shape[0] // n
    wide = jnp.concatenate([x[i * n:(i + 1) * n, :] for i in range(groups)], axis=1)
    sums = jnp.dot(band, wide, preferred_element_type=F32)
    return [sums[:, i * ch:(i + 1) * ch] * inv_count for i in range(groups)]


def _mean_major(x3, w):
    n = x3.shape[0]
    zeros = lambda k: jnp.zeros((k,) + x3.shape[1:], F32)
    back = lambda a, k: jnp.concatenate([zeros(k), a[:n - k]], axis=0)
    ahead = lambda a, k: jnp.concatenate([a[k:], zeros(k)], axis=0)
    trail, lead = x3, x3
    size = 1
    while size < w // 2:
        trail = trail + back(trail, size)
        lead = lead + ahead(lead, size)
        size *= 2
    s = back(trail, 1) + lead
    return jnp.concatenate(
        [s[i:i + 1] * (1.0 / cnt) for i, cnt in enumerate(_window_counts(n, w))], axis=0)


def _pool_kernel(p_ref, gate_ref, band_ref, inv_ref, wp_ref, ps_ref, y_ref, *, grid_mode):
    grp = pl.program_id(1)
    rows, ch = p_ref.shape
    means = _mean_minor(p_ref[...], band_ref[0], inv_ref[0])

    def finish(m):
        dlt = (m - p_ref[...].astype(F32)).astype(BF16)
        y = jnp.dot(dlt, wp_ref[0], preferred_element_type=F32) * ps_ref[...]
        y_ref[...] = (y * gate_ref[...].astype(F32)).astype(BF16)

    if grid_mode:
        m3 = jnp.stack(means, axis=0)
        for k, w in enumerate(POOL_WINDOWS):
            pl.when(grp == k)(lambda w=w: finish(_mean_major(m3, w).reshape(rows, ch)))
    else:
        finish(jnp.concatenate(means, axis=0))


def _pool_mixer(proj, w_pool, pool_scale, seq_len, grid_mode, rows_per_step):
    n_tok = proj.shape[0]
    n_grp = len(POOL_WINDOWS)
    p_blk0 = AB_PART["pool_in"] * BRANCH_W // POOL_GROUP_W
    g_blk0 = AB_PART["gate_b"] * BRANCH_W // POOL_GROUP_W
    n = GRID_W if grid_mode else seq_len
    assert not grid_mode or rows_per_step == seq_len
    bands = _window_bands(n)
    inv_count = np.broadcast_to(1.0 / bands.sum(axis=2, keepdims=True), (n_grp, n, POOL_GROUP_W))
    return pl.pallas_call(
        functools.partial(_pool_kernel, grid_mode=grid_mode),
        grid=(n_tok // rows_per_step, n_grp),
        in_specs=[
            pl.BlockSpec((rows_per_step, POOL_GROUP_W), lambda b, g: (b, p_blk0 + g)),
            pl.BlockSpec((rows_per_step, POOL_GROUP_W), lambda b, g: (b, g_blk0 + g)),
            pl.BlockSpec((1, n, n), lambda b, g: (g, 0, 0)),
            pl.BlockSpec((1, n, POOL_GROUP_W), lambda b, g: (g, 0, 0)),
            pl.BlockSpec((1, POOL_GROUP_W, POOL_GROUP_W), lambda b, g: (g, 0, 0)),
            pl.BlockSpec((1, POOL_GROUP_W), lambda b, g: (0, g)),
        ],
        out_specs=pl.BlockSpec((rows_per_step, POOL_GROUP_W), lambda b, g: (b, g)),
        out_shape=jax.ShapeDtypeStruct((n_tok, BRANCH_W), BF16),
        compiler_params=_cparams(2, "pool_mixer"),
        name="pool_mixer",
    )(proj, proj, jnp.asarray(bands, BF16), jnp.asarray(inv_count, F32), w_pool.astype(BF16),
      pool_scale.reshape(1, BRANCH_W))


def _hgrn_triangles():
    ones = np.ones((HGRN_CHUNK, HGRN_CHUNK), np.float32)
    return np.concatenate([np.tril(ones), np.triu(ones)], axis=0)


def _hgrn_partial_sum_matrix():
    c = HGRN_CHUNK
    blocks = []
    for m in HGRN_PSUM_LEVELS:
        mq = np.zeros((c, 2 * c), np.float32)
        mk = np.zeros((c, 2 * c), np.float32)
        for i in range(c):
            start = (i // (2 * m)) * (2 * m)
            r1, r2 = start + m - 1, start + m
            if i >= r2:
                mq[i, r2:i + 1] = 1
                mk[i, c + r2:c + i] = 1
            else:
                mq[i, c + i:c + r1 + 1] = 1
                mk[i, i + 1:r1 + 1] = 1
        blocks += [mq, mk]
    return np.concatenate(blocks, axis=0)


def _hgrn_level_index():
    i = np.arange(HGRN_CHUNK)
    x = i[:, None] ^ i[None, :]
    lv = np.zeros_like(x)
    nz = x > 0
    lv[nz] = np.floor(np.log2(x[nz])).astype(x.dtype) + 1
    return lv.astype(np.int32)


_HGRN_TRI = _hgrn_triangles()
_HGRN_PSUM = _hgrn_partial_sum_matrix()
_HGRN_LEVEL = _hgrn_level_index()

_NT = (((1,), (1,)), ((), ()))
_TN = (((0,), (0,)), ((), ()))


def _hgrn_kernel(*refs, n_chunks, heads, has_s0, has_prev, want_final, final_own):
    q_ref, ff_ref, fb_ref, v_ref, ga_ref, lb_ref, gon_ref, tri_ref, psum_ref, lvl_ref = refs[:10]
    pos = 10
    s0_ref = sfin_ref = None
    if has_s0:
        s0_ref = refs[pos]
        pos += 1
    if has_prev:
        pos += 1
    y_ref = refs[pos]
    pos += 1
    if want_final:
        sfin_ref = refs[pos]
        pos += 1
    o_scr, qfb_scr, dst_scr, dec_scr, st_scr = refs[pos:]

    c = HGRN_CHUNK
    inline = heads * n_chunks <= HGRN_IN_FLIGHT
    level = lvl_ref[...]
    row8 = jnp.concatenate([lax.broadcasted_iota(jnp.int32, (8, HEAD_D), 0)] * (c // 8), axis=0)

    def chunk_rows(ci):
        if isinstance(ci, int):
            return pl.ds(ci * c, c)
        return pl.ds(pl.multiple_of(ci * c, c), c)

    def over_chunks(body, init, unroll):
        if inline:
            carry = init
            for ci in range(n_chunks):
                carry = body(ci, carry)
            return carry
        return lax.fori_loop(0, n_chunks, body, init, unroll=unroll)

    def over_heads_and_chunks(fn, in_flight):
        def body(ci, carry):
            for hh in range(heads):
                fn(hh, ci, carry)
            return carry
        over_chunks(body, 0, unroll=max(1, min(n_chunks, in_flight // heads)))

    def hi_mid(x):
        hi = x.astype(BF16)
        return hi, (x - hi.astype(F32)).astype(BF16)

    def summed(zero_one, hi, mid):
        dd = jnp.dot(zero_one, jnp.concatenate([hi, mid], axis=1), preferred_element_type=F32)
        return dd[:, :HEAD_D] + dd[:, HEAD_D:]

    def local_pass(hh, ci, carry):
        rows = chunk_rows(ci)
        lanes = slice(hh * HEAD_D, (hh + 1) * HEAD_D)
        lb_f, lb_b = lb_ref[0:1, lanes], lb_ref[1:2, lanes]
        f_f = lb_f + (1.0 - lb_f) * jax.nn.sigmoid(ff_ref[rows, lanes])
        f_b = lb_b + (1.0 - lb_b) * jax.nn.sigmoid(fb_ref[rows, lanes])
        k_f, k_b = 1.0 - f_f, 1.0 - f_b
        hi_f, mid_f = hi_mid(jnp.log2(f_f))
        hi_b, mid_b = hi_mid(jnp.log2(f_b))
        b_f = summed(tri_ref[0:c, :], hi_f, mid_f)
        b_b = summed(tri_ref[c:2 * c, :], hi_b, mid_b)
        small = summed(psum_ref[...], jnp.concatenate([hi_f, hi_b], axis=0),
                       jnp.concatenate([mid_f, mid_b], axis=0))

        q = _silu(q_ref[rows, lanes].astype(F32))
        v = v_ref[rows, lanes]

        def level_scores(d_q, d_k, k_sel):
            qt = (q * jnp.exp2(d_q)).astype(BF16)
            kt = (k_sel * jnp.exp2(d_k)).astype(BF16)
            return lax.dot_general(qt, kt, _NT, preferred_element_type=F32)

        s = lax.dot_general(q.astype(BF16), (k_f + k_b).astype(BF16), _NT,
                            preferred_element_type=F32)
        s = jnp.where(level == 0, s, 0.0)
        odd = (row8 & 1) == 1
        q1 = (q * jnp.where(odd, f_f, f_b)).astype(BF16)
        k1 = jnp.where(odd, k_b, k_f).astype(BF16)
        s = jnp.where(level == 1, lax.dot_general(q1, k1, _NT, preferred_element_type=F32), s)
        for li, m in enumerate(HGRN_PSUM_LEVELS):
            second = (row8 & m) == m
            s_l = level_scores(small[2 * li * c:(2 * li + 1) * c], small[(2 * li + 1) * c:(2 * li + 2) * c],
                               jnp.where(second, k_b, k_f))
            s = jnp.where(level == m.bit_length(), s_l, s)
        for m in HGRN_LEVELS:
            dq_parts, dk_parts, k_parts = [], [], []
            for lo in range(0, c, 2 * m):
                mid_row, hi_row = lo + m, lo + 2 * m
                edge_f = b_f[mid_row - 1:mid_row]
                edge_b = b_b[mid_row:mid_row + 1]
                dq_parts += [b_b[lo:mid_row] - edge_b, b_f[mid_row:hi_row] - edge_f]
                dk_parts += [edge_f - b_f[lo:mid_row], edge_b - b_b[mid_row:hi_row]]
                k_parts += [k_f[lo:mid_row], k_b[mid_row:hi_row]]
            s_l = level_scores(jnp.concatenate(dq_parts, axis=0), jnp.concatenate(dk_parts, axis=0),
                               jnp.concatenate(k_parts, axis=0))
            s = jnp.where(level == m.bit_length(), s_l, s)

        q_fb = jnp.concatenate([q * jnp.exp2(b_f), q * jnp.exp2(b_b)], axis=1).astype(BF16)
        k_fb = jnp.concatenate([k_f * jnp.exp2(b_f[c - 1:c] - b_f),
                                k_b * jnp.exp2(b_b[0:1] - b_b)], axis=1).astype(BF16)
        dec = jnp.concatenate([jnp.exp2(b_f[c - 1:c]), jnp.exp2(b_b[0:1])], axis=1)

        o_scr[hh, rows, :] = jnp.dot(s.astype(BF16), v, preferred_element_type=F32)
        qfb_scr[hh, rows, :] = q_fb
        dst_scr[hh, ci] = lax.dot_general(v, k_fb, _TN, preferred_element_type=F32)
        dec_scr[hh, ci] = jnp.broadcast_to(dec, (8, 2 * HEAD_D))
        return carry

    over_heads_and_chunks(local_pass, HGRN_IN_FLIGHT)

    for hh in range(heads):
        if has_s0:
            starts = (s0_ref[0, 0, 0, hh].T, s0_ref[0, 0, 1, hh].T)
        else:
            starts = (jnp.zeros((HEAD_D, HEAD_D), F32),) * 2

        def scan_step(k, states, hh=hh):
            out = []
            for d, ci in ((0, k), (1, n_chunks - 1 - k)):
                lanes = slice(d * HEAD_D, (d + 1) * HEAD_D)
                st_scr[hh, ci, :, lanes] = states[d].astype(BF16)
                out.append(states[d] * dec_scr[hh, ci, 0:1, lanes] + dst_scr[hh, ci, :, lanes])
            return tuple(out)

        st_f, st_b = over_chunks(scan_step, starts, unroll=2)
        if want_final:
            sfin_ref[0, final_own, 0, hh] = st_f.T
            sfin_ref[0, final_own, 1, hh] = st_b.T
            for other in range(sfin_ref.shape[1]):
                if other != final_own:
                    sfin_ref[0, other, :, hh] = jnp.zeros((2, HEAD_D, HEAD_D), F32)

    def output_pass(hh, ci, carry):
        rows = chunk_rows(ci)
        lanes = slice(hh * HEAD_D, (hh + 1) * HEAD_D)
        o = o_scr[hh, rows, :] + lax.dot_general(qfb_scr[hh, rows, :], st_scr[hh, ci], _NT,
                                                 preferred_element_type=F32)
        y_ref[rows, lanes] = (_rms(o, gon_ref[...])
                              * ga_ref[rows, lanes].astype(F32)).astype(BF16)
        return carry

    over_heads_and_chunks(output_pass, HGRN_IN_FLIGHT // 2)


def _hgrn_mixer(gates, proj, lb, g_onorm, state, layer_j, seq_len, heads, want_final,
                final_prev=None):
    n_tok = proj.shape[0]
    n_seq = n_tok // seq_len
    n_chunks = seq_len // HGRN_CHUNK
    has_s0 = state is not None
    has_prev = final_prev is not None
    width = heads * HEAD_D
    blk = lambda part: pl.BlockSpec(
        (seq_len, width), lambda b, h: (b, part * (N_HEADS // heads) + h))
    in_specs = [blk(AB_PART["q"]), blk(0), blk(1), blk(AB_PART["i"]), blk(AB_PART["gate_a"]),
                pl.BlockSpec((2, width), lambda b, h: (0, h)),
                pl.BlockSpec((1, HEAD_D), lambda b, h: (0, 0)),
                pl.BlockSpec(_HGRN_TRI.shape, lambda b, h: (0, 0)),
                pl.BlockSpec(_HGRN_PSUM.shape, lambda b, h: (0, 0)),
                pl.BlockSpec(_HGRN_LEVEL.shape, lambda b, h: (0, 0))]
    args = [proj, gates, gates, proj, proj, lb, g_onorm.reshape(1, HEAD_D),
            jnp.asarray(_HGRN_TRI, BF16), jnp.asarray(_HGRN_PSUM, BF16), jnp.asarray(_HGRN_LEVEL)]
    state_blk = pl.BlockSpec((1, 1, 2, heads, HEAD_D, HEAD_D), lambda b, h: (b, layer_j, 0, h, 0, 0))
    if has_s0:
        in_specs.append(state_blk)
        args.append(state)
    aliases = {}
    if has_prev:
        aliases[len(args)] = 1
        in_specs.append(pl.BlockSpec(memory_space=pl.ANY))
        args.append(final_prev)
    out_specs = [pl.BlockSpec((seq_len, width), lambda b, h: (b, h))]
    out_shape = [jax.ShapeDtypeStruct((n_tok, BRANCH_W), BF16)]
    if want_final:
        out_specs.append(state_blk if has_prev else pl.BlockSpec(
            (1, N_AB, 2, heads, HEAD_D, HEAD_D), lambda b, h: (b, 0, 0, h, 0, 0)))
        out_shape.append(jax.ShapeDtypeStruct((n_seq, N_AB, 2, N_HEADS, HEAD_D, HEAD_D), F32))
    outs = pl.pallas_call(
        functools.partial(_hgrn_kernel, n_chunks=n_chunks, heads=heads, has_s0=has_s0,
                          has_prev=has_prev, want_final=want_final,
                          final_own=0 if has_prev else layer_j),
        grid=(n_seq, N_HEADS // heads),
        in_specs=in_specs,
        out_specs=out_specs,
        out_shape=out_shape,
        input_output_aliases=aliases,
        scratch_shapes=[
            pltpu.VMEM((heads, seq_len, HEAD_D), F32),
            pltpu.VMEM((heads, seq_len, 2 * HEAD_D), BF16),
            pltpu.VMEM((heads, n_chunks, HEAD_D, 2 * HEAD_D), F32),
            pltpu.VMEM((heads, n_chunks, 8, 2 * HEAD_D), F32),
            pltpu.VMEM((heads, n_chunks, HEAD_D, 2 * HEAD_D), BF16),
        ],
        compiler_params=_cparams(2, "hgrn_mixer"),
        name="hgrn_mixer",
    )(*args)
    return outs if want_final else (outs[0], None)


def kernel(x_prompt, x_sample, c, state_hgrn, c_ctx, w_ada, b_ada, g_pre, g_post, w_in_ab, w_out_ab,
           lb_logits, g_onorm_a, w_pool, pool_scale, w_in_c, w_out_c, ln_v_g, ln_v_b, w_spatial,
           b_spatial):
    n_batch, seq_p, _ = x_prompt.shape
    dec_batch, seq_s, _ = x_sample.shape
    assert dec_batch + 1 <= N_COND_ROWS and seq_s % TOKEN_TILE == 0 and POOL_ROWS % seq_p == 0

    cond = jnp.concatenate(
        [c_ctx[None, :], c, jnp.zeros((N_COND_ROWS - 1 - dec_batch, D_MODEL), F32)], axis=0)
    mods = _modulations(cond, w_ada, b_ada).reshape(DEPTH, N_COND_ROWS, 3, D_MODEL)
    lb_all = _lower_bounds(lb_logits)

    flows = [
        dict(x=x_prompt.reshape(n_batch * seq_p, D_MODEL), seq=seq_p, grid=False, state=None,
             row=lambda i: 0),
        dict(x=x_sample.reshape(dec_batch * seq_s, D_MODEL), seq=seq_s, grid=True, state=state_hgrn,
             row=lambda i: 1 + i // (seq_s // TOKEN_TILE)),
    ]
    w_in_ab, w_out_ab, w_in_c, w_out_c = (
        w.astype(BF16) for w in (w_in_ab, w_out_ab, w_in_c, w_out_c))
    new_state = None
    for l in range(DEPTH):
        j = l // 2
        if l % 2 == 0:
            for fi, fl in enumerate(flows):
                gates, proj = _in_proj(fl["x"], mods[l], g_pre[l], w_in_ab, j, fl["row"], TOKEN_TILE)
                heads = max(2, min(N_HEADS, HGRN_IN_FLIGHT * HGRN_CHUNK // fl["seq"]))
                if fi == 0:
                    y_a, new_state = _hgrn_mixer(gates, proj, lb_all[j], g_onorm_a[j], None, j,
                                                 fl["seq"], heads, want_final=True,
                                                 final_prev=new_state)
                else:
                    y_a, _ = _hgrn_mixer(gates, proj, lb_all[j], g_onorm_a[j], fl["state"], j,
                                         fl["seq"], heads, want_final=False)
                y_b = _pool_mixer(proj, w_pool[j], pool_scale[j], fl["seq"], fl["grid"], POOL_ROWS)
                fl["x"] = _out_proj([y_a, y_b], w_out_ab, j, fl["x"], mods[l], g_post[l],
                                    fl["row"], TOKEN_TILE)
        else:
            for fl in flows:
                fl["x"] = _layer_c(fl["x"], mods[l], g_pre[l], g_post[l], w_in_c, w_out_c, j,
                                   ln_v_g[j], ln_v_b[j], w_spatial[j], b_spatial[j],
                                   fl["row"], TOKEN_TILE)
    y_p = flows[0]["x"].reshape(n_batch, seq_p, D_MODEL)
    y_s = flows[1]["x"].reshape(dec_batch, seq_s, D_MODEL)
    return (y_p, y_s, new_state)
```

```python
import functools

import numpy as np
import jax
import jax.numpy as jnp
from jax import lax
from jax.experimental import pallas as pl
from jax.experimental.pallas import tpu as pltpu

F32 = jnp.float32
BF16 = jnp.bfloat16

D_MODEL = 1024
DEPTH = 4
N_AB = (DEPTH + 1) // 2
MIX_WIDTH = 2 * D_MODEL
BRANCH_W = MIX_WIDTH // 2
HEAD_D = 128
N_HEADS = BRANCH_W // HEAD_D
GRID_W = 64
POOL_WINDOWS = (2, 4, 8, 16)
POOL_GROUP_W = BRANCH_W // len(POOL_WINDOWS)
CHUNK_C = 128
N_GROUPS_C = 8
GROUP_W_C = MIX_WIDTH // N_GROUPS_C
AB_IN = 7 * BRANCH_W
C_IN = 3 * MIX_WIDTH
AB_F32_PARTS = 2
AB_PART = {"q": 0, "i": 1, "gate_a": 2, "pool_in": 3, "gate_b": 4}
EPS = 1e-6
N_COND_ROWS = 8

HGRN_CHUNK = 128
HGRN_LEVELS = (64, 32, 16, 8)
HGRN_PSUM_LEVELS = (2, 4)
HGRN_IN_FLIGHT = 32

TOKEN_TILE = 512
POOL_ROWS = 2048
VMEM_MIB = {"adaln_modulation": 24, "in_projection": 56, "out_projection": 40, "gmlp_layer": 48,
            "pool_mixer": 56, "hgrn_mixer": 48}
MIB = 1024 * 1024


def _cparams(n_axes, name):
    return pltpu.CompilerParams(
        dimension_semantics=("arbitrary",) * n_axes, vmem_limit_bytes=VMEM_MIB[name] * MIB)


def _resident(shape, layer=None):
    zeros = (0,) * len(shape)
    if layer is None:
        return pl.BlockSpec(shape, lambda *_: zeros, pipeline_mode=pl.Buffered(1))
    return pl.BlockSpec((None,) + tuple(shape), lambda *_: (layer,) + zeros,
                        pipeline_mode=pl.Buffered(1))


def _silu(x):
    return x * jax.nn.sigmoid(x)


def _rms(x, g):
    return x * lax.rsqrt(jnp.mean(x * x, axis=-1, keepdims=True) + EPS) * g


def _modulated_norm(x, mod, g):
    return _rms(x, g) * (1.0 + mod[1:2]) + mod[0:1]


def _mod_kernel(cond_ref, w_ref, b_ref, o_ref):
    a = _silu(cond_ref[...])
    o_ref[0] = jnp.dot(a.astype(BF16), w_ref[0].astype(BF16), preferred_element_type=F32) + b_ref[0]


def _modulations(cond, w_ada, b_ada):
    tn = 1024
    n3 = 3 * D_MODEL
    return pl.pallas_call(
        _mod_kernel,
        grid=(DEPTH, n3 // tn),
        in_specs=[
            pl.BlockSpec((N_COND_ROWS, D_MODEL), lambda l, j: (0, 0)),
            pl.BlockSpec((1, D_MODEL, tn), lambda l, j: (l, 0, j)),
            pl.BlockSpec((1, 1, tn), lambda l, j: (l, 0, j)),
        ],
        out_specs=pl.BlockSpec((1, N_COND_ROWS, tn), lambda l, j: (l, 0, j)),
        out_shape=jax.ShapeDtypeStruct((DEPTH, N_COND_ROWS, n3), F32),
        compiler_params=_cparams(2, "adaln_modulation"),
        name="adaln_modulation",
    )(cond, w_ada, b_ada.reshape(DEPTH, 1, n3))


def _lb_kernel(x_ref, o_ref):
    x = x_ref[...]
    e = jnp.exp(x - jnp.max(x, axis=0, keepdims=True))
    p = e / jnp.sum(e, axis=0, keepdims=True)
    run = p[0]
    o_ref[0] = run - p[0]
    for l in range(1, x.shape[0]):
        run = run + p[l]
        o_ref[l] = run - p[0]


def _lower_bounds(lb_logits):
    return pl.pallas_call(
        _lb_kernel, out_shape=jax.ShapeDtypeStruct(lb_logits.shape, F32), name="hgrn_lower_bounds",
    )(lb_logits)


def _in_kernel(x_ref, mod_ref, g_ref, w_ref, gates_ref, proj_ref):
    h = _modulated_norm(x_ref[...], mod_ref[0], g_ref[...]).astype(BF16)

    def part(k):
        return jnp.dot(h, w_ref[:, k * BRANCH_W:(k + 1) * BRANCH_W], preferred_element_type=F32)

    for slot, k in enumerate((1, 2)):
        gates_ref[:, slot * BRANCH_W:(slot + 1) * BRANCH_W] = part(k)
    for slot, k in enumerate((0, 3, 4, 5, 6)):
        proj_ref[:, slot * BRANCH_W:(slot + 1) * BRANCH_W] = part(k).astype(BF16)


def _in_proj(x, mod, g, w, layer, row_of_tile, tm):
    n_tok = x.shape[0]
    n_gate, n_proj = AB_F32_PARTS * BRANCH_W, AB_IN - AB_F32_PARTS * BRANCH_W
    return pl.pallas_call(
        _in_kernel,
        grid=(n_tok // tm,),
        in_specs=[
            pl.BlockSpec((tm, D_MODEL), lambda i: (i, 0)),
            pl.BlockSpec((1, 3, D_MODEL), lambda i: (row_of_tile(i), 0, 0)),
            _resident((1, D_MODEL)),
            _resident((D_MODEL, AB_IN), layer),
        ],
        out_specs=[pl.BlockSpec((tm, n_gate), lambda i: (i, 0)),
                   pl.BlockSpec((tm, n_proj), lambda i: (i, 0))],
        out_shape=[jax.ShapeDtypeStruct((n_tok, n_gate), F32),
                   jax.ShapeDtypeStruct((n_tok, n_proj), BF16)],
        compiler_params=_cparams(1, "in_projection"),
        name="in_projection",
    )(x, mod, g.reshape(1, D_MODEL), w)


def _out_kernel(*refs, n_parts):
    y_refs, w_refs = refs[:n_parts], refs[n_parts:2 * n_parts]
    x_ref, mod_ref, g_ref, o_ref = refs[2 * n_parts:]
    acc = jnp.dot(y_refs[0][...], w_refs[0][...], preferred_element_type=F32)
    for y_ref, w_ref in zip(y_refs[1:], w_refs[1:]):
        acc = acc + jnp.dot(y_ref[...], w_ref[...], preferred_element_type=F32)
    o_ref[...] = x_ref[...] + mod_ref[0][2:3] * _rms(acc, g_ref[...])


def _out_proj(ys, w, layer, x, mod, g, row_of_tile, tm):
    n_tok = x.shape[0]
    n_parts = len(ys)
    width = ys[0].shape[1]
    in_specs = [pl.BlockSpec((tm, width), lambda i: (i, 0)) for _ in ys]
    in_specs += [pl.BlockSpec((None, width, D_MODEL), lambda i, k=k: (layer, k, 0))
                 for k in range(n_parts)]
    in_specs += [
        pl.BlockSpec((tm, D_MODEL), lambda i: (i, 0)),
        pl.BlockSpec((1, 3, D_MODEL), lambda i: (row_of_tile(i), 0, 0)),
        pl.BlockSpec((1, D_MODEL), lambda i: (0, 0)),
    ]
    return pl.pallas_call(
        functools.partial(_out_kernel, n_parts=n_parts),
        grid=(n_tok // tm,),
        in_specs=in_specs,
        out_specs=pl.BlockSpec((tm, D_MODEL), lambda i: (i, 0)),
        out_shape=jax.ShapeDtypeStruct((n_tok, D_MODEL), F32),
        compiler_params=_cparams(1, "out_projection"),
        name="out_projection",
    )(*ys, *([w] * n_parts), x, mod, g.reshape(1, D_MODEL))


def _layer_c_kernel(x_ref, mod_ref, gpre_ref, win_ref, lng_ref, lnb_ref, ws_ref, bs_ref, wout_ref,
                    gpost_ref, o_ref, y_scr, *, tm):
    x = x_ref[...]
    mod = mod_ref[0]
    h = _modulated_norm(x, mod, gpre_ref[...]).astype(BF16)

    v = jnp.dot(h, win_ref[:, MIX_WIDTH:2 * MIX_WIDTH], preferred_element_type=F32)
    mu = jnp.mean(v, axis=-1, keepdims=True)
    vc = v - mu
    var = jnp.mean(vc * vc, axis=-1, keepdims=True)
    vn = (vc * lax.rsqrt(var + EPS) * lng_ref[...] + lnb_ref[...]).astype(BF16)

    for gi in range(N_GROUPS_C):
        cols = slice(gi * GROUP_W_C, (gi + 1) * GROUP_W_C)
        u = jnp.dot(h, win_ref[:, cols], preferred_element_type=F32)
        gate = _silu(jnp.dot(h, win_ref[:, 2 * MIX_WIDTH + gi * GROUP_W_C:
                                        2 * MIX_WIDTH + (gi + 1) * GROUP_W_C],
                             preferred_element_type=F32))
        for n in range(tm // CHUNK_C):
            rows = slice(n * CHUNK_C, (n + 1) * CHUNK_C)
            sp = jnp.dot(ws_ref[gi], vn[rows, cols], preferred_element_type=F32) + bs_ref[:, gi:gi + 1]
            y_scr[rows, cols] = (u[rows] * sp * gate[rows]).astype(BF16)

    acc = jnp.dot(y_scr[...], wout_ref[...], preferred_element_type=F32)
    o_ref[...] = x + mod[2:3] * _rms(acc, gpost_ref[...])


def _layer_c(x, mod, g_pre, g_post, w_in, w_out, layer, ln_g, ln_b, w_s, b_s, row_of_tile, tm):
    n_tok = x.shape[0]
    return pl.pallas_call(
        functools.partial(_layer_c_kernel, tm=tm),
        grid=(n_tok // tm,),
        in_specs=[
            pl.BlockSpec((tm, D_MODEL), lambda i: (i, 0)),
            pl.BlockSpec((1, 3, D_MODEL), lambda i: (row_of_tile(i), 0, 0)),
            _resident((1, D_MODEL)),
            _resident((D_MODEL, C_IN), layer),
            _resident((1, MIX_WIDTH)),
            _resident((1, MIX_WIDTH)),
            _resident((N_GROUPS_C, CHUNK_C, CHUNK_C)),
            _resident((CHUNK_C, N_GROUPS_C)),
            _resident((MIX_WIDTH, D_MODEL), layer),
            _resident((1, D_MODEL)),
        ],
        out_specs=pl.BlockSpec((tm, D_MODEL), lambda i: (i, 0)),
        out_shape=jax.ShapeDtypeStruct((n_tok, D_MODEL), F32),
        scratch_shapes=[pltpu.VMEM((tm, MIX_WIDTH), BF16)],
        compiler_params=_cparams(1, "gmlp_layer"),
        name="gmlp_layer",
    )(x, mod, g_pre.reshape(1, D_MODEL), w_in, ln_g.reshape(1, MIX_WIDTH), ln_b.reshape(1, MIX_WIDTH),
      w_s.astype(BF16), b_s.T, w_out, g_post.reshape(1, D_MODEL))


def _window_counts(n, w):
    return [min(i - w // 2 + w, n) - max(i - w // 2, 0) for i in range(n)]


def _window_bands(n):
    bands = np.zeros((len(POOL_WINDOWS), n, n), np.float32)
    for k, w in enumerate(POOL_WINDOWS):
        for i in range(n):
            bands[k, i, max(i - w // 2, 0):min(i - w // 2 + w, n)] = 1
    return bands


def _mean_minor(x, band, inv_count):
    n = band.shape[0]
    ch = x.shape[1]
    groups = x.shape[0] // n
    wide = jnp.concatenate([x[i * n:(i + 1) * n, :] for i in range(groups)], axis=1)
    sums = jnp.dot(band, wide, preferred_element_type=F32)
    return [sums[:, i * ch:(i + 1) * ch] * inv_count for i in range(groups)]


def _mean_major(x3, w):
    n = x3.shape[0]
    zeros = lambda k: jnp.zeros((k,) + x3.shape[1:], F32)
    back = lambda a, k: jnp.concatenate([zeros(k), a[:n - k]], axis=0)
    ahead = lambda a, k: jnp.concatenate([a[k:], zeros(k)], axis=0)
    trail, lead = x3, x3
    size = 1
    while size < w // 2:
        trail = trail + back(trail, size)
        lead = lead + ahead(lead, size)
        size *= 2
    s = back(trail, 1) + lead
    return jnp.concatenate(
        [s[i:i + 1] * (1.0 / cnt) for i, cnt in enumerate(_window_counts(n, w))], axis=0)


def _pool_kernel(p_ref, gate_ref, band_ref, inv_ref, wp_ref, ps_ref, y_ref, *, grid_mode):
    grp = pl.program_id(1)
    rows, ch = p_ref.shape
    means = _mean_minor(p_ref[...], band_ref[0], inv_ref[0])

    def finish(m):
        dlt = (m - p_ref[...].astype(F32)).astype(BF16)
        y = jnp.dot(dlt, wp_ref[0], preferred_element_type=F32) * ps_ref[...]
        y_ref[...] = (y * _silu(gate_ref[...].astype(F32))).astype(BF16)

    if grid_mode:
        m3 = jnp.stack(means, axis=0)
        for k, w in enumerate(POOL_WINDOWS):
            pl.when(grp == k)(lambda w=w: finish(_mean_major(m3, w).reshape(rows, ch)))
    else:
        finish(jnp.concatenate(means, axis=0))


def _pool_mixer(proj, w_pool, pool_scale, seq_len, grid_mode, rows_per_step):
    n_tok = proj.shape[0]
    n_grp = len(POOL_WINDOWS)
    p_blk0 = AB_PART["pool_in"] * BRANCH_W // POOL_GROUP_W
    g_blk0 = AB_PART["gate_b"] * BRANCH_W // POOL_GROUP_W
    n = GRID_W if grid_mode else seq_len
    assert not grid_mode or rows_per_step == seq_len
    bands = _window_bands(n)
    inv_count = np.broadcast_to(1.0 / bands.sum(axis=2, keepdims=True), (n_grp, n, POOL_GROUP_W))
    return pl.pallas_call(
        functools.partial(_pool_kernel, grid_mode=grid_mode),
        grid=(n_tok // rows_per_step, n_grp),
        in_specs=[
            pl.BlockSpec((rows_per_step, POOL_GROUP_W), lambda b, g: (b, p_blk0 + g)),
            pl.BlockSpec((rows_per_step, POOL_GROUP_W), lambda b, g: (b, g_blk0 + g)),
            pl.BlockSpec((1, n, n), lambda b, g: (g, 0, 0)),
            pl.BlockSpec((1, n, POOL_GROUP_W), lambda b, g: (g, 0, 0)),
            pl.BlockSpec((1, POOL_GROUP_W, POOL_GROUP_W), lambda b, g: (g, 0, 0)),
            pl.BlockSpec((1, POOL_GROUP_W), lambda b, g: (0, g)),
        ],
        out_specs=pl.BlockSpec((rows_per_step, POOL_GROUP_W), lambda b, g: (b, g)),
        out_shape=jax.ShapeDtypeStruct((n_tok, BRANCH_W), BF16),
        compiler_params=_cparams(2, "pool_mixer"),
        name="pool_mixer",
    )(proj, proj, jnp.asarray(bands, BF16), jnp.asarray(inv_count, F32), w_pool.astype(BF16),
      pool_scale.reshape(1, BRANCH_W))


def _hgrn_triangles():
    ones = np.ones((HGRN_CHUNK, HGRN_CHUNK), np.float32)
    return np.concatenate([np.tril(ones), np.triu(ones)], axis=0)


def _hgrn_partial_sum_matrix():
    c = HGRN_CHUNK
    blocks = []
    for m in HGRN_PSUM_LEVELS:
        mq = np.zeros((c, 2 * c), np.float32)
        mk = np.zeros((c, 2 * c), np.float32)
        for i in range(c):
            start = (i // (2 * m)) * (2 * m)
            r1, r2 = start + m - 1, start + m
            if i >= r2:
                mq[i, r2:i + 1] = 1
                mk[i, c + r2:c + i] = 1
            else:
                mq[i, c + i:c + r1 + 1] = 1
                mk[i, i + 1:r1 + 1] = 1
        blocks += [mq, mk]
    return np.concatenate(blocks, axis=0)


def _hgrn_level_index():
    i = np.arange(HGRN_CHUNK)
    x = i[:, None] ^ i[None, :]
    lv = np.zeros_like(x)
    nz = x > 0
    lv[nz] = np.floor(np.log2(x[nz])).astype(x.dtype) + 1
    return lv.astype(np.int32)


_HGRN_TRI = _hgrn_triangles()
_HGRN_PSUM = _hgrn_partial_sum_matrix()
_HGRN_LEVEL = _hgrn_level_index()

_NT = (((1,), (1,)), ((), ()))
_TN = (((0,), (0,)), ((), ()))


def _hgrn_kernel(*refs, n_chunks, heads, has_s0, has_prev, want_final, final_own):
    q_ref, ff_ref, fb_ref, v_ref, ga_ref, lb_ref, gon_ref, tri_ref, psum_ref, lvl_ref = refs[:10]
    pos = 10
    s0_ref = sfin_ref = None
    if has_s0:
        s0_ref = refs[pos]
        pos += 1
    if has_prev:
        pos += 1
    y_ref = refs[pos]
    pos += 1
    if want_final:
        sfin_ref = refs[pos]
        pos += 1
    o_scr, qfb_scr, dst_scr, dec_scr, st_scr = refs[pos:]

    c = HGRN_CHUNK
    inline = heads * n_chunks <= HGRN_IN_FLIGHT
    level = lvl_ref[...]
    row8 = jnp.concatenate([lax.broadcasted_iota(jnp.int32, (8, HEAD_D), 0)] * (c // 8), axis=0)

    def chunk_rows(ci):
        if isinstance(ci, int):
            return pl.ds(ci * c, c)
        return pl.ds(pl.multiple_of(ci * c, c), c)

    def over_chunks(body, init, unroll):
        if inline:
            carry = init
            for ci in range(n_chunks):
                carry = body(ci, carry)
            return carry
        return lax.fori_loop(0, n_chunks, body, init, unroll=unroll)

    def over_heads_and_chunks(fn, in_flight):
        def body(ci, carry):
            for hh in range(heads):
                fn(hh, ci, carry)
            return carry
        over_chunks(body, 0, unroll=max(1, min(n_chunks, in_flight // heads)))

    def hi_mid(x):
        hi = x.astype(BF16)
        return hi, (x - hi.astype(F32)).astype(BF16)

    def summed(zero_one, hi, mid):
        dd = jnp.dot(zero_one, jnp.concatenate([hi, mid], axis=1), preferred_element_type=F32)
        return dd[:, :HEAD_D] + dd[:, HEAD_D:]

    def local_pass(hh, ci, carry):
        rows = chunk_rows(ci)
        lanes = slice(hh * HEAD_D, (hh + 1) * HEAD_D)
        lb_f, lb_b = lb_ref[0:1, lanes], lb_ref[1:2, lanes]
        f_f = lb_f + (1.0 - lb_f) * jax.nn.sigmoid(ff_ref[rows, lanes])
        f_b = lb_b + (1.0 - lb_b) * jax.nn.sigmoid(fb_ref[rows, lanes])
        k_f, k_b = 1.0 - f_f, 1.0 - f_b
        hi_f, mid_f = hi_mid(jnp.log2(f_f))
        hi_b, mid_b = hi_mid(jnp.log2(f_b))
        b_f = summed(tri_ref[0:c, :], hi_f, mid_f)
        b_b = summed(tri_ref[c:2 * c, :], hi_b, mid_b)
        small = summed(psum_ref[...], jnp.concatenate([hi_f, hi_b], axis=0),
                       jnp.concatenate([mid_f, mid_b], axis=0))

        q = _silu(q_ref[rows, lanes].astype(F32))
        v = v_ref[rows, lanes]

        def level_scores(d_q, d_k, k_sel):
            qt = (q * jnp.exp2(d_q)).astype(BF16)
            kt = (k_sel * jnp.exp2(d_k)).astype(BF16)
            return lax.dot_general(qt, kt, _NT, preferred_element_type=F32)

        s = lax.dot_general(q.astype(BF16), (k_f + k_b).astype(BF16), _NT,
                            preferred_element_type=F32)
        s = jnp.where(level == 0, s, 0.0)
        odd = (row8 & 1) == 1
        q1 = (q * jnp.where(odd, f_f, f_b)).astype(BF16)
        k1 = jnp.where(odd, k_b, k_f).astype(BF16)
        s = jnp.where(level == 1, lax.dot_general(q1, k1, _NT, preferred_element_type=F32), s)
        for li, m in enumerate(HGRN_PSUM_LEVELS):
            second = (row8 & m) == m
            s_l = level_scores(small[2 * li * c:(2 * li + 1) * c], small[(2 * li + 1) * c:(2 * li + 2) * c],
                               jnp.where(second, k_b, k_f))
            s = jnp.where(level == m.bit_length(), s_l, s)
        for m in HGRN_LEVELS:
            dq_parts, dk_parts, k_parts = [], [], []
            for lo in range(0, c, 2 * m):
                mid_row, hi_row = lo + m, lo + 2 * m
                edge_f = b_f[mid_row - 1:mid_row]
                edge_b = b_b[mid_row:mid_row + 1]
                dq_parts += [b_b[lo:mid_row] - edge_b, b_f[mid_row:hi_row] - edge_f]
                dk_parts += [edge_f - b_f[lo:mid_row], edge_b - b_b[mid_row:hi_row]]
                k_parts += [k_f[lo:mid_row], k_b[mid_row:hi_row]]
            s_l = level_scores(jnp.concatenate(dq_parts, axis=0), jnp.concatenate(dk_parts, axis=0),
                               jnp.concatenate(k_parts, axis=0))
            s = jnp.where(level == m.bit_length(), s_l, s)

        q_fb = jnp.concatenate([q * jnp.exp2(b_f), q * jnp.exp2(b_b)], axis=1).astype(BF16)
        k_fb = jnp.concatenate([k_f * jnp.exp2(b_f[c - 1:c] - b_f),
                                k_b * jnp.exp2(b_b[0:1] - b_b)], axis=1).astype(BF16)
        dec = jnp.concatenate([jnp.exp2(b_f[c - 1:c]), jnp.exp2(b_b[0:1])], axis=1)

        o_scr[hh, rows, :] = jnp.dot(s.astype(BF16), v, preferred_element_type=F32)
        qfb_scr[hh, rows, :] = q_fb
        dst_scr[hh, ci] = lax.dot_general(v, k_fb, _TN, preferred_element_type=F32)
        dec_scr[hh, ci] = jnp.broadcast_to(dec, (8, 2 * HEAD_D))
        return carry

    over_heads_and_chunks(local_pass, HGRN_IN_FLIGHT)

    for hh in range(heads):
        if has_s0:
            starts = (s0_ref[0, 0, 0, hh].T, s0_ref[0, 0, 1, hh].T)
        else:
            starts = (jnp.zeros((HEAD_D, HEAD_D), F32),) * 2

        def scan_step(k, states, hh=hh):
            out = []
            for d, ci in ((0, k), (1, n_chunks - 1 - k)):
                lanes = slice(d * HEAD_D, (d + 1) * HEAD_D)
                st_scr[hh, ci, :, lanes] = states[d].astype(BF16)
                out.append(states[d] * dec_scr[hh, ci, 0:1, lanes] + dst_scr[hh, ci, :, lanes])
            return tuple(out)

        st_f, st_b = over_chunks(scan_step, starts, unroll=2)
        if want_final:
            sfin_ref[0, final_own, 0, hh] = st_f.T
            sfin_ref[0, final_own, 1, hh] = st_b.T
            for other in range(sfin_ref.shape[1]):
                if other != final_own:
                    sfin_ref[0, other, :, hh] = jnp.zeros((2, HEAD_D, HEAD_D), F32)

    def output_pass(hh, ci, carry):
        rows = chunk_rows(ci)
        lanes = slice(hh * HEAD_D, (hh + 1) * HEAD_D)
        o = o_scr[hh, rows, :] + lax.dot_general(qfb_scr[hh, rows, :], st_scr[hh, ci], _NT,
                                                 preferred_element_type=F32)
        y_ref[rows, lanes] = (_rms(o, gon_ref[...])
                              * _silu(ga_ref[rows, lanes].astype(F32))).astype(BF16)
        return carry

    over_heads_and_chunks(output_pass, HGRN_IN_FLIGHT // 2)


def _hgrn_mixer(gates, proj, lb, g_onorm, state, layer_j, seq_len, heads, want_final,
                final_prev=None):
    n_tok = proj.shape[0]
    n_seq = n_tok // seq_len
    n_chunks = seq_len // HGRN_CHUNK
    has_s0 = state is not None
    has_prev = final_prev is not None
    width = heads * HEAD_D
    blk = lambda part: pl.BlockSpec(
        (seq_len, width), lambda b, h: (b, part * (N_HEADS // heads) + h))
    in_specs = [blk(AB_PART["q"]), blk(0), blk(1), blk(AB_PART["i"]), blk(AB_PART["gate_a"]),
                pl.BlockSpec((2, width), lambda b, h: (0, h)),
                pl.BlockSpec((1, HEAD_D), lambda b, h: (0, 0)),
                pl.BlockSpec(_HGRN_TRI.shape, lambda b, h: (0, 0)),
                pl.BlockSpec(_HGRN_PSUM.shape, lambda b, h: (0, 0)),
                pl.BlockSpec(_HGRN_LEVEL.shape, lambda b, h: (0, 0))]
    args = [proj, gates, gates, proj, proj, lb, g_onorm.reshape(1, HEAD_D),
            jnp.asarray(_HGRN_TRI, BF16), jnp.asarray(_HGRN_PSUM, BF16), jnp.asarray(_HGRN_LEVEL)]
    state_blk = pl.BlockSpec((1, 1, 2, heads, HEAD_D, HEAD_D), lambda b, h: (b, layer_j, 0, h, 0, 0))
    if has_s0:
        in_specs.append(state_blk)
        args.append(state)
    aliases = {}
    if has_prev:
        aliases[len(args)] = 1
        in_specs.append(pl.BlockSpec(memory_space=pl.ANY))
        args.append(final_prev)
    out_specs = [pl.BlockSpec((seq_len, width), lambda b, h: (b, h))]
    out_shape = [jax.ShapeDtypeStruct((n_tok, BRANCH_W), BF16)]
    if want_final:
        out_specs.append(state_blk if has_prev else pl.BlockSpec(
            (1, N_AB, 2, heads, HEAD_D, HEAD_D), lambda b, h: (b, 0, 0, h, 0, 0)))
        out_shape.append(jax.ShapeDtypeStruct((n_seq, N_AB, 2, N_HEADS, HEAD_D, HEAD_D), F32))
    outs = pl.pallas_call(
        functools.partial(_hgrn_kernel, n_chunks=n_chunks, heads=heads, has_s0=has_s0,
                          has_prev=has_prev, want_final=want_final,
                          final_own=0 if has_prev else layer_j),
        grid=(n_seq, N_HEADS // heads),
        in_specs=in_specs,
        out_specs=out_specs,
        out_shape=out_shape,
        input_output_aliases=aliases,
        scratch_shapes=[
            pltpu.VMEM((heads, seq_len, HEAD_D), F32),
            pltpu.VMEM((heads, seq_len, 2 * HEAD_D), BF16),
            pltpu.VMEM((heads, n_chunks, HEAD_D, 2 * HEAD_D), F32),
            pltpu.VMEM((heads, n_chunks, 8, 2 * HEAD_D), F32),
            pltpu.VMEM((heads, n_chunks, HEAD_D, 2 * HEAD_D), BF16),
        ],
        compiler_params=_cparams(2, "hgrn_mixer"),
        name="hgrn_mixer",
    )(*args)
    return outs if want_final else (outs[0], None)


def kernel(x_prompt, x_sample, c, state_hgrn, c_ctx, w_ada, b_ada, g_pre, g_post, w_in_ab, w_out_ab,
           lb_logits, g_onorm_a, w_pool, pool_scale, w_in_c, w_out_c, ln_v_g, ln_v_b, w_spatial,
           b_spatial):
    n_batch, seq_p, _ = x_prompt.shape
    dec_batch, seq_s, _ = x_sample.shape
    assert dec_batch + 1 <= N_COND_ROWS and seq_s % TOKEN_TILE == 0 and POOL_ROWS % seq_p == 0

    cond = jnp.concatenate(
        [c_ctx[None, :], c, jnp.zeros((N_COND_ROWS - 1 - dec_batch, D_MODEL), F32)], axis=0)
    mods = _modulations(cond, w_ada, b_ada).reshape(DEPTH, N_COND_ROWS, 3, D_MODEL)
    lb_all = _lower_bounds(lb_logits)

    flows = [
        dict(x=x_prompt.reshape(n_batch * seq_p, D_MODEL), seq=seq_p, grid=False, state=None,
             row=lambda i: 0),
        dict(x=x_sample.reshape(dec_batch * seq_s, D_MODEL), seq=seq_s, grid=True, state=state_hgrn,
             row=lambda i: 1 + i // (seq_s // TOKEN_TILE)),
    ]
    w_in_ab, w_out_ab, w_in_c, w_out_c = (
        w.astype(BF16) for w in (w_in_ab, w_out_ab, w_in_c, w_out_c))
    new_state = None
    for l in range(DEPTH):
        j = l // 2
        if l % 2 == 0:
            for fi, fl in enumerate(flows):
                gates, proj = _in_proj(fl["x"], mods[l], g_pre[l], w_in_ab, j, fl["row"], TOKEN_TILE)
                heads = max(2, min(N_HEADS, HGRN_IN_FLIGHT * HGRN_CHUNK // fl["seq"]))
                if fi == 0:
                    y_a, new_state = _hgrn_mixer(gates, proj, lb_all[j], g_onorm_a[j], None, j,
                                                 fl["seq"], heads, want_final=True,
                                                 final_prev=new_state)
                else:
                    y_a, _ = _hgrn_mixer(gates, proj, lb_all[j], g_onorm_a[j], fl["state"], j,
                                         fl["seq"], heads, want_final=False)
                y_b = _pool_mixer(proj, w_pool[j], pool_scale[j], fl["seq"], fl["grid"], POOL_ROWS)
                fl["x"] = _out_proj([y_a, y_b], w_out_ab, j, fl["x"], mods[l], g_post[l],
                                    fl["row"], TOKEN_TILE)
        else:
            for fl in flows:
                fl["x"] = _layer_c(fl["x"], mods[l], g_pre[l], g_post[l], w_in_c, w_out_c, j,
                                   ln_v_g[j], ln_v_b[j], w_spatial[j], b_spatial[j],
                                   fl["row"], TOKEN_TILE)
    y_p = flows[0]["x"].reshape(n_batch, seq_p, D_MODEL)
    y_s = flows[1]["x"].reshape(dec_batch, seq_s, D_MODEL)
    return (y_p, y_s, new_state)
```
